```python
import math
import jax, jax.numpy as jnp
from jax import lax
import numpy as np

D_MODEL = 2048
BATCH = 4
SEQ = 2048
DEPTH = 2
DEC_BATCH = 16
DEC_SEQ = 16
PAST_LEN = 2048

CHUNK = 64
Q_BLOCK = 128
EPS = 1e-6
NEG_INF = -1e30

H_A = 4
A_NOPE = 128
A_ROPE = 64
A_VDIM = 128
A_Q_RANK = 512
A_KV_RANK = 256
ROPE_THETA = 10000.0
H_B = 4
B_DH = 64
H_C = 4
C_DH = 128
BAND_CHUNKS = 8
BAND_ROWS = BAND_CHUNKS * CHUNK
REL_CLIP = 128
H_D = 4
D_DH = 128
N_BRANCH = 4
BRANCH_W = H_A * A_VDIM
D_FF = 4 * D_MODEL

A_COLS = A_Q_RANK + A_KV_RANK + A_ROPE
B_COLS = 3 * H_B * 2 * B_DH
C_COLS = 3 * H_C * C_DH
D_COLS = 3 * H_D * D_DH + H_D
G_COLS = N_BRANCH * D_MODEL
IN_COLS = A_COLS + B_COLS + C_COLS + D_COLS + G_COLS
IN_SPLITS = [A_COLS, A_COLS + B_COLS, A_COLS + B_COLS + C_COLS, A_COLS + B_COLS + C_COLS + D_COLS]

kernel_name = 'hybrid_streaming_encoder_step'


def rms_norm(x, g):
    xf = x.astype(jnp.float32)
    y = xf * lax.rsqrt(jnp.mean(xf * xf, axis=-1, keepdims=True) + EPS)
    return (y * g.astype(jnp.float32)).astype(x.dtype)


def rope(x, pos):
    half = A_ROPE // 2
    inv = ROPE_THETA ** (-jnp.arange(half, dtype=jnp.float32) / half)
    ang = pos.astype(jnp.float32)[:, None] * inv[None, :]
    shape = (pos.shape[0],) + (1,) * (x.ndim - 3) + (half,)
    cos, sin = jnp.cos(ang).reshape(shape), jnp.sin(ang).reshape(shape)
    xf = x.astype(jnp.float32)
    x1, x2 = xf[..., :half], xf[..., half:]
    return jnp.concatenate([x1 * cos - x2 * sin, x2 * cos + x1 * sin], axis=-1).astype(x.dtype)


def masked_softmax(s, mask):
    return jax.nn.softmax(jnp.where(mask, s, NEG_INF), axis=-1)


def chunk_causal(qpos, kpos):
    return (kpos[None, :] // CHUNK) <= (qpos[:, None] // CHUNK)


def _rows(a, start):
    return lax.dynamic_slice_in_dim(a, start, Q_BLOCK, axis=1)


def sweep_query_blocks(fn, n_q):
    starts = jnp.arange(n_q // Q_BLOCK, dtype=jnp.int32) * Q_BLOCK
    outs = lax.map(fn, starts)

    def unblock(o):
        o = jnp.moveaxis(o, 0, 1)
        return o.reshape((o.shape[0], n_q) + o.shape[3:])

    return jax.tree_util.tree_map(unblock, outs)


def mla_core(q, k, v, qpos, kpos):
    s = jnp.einsum('bqhd,bkhd->bhqk', q, k).astype(jnp.float32) * (A_NOPE + A_ROPE) ** -0.5
    p = masked_softmax(s, chunk_causal(qpos, kpos))
    return jnp.einsum('bhqk,bkhd->bqhd', p.astype(v.dtype), v)


def diff_core(q, k, v, qpos, kpos, lam, slopes):
    scale = B_DH ** -0.5
    dist = jnp.abs(qpos[:, None] - kpos[None, :]).astype(jnp.float32)
    bias = -slopes[:, None, None] * dist[None]
    mask = chunk_causal(qpos, kpos)
    s1 = jnp.einsum('bqhd,bkhd->bhqk', q[..., :B_DH], k[..., :B_DH]).astype(jnp.float32) * scale + bias
    s2 = jnp.einsum('bqhd,bkhd->bhqk', q[..., B_DH:], k[..., B_DH:]).astype(jnp.float32) * scale + bias
    p = masked_softmax(s1, mask) - lam * masked_softmax(s2, mask)
    return jnp.einsum('bhqk,bkhd->bqhd', p.astype(v.dtype), v)


def band_core(q, k, v, qpos, kpos, rel_table):
    s = jnp.einsum('bnqhd,bnkhd->bhnqk', q, k).astype(jnp.float32) * C_DH ** -0.5
    rel = jnp.clip(qpos[:, :, None] - kpos[:, None, :], -REL_CLIP, REL_CLIP) + REL_CLIP
    s = s + rel_table.astype(jnp.float32)[:, rel]
    qc = (qpos // CHUNK)[:, :, None]
    kc = (kpos // CHUNK)[:, None, :]
    mask = (kpos[:, None, :] >= 0) & (kc <= qc) & (kc >= qc - BAND_CHUNKS)
    p = masked_softmax(s, mask)
    return jnp.einsum('bhnqk,bnkhd->bnqhd', p.astype(v.dtype), v)


def forget_core(q, k, v, fq, fk, qpos, kpos):
    s = jnp.einsum('bqhd,bkhd->bhqk', q, k).astype(jnp.float32) * D_DH ** -0.5
    s = s + (jnp.swapaxes(fq, 1, 2)[..., :, None] - jnp.swapaxes(fk, 1, 2)[..., None, :])
    p = masked_softmax(s, kpos[None, :] <= qpos[:, None])
    return jnp.einsum('bhqk,bkhd->bqhd', p.astype(v.dtype), v)


def band_prompt(q, k, v, rel_table):
    b, s, h, d = q.shape
    n_chunks = s // CHUNK
    band = BAND_ROWS + CHUNK
    pad = ((0, 0), (BAND_ROWS, 0), (0, 0), (0, 0))
    idx = jnp.arange(n_chunks, dtype=jnp.int32)[:, None] * CHUNK + jnp.arange(band, dtype=jnp.int32)[None, :]
    kb = jnp.pad(k, pad)[:, idx]
    vb = jnp.pad(v, pad)[:, idx]
    qpos = jnp.arange(s, dtype=jnp.int32).reshape(n_chunks, CHUNK)
    out = band_core(q.reshape(b, n_chunks, CHUNK, h, d), kb, vb, qpos, idx - BAND_ROWS, rel_table)
    return out.reshape(b, s, h, d)


def project(h, pos, lp):
    b, t = h.shape[:2]
    z = h @ lp['w_in']
    za, zb, zc, zd, zg = jnp.split(z, IN_SPLITS, axis=-1)
    cq, ckv, kpe = jnp.split(za, [A_Q_RANK, A_Q_RANK + A_KV_RANK], axis=-1)
    qa = (rms_norm(cq, lp['a_q_norm_g']) @ lp['w_a_uq']).reshape(b, t, H_A, A_NOPE + A_ROPE)
    qg = lp['a_q_gain']
    qa = jnp.concatenate([rms_norm(qa[..., :A_NOPE], qg[:A_NOPE]),
                          rope(rms_norm(qa[..., A_NOPE:], qg[A_NOPE:]), pos)], axis=-1)
    ckv = rms_norm(ckv, lp['a_kv_norm_g'])
    kpe = rope(rms_norm(kpe, lp['a_k_gain'][A_NOPE:]), pos)
    qb, kb, vb = jnp.split(zb, 3, axis=-1)
    qb = rms_norm(qb.reshape(b, t, H_B, 2, B_DH), lp['b_q_gain']).reshape(b, t, H_B, 2 * B_DH)
    kb = rms_norm(kb.reshape(b, t, H_B, 2, B_DH), lp['b_k_gain']).reshape(b, t, H_B, 2 * B_DH)
    vb = vb.reshape(b, t, H_B, 2 * B_DH)
    qc, kc, vc = jnp.split(zc, 3, axis=-1)
    qc = rms_norm(qc.reshape(b, t, H_C, C_DH), lp['c_q_gain'])
    kc = rms_norm(kc.reshape(b, t, H_C, C_DH), lp['c_k_gain'])
    vc = vc.reshape(b, t, H_C, C_DH)
    qd, kd, vd, fd = jnp.split(zd, [H_D * D_DH, 2 * H_D * D_DH, 3 * H_D * D_DH], axis=-1)
    qd = rms_norm(qd.reshape(b, t, H_D, D_DH), lp['d_q_gain'])
    kd = rms_norm(kd.reshape(b, t, H_D, D_DH), lp['d_k_gain'])
    vd = vd.reshape(b, t, H_D, D_DH)
    logf = jax.nn.log_sigmoid(fd.astype(jnp.float32) + lp['d_forget_b'].astype(jnp.float32))
    gates = jax.nn.sigmoid(zg.reshape(b, t, N_BRANCH, D_MODEL))
    return qa, ckv, kpe, qb, kb, vb, qc, kc, vc, qd, kd, vd, logf, gates


def mla_keys(ckv, kpe, lp):
    b, s = ckv.shape[:2]
    kv = (ckv @ lp['w_a_ukv']).reshape(b, s, H_A, A_NOPE + A_VDIM)
    k_nope = rms_norm(kv[..., :A_NOPE], lp['a_k_gain'][:A_NOPE])
    k = jnp.concatenate([k_nope, jnp.broadcast_to(kpe[:, :, None, :], (b, s, H_A, A_ROPE))], axis=-1)
    return k, kv[..., A_NOPE:]


def diff_lambda(lam_p, layer_idx):
    lam_init = 0.8 - 0.6 * math.exp(-0.3 * layer_idx)
    lf = lam_p.astype(jnp.float32)
    lam = jnp.exp(jnp.sum(lf[0] * lf[1])) - jnp.exp(jnp.sum(lf[2] * lf[3])) + lam_init
    return lam, lam_init


def merge_branches(o_a, o_b, o_c, o_d, gates, lp):
    b, t = gates.shape[:2]
    o = jnp.stack([o.reshape(b, t, BRANCH_W) for o in (o_a, o_b, o_c, o_d)], axis=2)
    br = jnp.einsum('btgi,gid->btgd', o, lp['w_branch'])
    return jnp.sum(gates * br, axis=2) @ lp['w_out']


def mixers_prompt(h, lp, layer_idx, slopes):
    b, s = h.shape[:2]
    pos = jnp.arange(s, dtype=jnp.int32)
    qa, ckv, kpe, qb, kb, vb, qc, kc, vc, qd, kd, vd, logf, gates = project(h, pos, lp)
    ka, va = mla_keys(ckv, kpe, lp)
    lam, lam_init = diff_lambda(lp['b_lambda'], layer_idx)
    fcum = jnp.cumsum(logf, axis=1)

    def block(start):
        qpos = start + jnp.arange(Q_BLOCK, dtype=jnp.int32)
        o_a = mla_core(_rows(qa, start), ka, va, qpos, pos)
        o_b = diff_core(_rows(qb, start), kb, vb, qpos, pos, lam, slopes)
        o_d = forget_core(_rows(qd, start), kd, vd, _rows(fcum, start), fcum, qpos, pos)
        return o_a, o_b, o_d

    o_a, o_b, o_d = sweep_query_blocks(block, s)
    o_b = rms_norm(o_b, lp['b_subln_g']) * (1.0 - lam_init)
    o_c = band_prompt(qc, kc, vc, lp['c_rel_bias'])
    mix = merge_branches(o_a, o_b, o_c, o_d, gates, lp)
    keep = min(BAND_ROWS, s)
    return mix, (ckv, kpe, kb, vb, kc[:, s - keep:], vc[:, s - keep:], kd, vd, logf)


def mixers_sample(h, caches, lp, layer_idx, slopes):
    lat_c, kpe_c, kb_c, vb_c, kc_c, vc_c, kd_c, vd_c, logf_c = caches
    t = h.shape[1]
    past = lat_c.shape[1]
    rows_c = kc_c.shape[1]
    pos = past + jnp.arange(t, dtype=jnp.int32)
    kpos = jnp.arange(past + t, dtype=jnp.int32)
    kpos_c = (past - rows_c) + jnp.arange(rows_c + t, dtype=jnp.int32)
    qa, ckv, kpe, qb, kb, vb, qc, kc, vc, qd, kd, vd, logf, gates = project(h, pos, lp)

    def cat(old, new):
        return jnp.concatenate([old, new], axis=1)

    ka, va = mla_keys(cat(lat_c, ckv), cat(kpe_c, kpe), lp)
    o_a = mla_core(qa, ka, va, pos, kpos)
    lam, lam_init = diff_lambda(lp['b_lambda'], layer_idx)
    o_b = diff_core(qb, cat(kb_c, kb), cat(vb_c, vb), pos, kpos, lam, slopes)
    o_b = rms_norm(o_b, lp['b_subln_g']) * (1.0 - lam_init)
    o_c = band_core(qc[:, None], cat(kc_c, kc)[:, None], cat(vc_c, vc)[:, None],
                    pos[None], kpos_c[None], lp['c_rel_bias'])[:, 0]
    fcum = jnp.cumsum(cat(logf_c.astype(jnp.float32), logf), axis=1)
    o_d = forget_core(qd, cat(kd_c, kd), cat(vd_c, vd), fcum[:, past:], fcum, pos, kpos)
    mix = merge_branches(o_a, o_b, o_c, o_d, gates, lp)
    return mix, (ckv, kpe, kb, vb, kc, vc, kd, vd, logf)


def ada_layer(x, c, lp, mixer_fn):
    mod = jax.nn.silu(c) @ lp['w_ada'] + lp['b_ada']
    sh1, sc1, g1, sh2, sc2, g2 = [m[:, None, :] for m in jnp.split(mod, 6, axis=-1)]
    h = rms_norm(x, lp['norm1_g']) * (1 + sc1) + sh1
    mix, state = mixer_fn(h)
    x = x + g1 * mix
    h2 = rms_norm(x, lp['norm2_g']) * (1 + sc2) + sh2
    u = jax.nn.relu(h2 @ lp['w_up'])
    x = x + g2 * ((u * u) @ lp['w_down'])
    return x, state


def setup_inputs(seed: int = 0) -> dict:
    key = jax.random.key(seed)
    keys = list(jax.random.split(key, 48))

    def nrm(shape, scale=1.0):
        return jax.random.normal(keys.pop(), shape, jnp.float32) * scale

    def gain(shape):
        return 1.0 + nrm(shape, 0.02)

    c_rows = min(BAND_ROWS, PAST_LEN)
    qk_a = A_NOPE + A_ROPE
    return {
        'x_prompt': nrm((BATCH, SEQ, D_MODEL)),
        'x_sample': nrm((DEC_BATCH, DEC_SEQ, D_MODEL)),
        'c_prompt': nrm((BATCH, D_MODEL)),
        'c_sample': nrm((DEC_BATCH, D_MODEL)),
        'cache_a_latent': nrm((DEPTH, DEC_BATCH, PAST_LEN, A_KV_RANK)),
        'cache_a_kpe': nrm((DEPTH, DEC_BATCH, PAST_LEN, A_ROPE)),
        'cache_b_k': nrm((DEPTH, DEC_BATCH, PAST_LEN, H_B, 2 * B_DH)),
        'cache_b_v': nrm((DEPTH, DEC_BATCH, PAST_LEN, H_B, 2 * B_DH)),
        'cache_c_k': nrm((DEPTH, DEC_BATCH, c_rows, H_C, C_DH)),
        'cache_c_v': nrm((DEPTH, DEC_BATCH, c_rows, H_C, C_DH)),
        'cache_d_k': nrm((DEPTH, DEC_BATCH, PAST_LEN, H_D, D_DH)),
        'cache_d_v': nrm((DEPTH, DEC_BATCH, PAST_LEN, H_D, D_DH)),
        'cache_d_logf': jax.nn.log_sigmoid(nrm((DEPTH, DEC_BATCH, PAST_LEN, H_D)) + 3.0),
        'norm1_g': gain((DEPTH, D_MODEL)),
        'norm2_g': gain((DEPTH, D_MODEL)),
        'w_ada': nrm((DEPTH, D_MODEL, 6 * D_MODEL), D_MODEL ** -0.5),
        'b_ada': nrm((DEPTH, 6 * D_MODEL), 0.02),
        'w_in': nrm((DEPTH, D_MODEL, IN_COLS), D_MODEL ** -0.5),
        'a_q_norm_g': gain((DEPTH, A_Q_RANK)),
        'a_kv_norm_g': gain((DEPTH, A_KV_RANK)),
        'w_a_uq': nrm((DEPTH, A_Q_RANK, H_A * qk_a), A_Q_RANK ** -0.5),
        'w_a_ukv': nrm((DEPTH, A_KV_RANK, H_A * (A_NOPE + A_VDIM)), A_KV_RANK ** -0.5),
        'a_q_gain': gain((DEPTH, qk_a)),
        'a_k_gain': gain((DEPTH, qk_a)),
        'b_q_gain': gain((DEPTH, B_DH)),
        'b_k_gain': gain((DEPTH, B_DH)),
        'b_lambda': nrm((DEPTH, 4, B_DH), 0.1),
        'b_subln_g': gain((DEPTH, 2 * B_DH)),
        'c_q_gain': gain((DEPTH, C_DH)),
        'c_k_gain': gain((DEPTH, C_DH)),
        'c_rel_bias': nrm((DEPTH, H_C, 2 * REL_CLIP + 1), 0.5),
        'd_q_gain': gain((DEPTH, D_DH)),
        'd_k_gain': gain((DEPTH, D_DH)),
        'd_forget_b': 3.0 + nrm((DEPTH, H_D), 0.5),
        'w_branch': nrm((DEPTH, N_BRANCH, BRANCH_W, D_MODEL), BRANCH_W ** -0.5),
        'w_out': nrm((DEPTH, D_MODEL, D_MODEL), D_MODEL ** -0.5),
        'w_up': nrm((DEPTH, D_MODEL, D_FF), D_MODEL ** -0.5),
        'w_down': nrm((DEPTH, D_FF, D_MODEL), D_FF ** -0.5),
    }


def reference(x_prompt, x_sample, c_prompt, c_sample,
              cache_a_latent, cache_a_kpe, cache_b_k, cache_b_v, cache_c_k, cache_c_v,
              cache_d_k, cache_d_v, cache_d_logf,
              norm1_g, norm2_g, w_ada, b_ada, w_in,
              a_q_norm_g, a_kv_norm_g, w_a_uq, w_a_ukv, a_q_gain, a_k_gain,
              b_q_gain, b_k_gain, b_lambda, b_subln_g,
              c_q_gain, c_k_gain, c_rel_bias,
              d_q_gain, d_k_gain, d_forget_b,
              w_branch, w_out, w_up, w_down):
    slopes = 2.0 ** (-8.0 * jnp.arange(1, H_B + 1, dtype=jnp.float32) / H_B)
    x_p, x_s = x_prompt, x_sample
    states_p, states_s = [], []
    for l in range(DEPTH):
        lp = {
            'norm1_g': norm1_g[l], 'norm2_g': norm2_g[l], 'w_ada': w_ada[l], 'b_ada': b_ada[l],
            'w_in': w_in[l], 'a_q_norm_g': a_q_norm_g[l], 'a_kv_norm_g': a_kv_norm_g[l],
            'w_a_uq': w_a_uq[l], 'w_a_ukv': w_a_ukv[l], 'a_q_gain': a_q_gain[l], 'a_k_gain': a_k_gain[l],
            'b_q_gain': b_q_gain[l], 'b_k_gain': b_k_gain[l], 'b_lambda': b_lambda[l], 'b_subln_g': b_subln_g[l],
            'c_q_gain': c_q_gain[l], 'c_k_gain': c_k_gain[l], 'c_rel_bias': c_rel_bias[l],
            'd_q_gain': d_q_gain[l], 'd_k_gain': d_k_gain[l], 'd_forget_b': d_forget_b[l],
            'w_branch': w_branch[l], 'w_out': w_out[l], 'w_up': w_up[l], 'w_down': w_down[l],
        }
        caches_l = (cache_a_latent[l], cache_a_kpe[l], cache_b_k[l], cache_b_v[l], cache_c_k[l],
                    cache_c_v[l], cache_d_k[l], cache_d_v[l], cache_d_logf[l])
        x_p, st_p = ada_layer(x_p, c_prompt, lp, lambda h: mixers_prompt(h, lp, l, slopes))
        x_s, st_s = ada_layer(x_s, c_sample, lp, lambda h: mixers_sample(h, caches_l, lp, l, slopes))
        states_p.append(st_p)
        states_s.append(st_s)
    a_lat_p, a_kpe_p, b_k_p, b_v_p, c_k_p, c_v_p, d_k_p, d_v_p, d_logf_p = [jnp.stack(z) for z in zip(*states_p)]
    a_lat_s, a_kpe_s, b_k_s, b_v_s, c_k_s, c_v_s, d_k_s, d_v_s, d_logf_s = [jnp.stack(z) for z in zip(*states_s)]
    return (x_p, x_s,
            a_lat_p, a_lat_s, a_kpe_p, a_kpe_s,
            b_k_p, b_k_s, b_v_p, b_v_s,
            c_k_p, c_k_s, c_v_p, c_v_s,
            d_k_p, d_k_s, d_v_p, d_v_s,
            d_logf_p, d_logf_s)
```

```python
import functools
import math

import jax
import jax.numpy as jnp
from jax import lax
from jax.experimental import pallas as pl
from jax.experimental.pallas import tpu as pltpu

BF = jnp.bfloat16
F32 = jnp.float32

CHUNK = 64
EPS = 1e-6
NEG_INF = -1e30
H = 4
A_NOPE, A_ROPE, A_VDIM = 128, 64, 128
A_Q_RANK, A_KV_RANK = 512, 256
A_QK_PAD = 256
ROPE_THETA = 10000.0
B_DH = 64
C_DH = 128
BAND_CHUNKS = 8
BAND_ROWS = BAND_CHUNKS * CHUNK
REL_CLIP = 128
D_DH = 128
HW = 512
N_BRANCH = 4
LANES = 128
PROJ_TILE = 512
PROJ_TILES = 11

VMEM_LIMIT_BYTES = 56 * 1024 * 1024


def _cparams(*sem):
    return pltpu.CompilerParams(dimension_semantics=sem, vmem_limit_bytes=VMEM_LIMIT_BYTES)


def _nt_dot(a, b):
    return lax.dot_general(a, b, (((1,), (1,)), ((), ())), preferred_element_type=F32)


def _dot(a, b):
    return jnp.dot(a, b, preferred_element_type=F32)


def _rms(z, n):
    ms = jnp.sum(z * z, axis=-1, keepdims=True) * (1.0 / n)
    return z * lax.rsqrt(ms + EPS)


def _rms_groups(z, gs):
    w = z.shape[-1]
    if gs >= LANES:
        parts = [_rms(z[:, g * gs:(g + 1) * gs], gs) for g in range(w // gs)]
        return parts[0] if len(parts) == 1 else jnp.concatenate(parts, axis=-1)
    assert gs * 2 == LANES
    lo = lax.broadcasted_iota(jnp.int32, (1, LANES), 1) < gs
    parts = []
    for g in range(w // LANES):
        zz = z[:, g * LANES:(g + 1) * LANES]
        sq = zz * zz
        s_lo = jnp.sum(jnp.where(lo, sq, 0.0), axis=-1, keepdims=True)
        s_hi = jnp.sum(jnp.where(lo, 0.0, sq), axis=-1, keepdims=True)
        ms = jnp.where(lo, s_lo, s_hi) * (1.0 / gs)
        parts.append(zz * lax.rsqrt(ms + EPS))
    return jnp.concatenate(parts, axis=-1)


def _rope128(r, cos, sin):
    half = A_ROPE // 2
    lane = lax.broadcasted_iota(jnp.int32, (1, LANES), 1)
    swapped = jnp.where(lane < half, pltpu.roll(r, LANES - half, 1), pltpu.roll(r, half, 1))
    return r * cos + swapped * sin


CHUNK_SHIFT = CHUNK.bit_length() - 1
assert 1 << CHUNK_SHIFT == CHUNK


def _chunk(pos):
    return jnp.right_shift(pos, CHUNK_SHIFT)


def _log_sigmoid(x):
    return jnp.minimum(x, 0.0) - jnp.log1p(jnp.exp(-jnp.abs(x)))


def _sigmoid(x):
    return 1.0 / (1.0 + jnp.exp(-x))


def _ada_kernel(c_ref, w_ref, b_ref, o_ref):
    c = c_ref[...]
    a = (c * _sigmoid(c)).astype(BF)
    o_ref[...] = _dot(a, w_ref[...].astype(BF)) + b_ref[...]


def _ada(c_all, w_ada, b_ada):
    depth, d, n = w_ada.shape
    r = c_all.shape[0]
    tn = 1024
    return pl.pallas_call(
        _ada_kernel,
        out_shape=jax.ShapeDtypeStruct((depth, r, n), F32),
        grid=(depth, n // tn),
        in_specs=[
            pl.BlockSpec((r, d), lambda l, j: (0, 0)),
            pl.BlockSpec((None, d, tn), lambda l, j: (l, 0, j)),
            pl.BlockSpec((None, 1, tn), lambda l, j: (l, 0, j)),
        ],
        out_specs=pl.BlockSpec((None, r, tn), lambda l, j: (l, 0, j)),
        compiler_params=_cparams("arbitrary", "arbitrary"),
        name="ada_mod",
    )(c_all, w_ada, b_ada.reshape(depth, 1, n))


def _proj_kernel(x_ref, sc_ref, sh_ref, ng_ref, w_ref, gq_ref, gkv_ref, gkpe_ref, gb_ref, gc_ref,
                 gd_ref, fb_ref, cos_ref, sin_ref,
                 h_out, cq_out, ckv_out, kpe_out, kpep_out, logf_out,
                 qb_out, kb_out, vb_out, qc_out, kc_out, vc_out, qd_out, kd_out, vd_out,
                 h_scr):
    j = pl.program_id(1)

    @pl.when(j == 0)
    def _():
        x = x_ref[...]
        y = _rms(x, x.shape[-1]) * ng_ref[...]
        hb = (y * (1.0 + sc_ref[...]) + sh_ref[...]).astype(BF)
        h_scr[...] = hb
        h_out[...] = hb

    z = _dot(h_scr[...], w_ref[...])

    @pl.when(j == 0)
    def _():
        cq_out[...] = (_rms(z, A_Q_RANK) * gq_ref[...]).astype(BF)

    @pl.when(j == 1)
    def _():
        ckv_out[...] = _rms(z[:, :A_KV_RANK], A_KV_RANK) * gkv_ref[...]
        kp = _rms(z[:, A_KV_RANK:A_KV_RANK + LANES], A_ROPE) * gkpe_ref[...]
        kp = _rope128(kp, cos_ref[...], sin_ref[...])
        kpe_out[...] = kp[:, :A_ROPE]
        kpep_out[...] = kp.astype(BF)
        f = z[:, A_KV_RANK + LANES:] + fb_ref[...]
        logf_out[...] = _log_sigmoid(f)[:, :H]

    def normed(gs, g_ref):
        return _rms_groups(z, gs) * g_ref[...]

    @pl.when(j == 2)
    def _():
        qb_out[...] = normed(B_DH, gb_ref.at[0:1]).astype(BF)

    @pl.when(j == 3)
    def _():
        kb_out[...] = normed(B_DH, gb_ref.at[1:2])

    @pl.when(j == 4)
    def _():
        vb_out[...] = z

    @pl.when(j == 5)
    def _():
        qc_out[...] = normed(C_DH, gc_ref.at[0:1]).astype(BF)

    @pl.when(j == 6)
    def _():
        kc_out[...] = normed(C_DH, gc_ref.at[1:2])

    @pl.when(j == 7)
    def _():
        vc_out[...] = z

    @pl.when(j == 8)
    def _():
        qd_out[...] = normed(D_DH, gd_ref.at[0:1]).astype(BF)

    @pl.when(j == 9)
    def _():
        kd_out[...] = normed(D_DH, gd_ref.at[1:2])

    @pl.when(j == 10)
    def _():
        vd_out[...] = z


def _proj(x2d, sc, sh, ng, wp, gq, gkv, gkpe, gb, gc, gd, fb, cos_t, sin_t, tm):
    m, d = x2d.shape
    groups, rows = sc.shape[0], sc.shape[1]
    tiles_per_group = (m // groups) // tm
    tab_blocks = cos_t.shape[0] // tm
    row = lambda i, j: (i, 0)
    const2 = lambda i, j: (0, 0)
    mod_spec = pl.BlockSpec((None, rows, d), lambda i, j: (i // tiles_per_group, 0, 0))
    tab_spec = pl.BlockSpec((tm, LANES), lambda i, j: (i % tab_blocks, 0))

    def out(width, dtype):
        return jax.ShapeDtypeStruct((m, width), dtype), pl.BlockSpec((tm, width), row)

    outs = [out(d, BF), out(A_Q_RANK, BF), out(A_KV_RANK, F32), out(A_ROPE, F32), out(LANES, BF),
            out(H, F32),
            out(HW, BF), out(HW, F32), out(HW, F32),
            out(HW, BF), out(HW, F32), out(HW, F32),
            out(HW, BF), out(HW, F32), out(HW, F32)]
    return pl.pallas_call(
        _proj_kernel,
        out_shape=[o[0] for o in outs],
        grid=(m // tm, PROJ_TILES),
        in_specs=[
            pl.BlockSpec((tm, d), row),
            mod_spec, mod_spec,
            pl.BlockSpec((1, d), const2),
            pl.BlockSpec((d, PROJ_TILE), lambda i, j: (0, j)),
            pl.BlockSpec((1, A_Q_RANK), const2),
            pl.BlockSpec((1, A_KV_RANK), const2),
            pl.BlockSpec((1, LANES), const2),
            pl.BlockSpec((2, HW), const2),
            pl.BlockSpec((2, HW), const2),
            pl.BlockSpec((2, HW), const2),
            pl.BlockSpec((1, PROJ_TILE - A_KV_RANK - LANES), const2),
            tab_spec, tab_spec,
        ],
        out_specs=[o[1] for o in outs],
        scratch_shapes=[pltpu.VMEM((tm, d), BF)],
        compiler_params=_cparams("arbitrary", "arbitrary"),
        name="in_proj",
    )(x2d, sc, sh, ng, wp, gq, gkv, gkpe, gb, gc, gd, fb, cos_t, sin_t)


def _aprep_kernel(cq_ref, ckv_ref, kpep_ref, wuq_ref, wukv_ref, gqn_ref, gqr_ref, gkn_ref,
                  cos_ref, sin_ref, q_out, k_out, v_out):
    zq = _dot(cq_ref[...], wuq_ref[...])
    cos, sin = cos_ref[...], sin_ref[...]
    parts = []
    for h in range(H):
        nope = zq[:, h * A_QK_PAD:h * A_QK_PAD + A_NOPE]
        rp = zq[:, h * A_QK_PAD + A_NOPE:(h + 1) * A_QK_PAD]
        parts.append(_rms(nope, A_NOPE) * gqn_ref[...])
        parts.append(_rope128(_rms(rp, A_ROPE) * gqr_ref[...], cos, sin))
    q_out[...] = jnp.concatenate(parts, axis=-1).astype(BF)

    zkv = _dot(ckv_ref[...].astype(BF), wukv_ref[...])
    kp = kpep_ref[...]
    parts = []
    for h in range(H):
        kn = _rms(zkv[:, h * A_NOPE:(h + 1) * A_NOPE], A_NOPE) * gkn_ref[...]
        parts.append(kn.astype(BF))
        parts.append(kp)
    k_out[...] = jnp.concatenate(parts, axis=-1)
    v_out[...] = zkv[:, H * A_NOPE:].astype(BF)


def _aprep(cqn, ckvn, kpep, wuq, wukv, gqn, gqr, gkn, cos_t, sin_t, tm):
    m = cqn.shape[0]
    tab_blocks = cos_t.shape[0] // tm
    row = lambda i: (i, 0)
    const = lambda i: (0, 0)
    tab_spec = pl.BlockSpec((tm, LANES), lambda i: (i % tab_blocks, 0))
    return pl.pallas_call(
        _aprep_kernel,
        out_shape=[jax.ShapeDtypeStruct((m, H * A_QK_PAD), BF),
                   jax.ShapeDtypeStruct((m, H * A_QK_PAD), BF),
                   jax.ShapeDtypeStruct((m, HW), BF)],
        grid=(m // tm,),
        in_specs=[
            pl.BlockSpec((tm, A_Q_RANK), row),
            pl.BlockSpec((tm, A_KV_RANK), row),
            pl.BlockSpec((tm, LANES), row),
            pl.BlockSpec(wuq.shape, const),
            pl.BlockSpec(wukv.shape, const),
            pl.BlockSpec((1, LANES), const),
            pl.BlockSpec((1, LANES), const),
            pl.BlockSpec((1, LANES), const),
            tab_spec, tab_spec,
        ],
        out_specs=[pl.BlockSpec((tm, H * A_QK_PAD), row),
                   pl.BlockSpec((tm, H * A_QK_PAD), row),
                   pl.BlockSpec((tm, HW), row)],
        compiler_params=_cparams("arbitrary"),
        name="a_prep",
    )(cqn, ckvn, kpep, wuq, wukv, gqn, gqr, gkn, cos_t, sin_t)


CUM_BLOCK = 256


def _tri_upper(n):
    r = lax.broadcasted_iota(jnp.int32, (n, n), 0)
    c = lax.broadcasted_iota(jnp.int32, (n, n), 1)
    return jnp.where(r <= c, 1.0, 0.0).astype(BF)


def _dot3(x, u):
    hi = x.astype(BF)
    r1 = x - hi.astype(F32)
    mid = r1.astype(BF)
    lo = (r1 - mid.astype(F32)).astype(BF)
    return _dot(hi, u) + _dot(mid, u) + _dot(lo, u)


def _cumsum_lanes(src_ref, dst_ref):
    u = _tri_upper(CUM_BLOCK)
    carry = jnp.zeros((8, 1), F32)
    for b in range(src_ref.shape[-1] // CUM_BLOCK):
        blk = slice(b * CUM_BLOCK, (b + 1) * CUM_BLOCK)
        c = _dot3(src_ref[:, blk], u) + carry
        dst_ref[:, blk] = c
        carry = c[:, CUM_BLOCK - 1:CUM_BLOCK]
    return carry


def _cumsum_kernel(x_ref, o_ref):
    _cumsum_lanes(x_ref, o_ref)


def _cumsum_rows(x):
    b, r, s = x.shape
    return pl.pallas_call(
        _cumsum_kernel,
        out_shape=jax.ShapeDtypeStruct(x.shape, F32),
        grid=(b,),
        in_specs=[pl.BlockSpec((None, r, s), lambda i: (i, 0, 0))],
        out_specs=pl.BlockSpec((None, r, s), lambda i: (i, 0, 0)),
        compiler_params=_cparams("arbitrary"),
        name="forget_cumsum",
    )(x)


def _relbias_kernel(tab_ref, o_ref, *, bases, rows, cols, dmin):
    h = pl.program_id(0)
    r = lax.broadcasted_iota(jnp.int32, (rows, cols), 0)
    c = lax.broadcasted_iota(jnp.int32, (rows, cols), 1)
    for t, base in enumerate(bases):
        idx = jnp.clip(base + r - c, -REL_CLIP, REL_CLIP) + REL_CLIP
        lo = min(max(base - (cols - 1), max(dmin, -REL_CLIP)), REL_CLIP) + REL_CLIP
        hi = min(max(base + rows - 1, -REL_CLIP), REL_CLIP) + REL_CLIP

        def body(e, acc):
            return jnp.where(idx == e, tab_ref[h, e], acc)

        o_ref[t] = lax.fori_loop(lo, hi + 1, body, jnp.zeros((rows, cols), F32))


def _relbias(table, bases, rows, cols, dmin):
    nh = table.shape[0]
    return pl.pallas_call(
        functools.partial(_relbias_kernel, bases=tuple(bases), rows=rows, cols=cols, dmin=dmin),
        out_shape=jax.ShapeDtypeStruct((nh, len(bases), rows, cols), F32),
        grid=(nh,),
        in_specs=[pl.BlockSpec(memory_space=pltpu.SMEM)],
        out_specs=pl.BlockSpec((None, len(bases), rows, cols), lambda h: (h, 0, 0, 0)),
        compiler_params=_cparams("arbitrary"),
        name="rel_bias_tiles",
    )(table)


def _online_update(s, v, m_ref, l_ref, acc_ref, idx):
    m_prev = m_ref[idx]
    m_new = jnp.maximum(m_prev, jnp.max(s, axis=-1, keepdims=True))
    alpha = jnp.exp(m_prev - m_new)
    p = jnp.exp(s - m_new)
    l_ref[idx] = alpha * l_ref[idx] + jnp.sum(p, axis=-1, keepdims=True)
    acc_ref[idx] = alpha * acc_ref[idx] + _dot(p.astype(BF), v)
    m_ref[idx] = m_new


def _diff_lambda(lam_ref, layer_idx):
    lam_init = 0.8 - 0.6 * math.exp(-0.3 * layer_idx)
    lp = lam_ref[...]
    a = jnp.sum(lp[0:1] * lp[1:2], keepdims=True)
    b = jnp.sum(lp[2:3] * lp[3:4], keepdims=True)
    return jnp.exp(a) - jnp.exp(b) + lam_init, lam_init


ALIBI_SLOPES = tuple(2.0 ** (-8.0 * (h + 1) / H) for h in range(H))


def _pattn_kernel(*refs, mode, tq, tk, nk, layer_idx, back):
    if mode == "a":
        q_ref, k_ref, v_ref, o_ref, m_ref, l_ref, acc_ref = refs
        dq, scale = A_QK_PAD, (A_NOPE + A_ROPE) ** -0.5
    elif mode == "b":
        q_ref, k_ref, v_ref, lam_ref, sg_ref, o_ref, m_ref, l_ref, acc_ref = refs
        dq, scale = 2 * B_DH, B_DH ** -0.5
    elif mode == "c":
        q_ref, k_ref, v_ref, bias_ref, o_ref, m_ref, l_ref, acc_ref = refs
        dq, scale = C_DH, C_DH ** -0.5
    else:
        q_ref, k_ref, v_ref, fq_ref, fk_ref, o_ref, m_ref, l_ref, acc_ref = refs
        dq, scale = D_DH, D_DH ** -0.5
    dv = HW // H
    i = pl.program_id(1)
    j = pl.program_id(2)
    if mode == "c":
        kb = i - back + j
        valid = kb >= 0
        first = jnp.maximum(back - i, 0)
        last = nk - 1
    else:
        last = ((i + 1) * tq - 1) // tk
        kb = j
        valid = j <= last
        first = 0

    @pl.when(j == first)
    def _():
        m_ref[...] = jnp.full(m_ref.shape, -jnp.inf, F32)
        l_ref[...] = jnp.zeros(l_ref.shape, F32)
        acc_ref[...] = jnp.zeros(acc_ref.shape, F32)

    @pl.when(valid)
    def _():
        qpos = i * tq + lax.broadcasted_iota(jnp.int32, (tq, tk), 0)
        kpos = kb * tk + lax.broadcasted_iota(jnp.int32, (tq, tk), 1)
        if mode in ("a", "b"):
            mask = _chunk(kpos) <= _chunk(qpos)
        elif mode == "c":
            qc = _chunk(qpos)
            kc = _chunk(kpos)
            mask = (kc <= qc) & (kc >= qc - BAND_CHUNKS)
        else:
            mask = kpos <= qpos
        if mode == "b":
            dist = jnp.abs(qpos - kpos).astype(F32)
            lo = lax.broadcasted_iota(jnp.int32, (1, dq), 1) < B_DH
        for h in range(H):
            q = q_ref[:, h * dq:(h + 1) * dq]
            k = k_ref[:, h * dq:(h + 1) * dq].astype(BF)
            v = v_ref[:, h * dv:(h + 1) * dv].astype(BF)
            if mode == "b":
                bias = -ALIBI_SLOPES[h] * dist
                zero = jnp.zeros_like(q)
                s1 = _nt_dot(jnp.where(lo, q, zero), k) * scale + bias
                s2 = _nt_dot(jnp.where(lo, zero, q), k) * scale + bias
                _online_update(jnp.where(mask, s1, NEG_INF), v, m_ref, l_ref, acc_ref, h)
                _online_update(jnp.where(mask, s2, NEG_INF), v, m_ref, l_ref, acc_ref, H + h)
                continue
            s = _nt_dot(q, k) * scale
            if mode == "c":
                s = s + bias_ref[h]
            elif mode == "d":
                s = s + (fq_ref[:, h:h + 1] - fk_ref[h:h + 1, :])
            _online_update(jnp.where(mask, s, NEG_INF), v, m_ref, l_ref, acc_ref, h)

    @pl.when(j == last)
    def _():
        for h in range(H):
            o = acc_ref[h] / l_ref[h]
            if mode == "b":
                lam, lam_init = _diff_lambda(lam_ref, layer_idx)
                o = o - lam * (acc_ref[H + h] / l_ref[H + h])
                o = (_rms(o, dv) * sg_ref[...]) * (1.0 - lam_init)
            o_ref[:, h * dv:(h + 1) * dv] = o.astype(BF)


def _pattn(mode, q, k, v, extras, layer_idx=0):
    b, s, qw = q.shape
    kw = k.shape[-1]
    back = 0
    if mode == "c":
        tq = tk = 256
        back = BAND_ROWS // tk
        nk = back + 1
        kmap = lambda bi, i, j: (bi, jnp.maximum(i - back + j, 0), 0)
    else:
        tq = tk = 512
        nk = s // tk
        kmap = lambda bi, i, j: (bi, jnp.minimum(j, ((i + 1) * tq - 1) // tk), 0)
    qmap = lambda bi, i, j: (bi, i, 0)
    in_specs = [pl.BlockSpec((None, tq, qw), qmap),
                pl.BlockSpec((None, tk, kw), kmap),
                pl.BlockSpec((None, tk, HW), kmap)]
    if mode == "b":
        lam, sg = extras
        in_specs += [pl.BlockSpec(lam.shape, lambda bi, i, j: (0, 0)),
                     pl.BlockSpec(sg.shape, lambda bi, i, j: (0, 0))]
    elif mode == "c":
        (bias,) = extras
        in_specs += [pl.BlockSpec((H, None, tq, tk), lambda bi, i, j: (0, j, 0, 0))]
    elif mode == "d":
        fcol, frow = extras
        in_specs += [pl.BlockSpec((None, tq, H), qmap),
                     pl.BlockSpec((None, 8, tk), lambda bi, i, j: (bi, 0, kmap(bi, i, j)[1]))]
    nstate = 2 * H if mode == "b" else H
    return pl.pallas_call(
        functools.partial(_pattn_kernel, mode=mode, tq=tq, tk=tk, nk=nk, layer_idx=layer_idx, back=back),
        out_shape=jax.ShapeDtypeStruct((b, s, HW), BF),
        grid=(b, s // tq, nk),
        in_specs=in_specs,
        out_specs=pl.BlockSpec((None, tq, HW), qmap),
        scratch_shapes=[pltpu.VMEM((nstate, tq, 1), F32),
                        pltpu.VMEM((nstate, tq, 1), F32),
                        pltpu.VMEM((nstate, tq, HW // H), F32)],
        compiler_params=_cparams("arbitrary", "arbitrary", "arbitrary"),
        name="prompt_attn_" + mode,
    )(q, k, v, *extras)


def _pad_rows(x, rows):
    return jnp.concatenate([x, jnp.zeros((rows - x.shape[0], x.shape[1]), x.dtype)], axis=0)


def _sattn_kernel(*refs, mode, t, past, layer_idx):
    if mode == "a":
        (q_ref, lat_ref, kpe_ref, kn_ref, vn_ref, wukv_ref, gkn_ref, o_ref) = refs
        dq, scale = A_QK_PAD, (A_NOPE + A_ROPE) ** -0.5
    elif mode == "b":
        (q_ref, kc_ref, vc_ref, kn_ref, vn_ref, lam_ref, sg_ref, o_ref) = refs
        dq, scale = 2 * B_DH, B_DH ** -0.5
    elif mode == "c":
        (q_ref, kc_ref, vc_ref, kn_ref, vn_ref, bias_ref, o_ref) = refs
        dq, scale = C_DH, C_DH ** -0.5
    else:
        (q_ref, kc_ref, vc_ref, kn_ref, vn_ref, lfc_ref, lfn_ref, o_ref, f_scr) = refs
        dq, scale = D_DH, D_DH ** -0.5
    dv = HW // H
    rows_c = lat_ref.shape[0] if mode == "a" else kc_ref.shape[0]
    npad = LANES
    qpos_c = past + lax.broadcasted_iota(jnp.int32, (t, rows_c), 0)
    kpos_c = (past - rows_c) + lax.broadcasted_iota(jnp.int32, (t, rows_c), 1)
    qpos_n = past + lax.broadcasted_iota(jnp.int32, (t, npad), 0)
    col_n = lax.broadcasted_iota(jnp.int32, (t, npad), 1)
    kpos_n = past + col_n
    real_n = col_n < t
    if mode in ("a", "b"):
        mask_c = _chunk(kpos_c) <= _chunk(qpos_c)
        mask_n = real_n & (_chunk(kpos_n) <= _chunk(qpos_n))
    elif mode == "c":
        qc_c, kc_c = _chunk(qpos_c), _chunk(kpos_c)
        qc_n, kc_n = _chunk(qpos_n), _chunk(kpos_n)
        mask_c = (kpos_c >= 0) & (kc_c <= qc_c) & (kc_c >= qc_c - BAND_CHUNKS)
        mask_n = real_n & (kc_n <= qc_n) & (kc_n >= qc_n - BAND_CHUNKS)
    else:
        mask_c = kpos_c <= qpos_c
        mask_n = real_n & (kpos_n <= qpos_n)

    if mode == "a":
        lat = lat_ref[...].astype(BF)
        kpe_c = kpe_ref[...]
        kpe_c = jnp.concatenate([kpe_c, jnp.zeros_like(kpe_c)], axis=-1).astype(BF)
    if mode == "b":
        dist_c = jnp.abs(qpos_c - kpos_c).astype(F32)
        dist_n = jnp.abs(qpos_n - kpos_n).astype(F32)
        lo = lax.broadcasted_iota(jnp.int32, (1, dq), 1) < B_DH
        lam, lam_init = _diff_lambda(lam_ref, layer_idx)
    if mode == "d":
        carry = _cumsum_lanes(lfc_ref, f_scr)
        f_new = _dot3(lfn_ref[...], _tri_upper(LANES)) + carry
        eye = (lax.broadcasted_iota(jnp.int32, (t, npad), 0) == col_n)

    def softmax_pv(s_c, s_n, v_c, v_n):
        m = jnp.maximum(jnp.max(s_c, axis=-1, keepdims=True), jnp.max(s_n, axis=-1, keepdims=True))
        p_c = jnp.exp(s_c - m)
        p_n = jnp.exp(s_n - m)
        l = jnp.sum(p_c, axis=-1, keepdims=True) + jnp.sum(p_n, axis=-1, keepdims=True)
        return (_dot(p_c.astype(BF), v_c) + _dot(p_n.astype(BF), v_n)) / l

    for h in range(H):
        q = q_ref[:, h * dq:(h + 1) * dq]
        if mode == "a":
            kn = _pad_rows(kn_ref[:, h * dq:(h + 1) * dq], npad)
            vn = _pad_rows(vn_ref[:, h * dv:(h + 1) * dv], npad)
            w = wukv_ref[...]
            k_nope = _dot(lat, w[:, h * A_NOPE:(h + 1) * A_NOPE])
            k_nope = (_rms(k_nope, A_NOPE) * gkn_ref[...]).astype(BF)
            v_c = _dot(lat, w[:, H * A_NOPE + h * A_VDIM:H * A_NOPE + (h + 1) * A_VDIM]).astype(BF)
            s_c = (_nt_dot(q[:, :A_NOPE], k_nope) + _nt_dot(q[:, A_NOPE:], kpe_c)) * scale
            s_n = _nt_dot(q, kn) * scale
        else:
            k_c = kc_ref[:, h * dq:(h + 1) * dq].astype(BF)
            v_c = vc_ref[:, h * dv:(h + 1) * dv].astype(BF)
            kn = _pad_rows(kn_ref[:, h * dq:(h + 1) * dq], npad).astype(BF)
            vn = _pad_rows(vn_ref[:, h * dv:(h + 1) * dv], npad).astype(BF)
        if mode == "b":
            zero = jnp.zeros_like(q)
            q1, q2 = jnp.where(lo, q, zero), jnp.where(lo, zero, q)
            bias_c = -ALIBI_SLOPES[h] * dist_c
            bias_n = -ALIBI_SLOPES[h] * dist_n
            outs = []
            for qq in (q1, q2):
                s_c = jnp.where(mask_c, _nt_dot(qq, k_c) * scale + bias_c, NEG_INF)
                s_n = jnp.where(mask_n, _nt_dot(qq, kn) * scale + bias_n, NEG_INF)
                outs.append(softmax_pv(s_c, s_n, v_c, vn))
            o = outs[0] - lam * outs[1]
            o = (_rms(o, dv) * sg_ref[...]) * (1.0 - lam_init)
        else:
            if mode != "a":
                s_c = _nt_dot(q, k_c) * scale
                s_n = _nt_dot(q, kn) * scale
            if mode == "c":
                s_c = s_c + bias_ref[h, :, :rows_c]
                s_n = s_n + bias_ref[h, :, rows_c:]
            elif mode == "d":
                fq = jnp.sum(jnp.where(eye, f_new[h:h + 1, :], 0.0), axis=-1, keepdims=True)
                s_c = s_c + (fq - f_scr[h:h + 1, :])
                s_n = s_n + (fq - f_new[h:h + 1, :])
            s_c = jnp.where(mask_c, s_c, NEG_INF)
            s_n = jnp.where(mask_n, s_n, NEG_INF)
            o = softmax_pv(s_c, s_n, v_c, vn)
        o_ref[:, h * dv:(h + 1) * dv] = o.astype(BF)


def _sattn(mode, q, cache_k, cache_v, kn, vn, extras, t, past, layer_idx=0):
    nb = cache_k.shape[0]
    qw = q.shape[-1]
    row = lambda b: (b, 0)
    in_specs = [pl.BlockSpec((t, qw), row),
                pl.BlockSpec((None,) + cache_k.shape[1:], lambda b: (b, 0, 0)),
                pl.BlockSpec((None,) + cache_v.shape[1:], lambda b: (b, 0, 0)),
                pl.BlockSpec((t, kn.shape[-1]), row),
                pl.BlockSpec((t, vn.shape[-1]), row)]
    scratch = []
    if mode == "d":
        lfc, lfn = extras
        in_specs += [pl.BlockSpec((None,) + lfc.shape[1:], lambda b: (b, 0, 0)),
                     pl.BlockSpec((None,) + lfn.shape[1:], lambda b: (b, 0, 0))]
        scratch = [pltpu.VMEM(lfc.shape[1:], F32)]
    elif mode == "c":
        (bias,) = extras
        in_specs += [pl.BlockSpec(bias.shape, lambda b: (0, 0, 0))]
    else:
        in_specs += [pl.BlockSpec(e.shape, lambda b: (0, 0)) for e in extras]
    return pl.pallas_call(
        functools.partial(_sattn_kernel, mode=mode, t=t, past=past, layer_idx=layer_idx),
        out_shape=jax.ShapeDtypeStruct((nb * t, HW), BF),
        grid=(nb,),
        in_specs=in_specs,
        out_specs=pl.BlockSpec((t, HW), row),
        scratch_shapes=scratch,
        compiler_params=_cparams("arbitrary"),
        name="sample_attn_" + mode,
    )(q, cache_k, cache_v, kn, vn, *extras)


def _gmerge_kernel(h_ref, oa_ref, ob_ref, oc_ref, od_ref, wg0_ref, wg1_ref, wg2_ref, wg3_ref,
                   wb_ref, out_ref):
    hb = h_ref[...]
    acc = None
    for g, (o_ref, wg_ref) in enumerate(((oa_ref, wg0_ref), (ob_ref, wg1_ref),
                                         (oc_ref, wg2_ref), (od_ref, wg3_ref))):
        term = _sigmoid(_dot(hb, wg_ref[...])) * _dot(o_ref[...], wb_ref[g])
        acc = term if acc is None else acc + term
    out_ref[...] = acc.astype(BF)


def _gmerge(hb, outs, wg, wb, tm):
    m, d = hb.shape
    tn = 512
    nt = d // tn
    row = lambda i, n: (i, 0)
    wg_specs = [pl.BlockSpec((d, tn), functools.partial(lambda i, n, g: (0, g * nt + n), g=g))
                for g in range(N_BRANCH)]
    return pl.pallas_call(
        _gmerge_kernel,
        out_shape=jax.ShapeDtypeStruct((m, d), BF),
        grid=(m // tm, nt),
        in_specs=[pl.BlockSpec((tm, d), row)] + [pl.BlockSpec((tm, HW), row)] * N_BRANCH + wg_specs
                 + [pl.BlockSpec((N_BRANCH, HW, tn), lambda i, n: (0, 0, n))],
        out_specs=pl.BlockSpec((tm, tn), lambda i, n: (i, n)),
        compiler_params=_cparams("arbitrary", "arbitrary"),
        name="gate_merge",
    )(hb, *outs, wg, wg, wg, wg, wb)


def _oproj_kernel(mix_ref, x_ref, g_ref, w_ref, o_ref):
    o_ref[...] = x_ref[...] + g_ref[...] * _dot(mix_ref[...], w_ref[...])


def _oproj(mix, x2d, gate, w, tm):
    m, d = x2d.shape
    rows = gate.shape[1]
    tiles_per_group = (m // gate.shape[0]) // tm
    row = lambda i: (i, 0)
    return pl.pallas_call(
        _oproj_kernel,
        out_shape=jax.ShapeDtypeStruct((m, d), F32),
        grid=(m // tm,),
        in_specs=[pl.BlockSpec((tm, d), row), pl.BlockSpec((tm, d), row),
                  pl.BlockSpec((None, rows, d), lambda i: (i // tiles_per_group, 0, 0)),
                  pl.BlockSpec((d, d), lambda i: (0, 0))],
        out_specs=pl.BlockSpec((tm, d), row),
        compiler_params=_cparams("arbitrary"),
        name="out_proj",
    )(mix, x2d, gate, w)


def _ffn_kernel(x_ref, sc_ref, sh_ref, g_ref, ng_ref, wu_ref, wd_ref, o_ref, h_scr, *, nf):
    f = pl.program_id(1)

    @pl.when(f == 0)
    def _():
        x = x_ref[...]
        y = _rms(x, x.shape[-1]) * ng_ref[...]
        h_scr[...] = (y * (1.0 + sc_ref[...]) + sh_ref[...]).astype(BF)
        o_ref[...] = jnp.zeros(o_ref.shape, F32)

    u = jnp.maximum(_dot(h_scr[...], wu_ref[...]), 0.0)
    o_ref[...] += _dot((u * u).astype(BF), wd_ref[...])

    @pl.when(f == nf - 1)
    def _():
        o_ref[...] = x_ref[...] + g_ref[...] * o_ref[...]


def _ffn(x2d, sc, sh, gate, ng, wu, wd, tm):
    m, d = x2d.shape
    dff = wu.shape[1]
    tf = 1024
    rows = sc.shape[1]
    tiles_per_group = (m // sc.shape[0]) // tm
    row = lambda i, f: (i, 0)
    mod_spec = pl.BlockSpec((None, rows, d), lambda i, f: (i // tiles_per_group, 0, 0))
    return pl.pallas_call(
        functools.partial(_ffn_kernel, nf=dff // tf),
        out_shape=jax.ShapeDtypeStruct((m, d), F32),
        grid=(m // tm, dff // tf),
        in_specs=[pl.BlockSpec((tm, d), row), mod_spec, mod_spec, mod_spec,
                  pl.BlockSpec((1, d), lambda i, f: (0, 0)),
                  pl.BlockSpec((d, tf), lambda i, f: (0, f)),
                  pl.BlockSpec((tf, d), lambda i, f: (f, 0))],
        out_specs=pl.BlockSpec((tm, d), row),
        scratch_shapes=[pltpu.VMEM((tm, d), BF)],
        compiler_params=_cparams("arbitrary", "arbitrary"),
        name="ffn",
    )(x2d, sc, sh, gate, ng, wu, wd)


def _pack_layer(l, p):
    w_in = p["w_in"][l]
    d = w_in.shape[0]
    o = 0
    cols = {}
    for name, width in (("cq", A_Q_RANK), ("ckv", A_KV_RANK), ("kpe", A_ROPE),
                        ("qb", HW), ("kb", HW), ("vb", HW), ("qc", HW), ("kc", HW), ("vc", HW),
                        ("qd", HW), ("kd", HW), ("vd", HW), ("fd", H)):
        cols[name] = w_in[:, o:o + width]
        o += width
    zeros = lambda n: jnp.zeros((d, n), w_in.dtype)
    wp = jnp.concatenate(
        [cols["cq"], cols["ckv"], cols["kpe"], zeros(LANES - A_ROPE), cols["fd"], zeros(LANES - H)]
        + [cols[n] for n in ("qb", "kb", "vb", "qc", "kc", "vc", "qd", "kd", "vd")], axis=1).astype(BF)
    wg = w_in[:, o:].astype(BF)

    qk = A_NOPE + A_ROPE
    wuq = p["w_a_uq"][l].reshape(A_Q_RANK, H, qk)
    wuq = jnp.pad(wuq, ((0, 0), (0, 0), (0, A_QK_PAD - qk))).reshape(A_Q_RANK, H * A_QK_PAD).astype(BF)
    wukv = p["w_a_ukv"][l].reshape(A_KV_RANK, H, A_NOPE + A_VDIM)
    wukv = jnp.concatenate([wukv[:, :, :A_NOPE].reshape(A_KV_RANK, H * A_NOPE),
                            wukv[:, :, A_NOPE:].reshape(A_KV_RANK, H * A_VDIM)], axis=1).astype(BF)

    def row(v, width=None):
        v = v.reshape(1, -1).astype(F32)
        if width is not None and v.shape[1] < width:
            v = jnp.pad(v, ((0, 0), (0, width - v.shape[1])))
        return v

    def head_rows(gq, gk, reps):
        return jnp.stack([jnp.tile(gq, reps), jnp.tile(gk, reps)]).astype(F32)

    return dict(
        wp=wp, wg=wg, wuq=wuq, wukv=wukv,
        ng1=row(p["norm1_g"][l]), ng2=row(p["norm2_g"][l]),
        gq=row(p["a_q_norm_g"][l]), gkv=row(p["a_kv_norm_g"][l]),
        gkpe=row(p["a_k_gain"][l][A_NOPE:], LANES),
        gqn=row(p["a_q_gain"][l][:A_NOPE]), gqr=row(p["a_q_gain"][l][A_NOPE:], LANES),
        gkn=row(p["a_k_gain"][l][:A_NOPE]),
        gb=head_rows(p["b_q_gain"][l], p["b_k_gain"][l], HW // B_DH),
        gc=head_rows(p["c_q_gain"][l], p["c_k_gain"][l], H),
        gd=head_rows(p["d_q_gain"][l], p["d_k_gain"][l], H),
        fb=row(p["d_forget_b"][l], PROJ_TILE - A_KV_RANK - LANES),
        lam=p["b_lambda"][l].astype(F32), sg=row(p["b_subln_g"][l]),
        rel=p["c_rel_bias"][l].astype(F32),
        wb=p["w_branch"][l].astype(BF), wo=p["w_out"][l].astype(BF),
        wu=p["w_up"][l].astype(BF), wd=p["w_down"][l].astype(BF),
    )


def _rope_tables(pos):
    half = A_ROPE // 2
    inv = ROPE_THETA ** (-jnp.arange(half, dtype=F32) / half)
    ang = pos.astype(F32)[:, None] * inv[None, :]
    cos, sin = jnp.cos(ang), jnp.sin(ang)
    z = jnp.zeros((pos.shape[0], LANES - A_ROPE), F32)
    return jnp.concatenate([cos, cos, z], axis=1), jnp.concatenate([-sin, sin, z], axis=1)


def _heads_to_rows(x, lanes):
    b, t, h = x.shape
    return jnp.pad(jnp.swapaxes(x, 1, 2), ((0, 0), (0, 8 - h), (0, lanes - t)))


def _front(x2d, mods, pk, tabs, tm):
    sh1, sc1 = mods[0], mods[1]
    cos_t, sin_t = tabs
    (hb, cqn, ckvn, kpe, kpep, logf, qb, kb, vb, qc, kc, vc, qd, kd, vd) = _proj(
        x2d, sc1, sh1, pk["ng1"], pk["wp"], pk["gq"], pk["gkv"], pk["gkpe"], pk["gb"], pk["gc"],
        pk["gd"], pk["fb"], cos_t, sin_t, tm)
    qa, ka, va = _aprep(cqn, ckvn, kpep, pk["wuq"], pk["wukv"], pk["gqn"], pk["gqr"], pk["gkn"],
                        cos_t, sin_t, tm)
    return dict(h=hb, ckv=ckvn, kpe=kpe, logf=logf, qa=qa, ka=ka, va=va, qb=qb, kb=kb, vb=vb,
                qc=qc, kc=kc, vc=vc, qd=qd, kd=kd, vd=vd)


def _back(x2d, hb, outs, mods, pk, tm):
    g1, sh2, sc2, g2 = mods[2], mods[3], mods[4], mods[5]
    mix = _gmerge(hb, outs, pk["wg"], pk["wb"], tm)
    x2d = _oproj(mix, x2d, g1, pk["wo"], tm)
    return _ffn(x2d, sc2, sh2, g2, pk["ng2"], pk["wu"], pk["wd"], tm)


def _prompt_layer(x2d, mods, pk, tabs, bias_tiles, b, s, l):
    tm = 512
    f = _front(x2d, mods, pk, tabs, tm)
    r3 = lambda a: a.reshape(b, s, a.shape[-1])
    o_a = _pattn("a", r3(f["qa"]), r3(f["ka"]), r3(f["va"]), ())
    o_b = _pattn("b", r3(f["qb"]), r3(f["kb"]), r3(f["vb"]), (pk["lam"], pk["sg"]), l)
    o_c = _pattn("c", r3(f["qc"]), r3(f["kc"]), r3(f["vc"]), (bias_tiles,))
    frow = _cumsum_rows(_heads_to_rows(r3(f["logf"]), s))
    fcol = jnp.swapaxes(frow[:, :H, :], 1, 2)
    o_d = _pattn("d", r3(f["qd"]), r3(f["kd"]), r3(f["vd"]), (fcol, frow))
    outs = [o.reshape(b * s, HW) for o in (o_a, o_b, o_c, o_d)]
    x2d = _back(x2d, f["h"], outs, mods, pk, tm)
    keep = min(BAND_ROWS, s)
    state = (r3(f["ckv"]), r3(f["kpe"]),
             f["kb"].reshape(b, s, H, 2 * B_DH), f["vb"].reshape(b, s, H, 2 * B_DH),
             f["kc"].reshape(b, s, H, C_DH)[:, s - keep:], f["vc"].reshape(b, s, H, C_DH)[:, s - keep:],
             f["kd"].reshape(b, s, H, D_DH), f["vd"].reshape(b, s, H, D_DH), r3(f["logf"]))
    return x2d, state


def _sample_layer(x2d, mods, pk, tabs, bias_tiles, caches, nb, t, l):
    lat_c, kpe_c, kb_c, vb_c, kc_c, vc_c, kd_c, vd_c, logf_c = caches
    past = lat_c.shape[1]
    tm = nb * t
    f = _front(x2d, mods, pk, tabs, tm)
    flat = lambda a: a.reshape(a.shape[0], a.shape[1], HW)
    o_a = _sattn("a", f["qa"], lat_c, kpe_c, f["ka"], f["va"], (pk["wukv"], pk["gkn"]), t, past)
    o_b = _sattn("b", f["qb"], flat(kb_c), flat(vb_c), f["kb"], f["vb"], (pk["lam"], pk["sg"]), t, past, l)
    o_c = _sattn("c", f["qc"], flat(kc_c), flat(vc_c), f["kc"], f["vc"], (bias_tiles,), t, past)
    lfc = _heads_to_rows(logf_c.astype(F32), past)
    lfn = _heads_to_rows(f["logf"].reshape(nb, t, H), LANES)
    o_d = _sattn("d", f["qd"], flat(kd_c), flat(vd_c), f["kd"], f["vd"], (lfc, lfn), t, past)
    x2d = _back(x2d, f["h"], [o_a, o_b, o_c, o_d], mods, pk, tm)
    r3 = lambda a: a.reshape(nb, t, a.shape[-1])
    r4 = lambda a, dh: a.reshape(nb, t, H, dh)
    state = (r3(f["ckv"]), r3(f["kpe"]), r4(f["kb"], 2 * B_DH), r4(f["vb"], 2 * B_DH),
             r4(f["kc"], C_DH), r4(f["vc"], C_DH), r4(f["kd"], D_DH), r4(f["vd"], D_DH), r3(f["logf"]))
    return x2d, state


def kernel(x_prompt, x_sample, c_prompt, c_sample,
           cache_a_latent, cache_a_kpe, cache_b_k, cache_b_v, cache_c_k, cache_c_v,
           cache_d_k, cache_d_v, cache_d_logf,
           norm1_g, norm2_g, w_ada, b_ada, w_in,
           a_q_norm_g, a_kv_norm_g, w_a_uq, w_a_ukv, a_q_gain, a_k_gain,
           b_q_gain, b_k_gain, b_lambda, b_subln_g,
           c_q_gain, c_k_gain, c_rel_bias,
           d_q_gain, d_k_gain, d_forget_b,
           w_branch, w_out, w_up, w_down):
    params = dict(norm1_g=norm1_g, norm2_g=norm2_g, w_in=w_in, a_q_norm_g=a_q_norm_g,
                  a_kv_norm_g=a_kv_norm_g, w_a_uq=w_a_uq, w_a_ukv=w_a_ukv, a_q_gain=a_q_gain,
                  a_k_gain=a_k_gain, b_q_gain=b_q_gain, b_k_gain=b_k_gain, b_lambda=b_lambda,
                  b_subln_g=b_subln_g, c_q_gain=c_q_gain, c_k_gain=c_k_gain, c_rel_bias=c_rel_bias,
                  d_q_gain=d_q_gain, d_k_gain=d_k_gain, d_forget_b=d_forget_b,
                  w_branch=w_branch, w_out=w_out, w_up=w_up, w_down=w_down)
    depth = w_in.shape[0]
    b, s, d = x_prompt.shape
    nb, t, _ = x_sample.shape
    past = cache_a_latent.shape[2]
    rows_c = cache_c_k.shape[2]

    n_c = b + nb
    c_all = jnp.pad(jnp.concatenate([c_prompt, c_sample], axis=0), ((0, (-n_c) % 8), (0, 0)))
    mod_all = _ada(c_all, w_ada, b_ada)

    tabs_p = _rope_tables(jnp.arange(s, dtype=jnp.int32))
    tabs_s = _rope_tables(jnp.tile(past + jnp.arange(t, dtype=jnp.int32), nb))

    tile = 256
    p_bases = [(BAND_ROWS // tile - k) * tile for k in range(BAND_ROWS // tile + 1)]

    x_p = x_prompt.reshape(b * s, d)
    x_s = x_sample.reshape(nb * t, d)
    states_p, states_s = [], []
    for l in range(depth):
        pk = _pack_layer(l, params)
        mod = mod_all[l]
        mods_p = [m.reshape(b, 1, d) for m in jnp.split(mod[:b], 6, axis=-1)]
        mods_s = [jnp.repeat(m, t, axis=0).reshape(1, nb * t, d) for m in jnp.split(mod[b:n_c], 6, axis=-1)]
        bias_p = _relbias(pk["rel"], p_bases, tile, tile, -(CHUNK - 1))
        bias_s = jnp.concatenate(
            [_relbias(pk["rel"], [rows_c], t, rows_c, -REL_CLIP)[:, 0],
             _relbias(pk["rel"], [0], t, LANES, -REL_CLIP)[:, 0]], axis=-1)
        caches_l = (cache_a_latent[l], cache_a_kpe[l], cache_b_k[l], cache_b_v[l], cache_c_k[l],
                    cache_c_v[l], cache_d_k[l], cache_d_v[l], cache_d_logf[l])
        x_p, st_p = _prompt_layer(x_p, mods_p, pk, tabs_p, bias_p, b, s, l)
        x_s, st_s = _sample_layer(x_s, mods_s, pk, tabs_s, bias_s, caches_l, nb, t, l)
        states_p.append(st_p)
        states_s.append(st_s)
    sp = [jnp.stack(z) for z in zip(*states_p)]
    ss = [jnp.stack(z) for z in zip(*states_s)]
    out = [x_p.reshape(b, s, d), x_s.reshape(nb, t, d)]
    for a, c in zip(sp, ss):
        out += [a, c]
    return tuple(out)
```

```python
import functools
import math

import jax
import jax.numpy as jnp
from jax import lax
from jax.experimental import pallas as pl
from jax.experimental.pallas import tpu as pltpu

BF = jnp.bfloat16
F32 = jnp.float32

CHUNK = 64
EPS = 1e-6
NEG_INF = -1e30
H = 4
A_NOPE, A_ROPE, A_VDIM = 128, 64, 128
A_Q_RANK, A_KV_RANK = 512, 256
A_QK_PAD = 256
ROPE_THETA = 10000.0
B_DH = 64
C_DH = 128
BAND_CHUNKS = 8
BAND_ROWS = BAND_CHUNKS * CHUNK
REL_CLIP = 128
D_DH = 128
HW = 512
N_BRANCH = 4
LANES = 128
PROJ_TILE = 512

VMEM_LIMIT_BYTES = 56 * 1024 * 1024


def _cparams(*sem):
    return pltpu.CompilerParams(dimension_semantics=sem, vmem_limit_bytes=VMEM_LIMIT_BYTES)


def _nt_dot(a, b):
    return lax.dot_general(a, b, (((1,), (1,)), ((), ())), preferred_element_type=F32)


def _dot(a, b):
    return jnp.dot(a, b, preferred_element_type=F32)


def _rms(z, n):
    ms = jnp.sum(z * z, axis=-1, keepdims=True) * (1.0 / n)
    return z * lax.rsqrt(ms + EPS)


def _rms_groups(z, gs):
    w = z.shape[-1]
    if gs >= LANES:
        parts = [_rms(z[:, g * gs:(g + 1) * gs], gs) for g in range(w // gs)]
        return parts[0] if len(parts) == 1 else jnp.concatenate(parts, axis=-1)
    assert gs * 2 == LANES
    lo = lax.broadcasted_iota(jnp.int32, (1, LANES), 1) < gs
    parts = []
    for g in range(w // LANES):
        zz = z[:, g * LANES:(g + 1) * LANES]
        sq = zz * zz
        s_lo = jnp.sum(jnp.where(lo, sq, 0.0), axis=-1, keepdims=True)
        s_hi = jnp.sum(jnp.where(lo, 0.0, sq), axis=-1, keepdims=True)
        ms = jnp.where(lo, s_lo, s_hi) * (1.0 / gs)
        parts.append(zz * lax.rsqrt(ms + EPS))
    return jnp.concatenate(parts, axis=-1)


def _rope128(r, cos, sin):
    half = A_ROPE // 2
    lane = lax.broadcasted_iota(jnp.int32, (1, LANES), 1)
    swapped = jnp.where(lane < half, pltpu.roll(r, LANES - half, 1), pltpu.roll(r, half, 1))
    return r * cos + swapped * sin


CHUNK_SHIFT = CHUNK.bit_length() - 1
assert 1 << CHUNK_SHIFT == CHUNK


def _chunk(pos):
    return jnp.right_shift(pos, CHUNK_SHIFT)


def _log_sigmoid(x):
    return jnp.minimum(x, 0.0) - jnp.log1p(jnp.exp(-jnp.abs(x)))


def _sigmoid(x):
    return 1.0 / (1.0 + jnp.exp(-x))


def _ada_kernel(c_ref, w_ref, b_ref, o_ref):
    c = c_ref[...]
    a = (c * _sigmoid(c)).astype(BF)
    o_ref[...] = _dot(a, w_ref[...].astype(BF)) + b_ref[...]


def _ada(c_all, w_ada, b_ada):
    depth, d, n = w_ada.shape
    r = c_all.shape[0]
    tn = 1024
    return pl.pallas_call(
        _ada_kernel,
        out_shape=jax.ShapeDtypeStruct((depth, r, n), F32),
        grid=(depth, n // tn),
        in_specs=[
            pl.BlockSpec((r, d), lambda l, j: (0, 0)),
            pl.BlockSpec((None, d, tn), lambda l, j: (l, 0, j)),
            pl.BlockSpec((None, 1, tn), lambda l, j: (l, 0, j)),
        ],
        out_specs=pl.BlockSpec((None, r, tn), lambda l, j: (l, 0, j)),
        compiler_params=_cparams("arbitrary", "arbitrary"),
        name="ada_mod",
    )(c_all, w_ada, b_ada.reshape(depth, 1, n))


WIN_TILE = PROJ_TILE
WIN_PROJ_STARTS = (0, A_Q_RANK, 832, 1344, 1856, 2368, 2880, 3392, 3904, 4416, 4928, 5440)
WIN_GATE_START = 5444
PROJ_TILES = len(WIN_PROJ_STARTS)


def _winprep_kernel(off_ref, w_hbm, o_ref, buf, sem, *, layer, nsteps):
    s = pl.program_id(0)
    slot = s % 2

    def copy(step, slot_):
        return pltpu.make_async_copy(w_hbm.at[pl.ds(off_ref[step], WIN_TILE), layer, :],
                                     buf.at[slot_], sem.at[slot_])

    @pl.when(s == 0)
    def _():
        copy(0, 0).start()

    @pl.when(s + 1 < nsteps)
    def _():
        copy(s + 1, 1 - slot).start()

    copy(s, slot).wait()
    o_ref[...] = buf[slot].T.astype(BF)


def _win_prep(w_in_t, layer, d):
    gate_tiles = N_BRANCH * d // WIN_TILE
    starts = WIN_PROJ_STARTS + tuple(WIN_GATE_START + g * WIN_TILE for g in range(gate_tiles))
    assert starts[-1] + WIN_TILE == w_in_t.shape[0]
    nsteps = len(starts)
    rows = w_in_t.shape[2]
    return pl.pallas_call(
        functools.partial(_winprep_kernel, layer=layer, nsteps=nsteps),
        out_shape=jax.ShapeDtypeStruct((rows, nsteps * WIN_TILE), BF),
        grid_spec=pltpu.PrefetchScalarGridSpec(
            num_scalar_prefetch=1,
            grid=(nsteps,),
            in_specs=[pl.BlockSpec(memory_space=pl.ANY)],
            out_specs=pl.BlockSpec((rows, WIN_TILE), lambda s, off: (0, s)),
            scratch_shapes=[pltpu.VMEM((2, WIN_TILE, rows), F32), pltpu.SemaphoreType.DMA((2,))],
        ),
        compiler_params=_cparams("arbitrary"),
        name="w_in_repack",
    )(jnp.asarray(starts, jnp.int32), w_in_t)


def _proj_kernel(x_ref, sc_ref, sh_ref, ng_ref, w_ref, gq_ref, gkv_ref, gkpe_ref, gb_ref, gc_ref,
                 gd_ref, fb_ref, cos_ref, sin_ref,
                 h_out, cq_out, ckv_out, kpe_out, kpep_out, logf_out,
                 qb_out, kb_out, vb_out, qc_out, kc_out, vc_out, qd_out, kd_out, vd_out,
                 h_scr):
    j = pl.program_id(1)

    @pl.when(j == 0)
    def _():
        x = x_ref[...]
        y = _rms(x, x.shape[-1]) * ng_ref[...]
        hb = (y * (1.0 + sc_ref[...]) + sh_ref[...]).astype(BF)
        h_scr[...] = hb
        h_out[...] = hb

    z = _dot(h_scr[...], w_ref[...])

    @pl.when(j == 0)
    def _():
        cq_out[...] = (_rms(z, A_Q_RANK) * gq_ref[...]).astype(BF)

    @pl.when(j == 1)
    def _():
        ckv_out[...] = _rms(z[:, :A_KV_RANK], A_KV_RANK) * gkv_ref[...]
        lane = lax.broadcasted_iota(jnp.int32, (1, LANES), 1)
        kp = jnp.where(lane < A_ROPE, z[:, A_KV_RANK:A_KV_RANK + LANES], 0.0)
        kp = _rope128(_rms(kp, A_ROPE) * gkpe_ref[...], cos_ref[...], sin_ref[...])
        kpe_out[...] = kp[:, :A_ROPE]
        kpep_out[...] = kp.astype(BF)

    @pl.when(j == PROJ_TILES - 1)
    def _():
        logf_out[...] = _log_sigmoid(z[:, :LANES] + fb_ref[...])[:, :H]

    def normed(gs, g_ref):
        return _rms_groups(z, gs) * g_ref[...]

    @pl.when(j == 2)
    def _():
        qb_out[...] = normed(B_DH, gb_ref.at[0:1]).astype(BF)

    @pl.when(j == 3)
    def _():
        kb_out[...] = normed(B_DH, gb_ref.at[1:2])

    @pl.when(j == 4)
    def _():
        vb_out[...] = z

    @pl.when(j == 5)
    def _():
        qc_out[...] = normed(C_DH, gc_ref.at[0:1]).astype(BF)

    @pl.when(j == 6)
    def _():
        kc_out[...] = normed(C_DH, gc_ref.at[1:2])

    @pl.when(j == 7)
    def _():
        vc_out[...] = z

    @pl.when(j == 8)
    def _():
        qd_out[...] = normed(D_DH, gd_ref.at[0:1]).astype(BF)

    @pl.when(j == 9)
    def _():
        kd_out[...] = normed(D_DH, gd_ref.at[1:2])

    @pl.when(j == 10)
    def _():
        vd_out[...] = z


def _proj(x2d, sc, sh, ng, wp, gq, gkv, gkpe, gb, gc, gd, fb, cos_t, sin_t, tm):
    m, d = x2d.shape
    groups, rows = sc.shape[0], sc.shape[1]
    tiles_per_group = (m // groups) // tm
    tab_blocks = cos_t.shape[0] // tm
    row = lambda i, j: (i, 0)
    const2 = lambda i, j: (0, 0)
    mod_spec = pl.BlockSpec((None, rows, d), lambda i, j: (i // tiles_per_group, 0, 0))
    tab_spec = pl.BlockSpec((tm, LANES), lambda i, j: (i % tab_blocks, 0))

    def out(width, dtype):
        return jax.ShapeDtypeStruct((m, width), dtype), pl.BlockSpec((tm, width), row)

    outs = [out(d, BF), out(A_Q_RANK, BF), out(A_KV_RANK, F32), out(A_ROPE, F32), out(LANES, BF),
            out(H, F32),
            out(HW, BF), out(HW, F32), out(HW, F32),
            out(HW, BF), out(HW, F32), out(HW, F32),
            out(HW, BF), out(HW, F32), out(HW, F32)]
    return pl.pallas_call(
        _proj_kernel,
        out_shape=[o[0] for o in outs],
        grid=(m // tm, PROJ_TILES),
        in_specs=[
            pl.BlockSpec((tm, d), row),
            mod_spec, mod_spec,
            pl.BlockSpec((1, d), const2),
            pl.BlockSpec((d, PROJ_TILE), lambda i, j: (0, j)),
            pl.BlockSpec((1, A_Q_RANK), const2),
            pl.BlockSpec((1, A_KV_RANK), const2),
            pl.BlockSpec((1, LANES), const2),
            pl.BlockSpec((2, HW), const2),
            pl.BlockSpec((2, HW), const2),
            pl.BlockSpec((2, HW), const2),
            pl.BlockSpec((1, LANES), const2),
            tab_spec, tab_spec,
        ],
        out_specs=[o[1] for o in outs],
        scratch_shapes=[pltpu.VMEM((tm, d), BF)],
        compiler_params=_cparams("arbitrary", "arbitrary"),
        name="in_proj",
    )(x2d, sc, sh, ng, wp, gq, gkv, gkpe, gb, gc, gd, fb, cos_t, sin_t)


def _aprep_kernel(cq_ref, ckv_ref, kpep_ref, wuq_ref, wukv_ref, gqn_ref, gqr_ref, gkn_ref,
                  cos_ref, sin_ref, q_out, k_out, v_out):
    zq = _dot(cq_ref[...], wuq_ref[...])
    cos, sin = cos_ref[...], sin_ref[...]
    parts = []
    for h in range(H):
        nope = zq[:, h * A_QK_PAD:h * A_QK_PAD + A_NOPE]
        rp = zq[:, h * A_QK_PAD + A_NOPE:(h + 1) * A_QK_PAD]
        parts.append(_rms(nope, A_NOPE) * gqn_ref[...])
        parts.append(_rope128(_rms(rp, A_ROPE) * gqr_ref[...], cos, sin))
    q_out[...] = jnp.concatenate(parts, axis=-1).astype(BF)

    zkv = _dot(ckv_ref[...].astype(BF), wukv_ref[...])
    kp = kpep_ref[...]
    parts = []
    for h in range(H):
        kn = _rms(zkv[:, h * A_NOPE:(h + 1) * A_NOPE], A_NOPE) * gkn_ref[...]
        parts.append(kn.astype(BF))
        parts.append(kp)
    k_out[...] = jnp.concatenate(parts, axis=-1)
    v_out[...] = zkv[:, H * A_NOPE:].astype(BF)


def _aprep(cqn, ckvn, kpep, wuq, wukv, gqn, gqr, gkn, cos_t, sin_t, tm):
    m = cqn.shape[0]
    tab_blocks = cos_t.shape[0] // tm
    row = lambda i: (i, 0)
    const = lambda i: (0, 0)
    tab_spec = pl.BlockSpec((tm, LANES), lambda i: (i % tab_blocks, 0))
    return pl.pallas_call(
        _aprep_kernel,
        out_shape=[jax.ShapeDtypeStruct((m, H * A_QK_PAD), BF),
                   jax.ShapeDtypeStruct((m, H * A_QK_PAD), BF),
                   jax.ShapeDtypeStruct((m, HW), BF)],
        grid=(m // tm,),
        in_specs=[
            pl.BlockSpec((tm, A_Q_RANK), row),
            pl.BlockSpec((tm, A_KV_RANK), row),
            pl.BlockSpec((tm, LANES), row),
            pl.BlockSpec(wuq.shape, const),
            pl.BlockSpec(wukv.shape, const),
            pl.BlockSpec((1, LANES), const),
            pl.BlockSpec((1, LANES), const),
            pl.BlockSpec((1, LANES), const),
            tab_spec, tab_spec,
        ],
        out_specs=[pl.BlockSpec((tm, H * A_QK_PAD), row),
                   pl.BlockSpec((tm, H * A_QK_PAD), row),
                   pl.BlockSpec((tm, HW), row)],
        compiler_params=_cparams("arbitrary"),
        name="a_prep",
    )(cqn, ckvn, kpep, wuq, wukv, gqn, gqr, gkn, cos_t, sin_t)


CUM_BLOCK = 256


def _tri_upper(n):
    r = lax.broadcasted_iota(jnp.int32, (n, n), 0)
    c = lax.broadcasted_iota(jnp.int32, (n, n), 1)
    return jnp.where(r <= c, 1.0, 0.0).astype(BF)


def _dot3(x, u):
    hi = x.astype(BF)
    r1 = x - hi.astype(F32)
    mid = r1.astype(BF)
    lo = (r1 - mid.astype(F32)).astype(BF)
    return _dot(hi, u) + _dot(mid, u) + _dot(lo, u)


def _cumsum_lanes(src_ref, dst_ref):
    u = _tri_upper(CUM_BLOCK)
    carry = jnp.zeros((8, 1), F32)
    for b in range(src_ref.shape[-1] // CUM_BLOCK):
        blk = slice(b * CUM_BLOCK, (b + 1) * CUM_BLOCK)
        c = _dot3(src_ref[:, blk], u) + carry
        dst_ref[:, blk] = c
        carry = c[:, CUM_BLOCK - 1:CUM_BLOCK]
    return carry


def _cumsum_kernel(x_ref, o_ref):
    _cumsum_lanes(x_ref, o_ref)


def _cumsum_rows(x):
    b, r, s = x.shape
    return pl.pallas_call(
        _cumsum_kernel,
        out_shape=jax.ShapeDtypeStruct(x.shape, F32),
        grid=(b,),
        in_specs=[pl.BlockSpec((None, r, s), lambda i: (i, 0, 0))],
        out_specs=pl.BlockSpec((None, r, s), lambda i: (i, 0, 0)),
        compiler_params=_cparams("arbitrary"),
        name="forget_cumsum",
    )(x)


def _relbias_kernel(tab_ref, o_ref, *, bases, rows, cols, dmin):
    h = pl.program_id(0)
    r = lax.broadcasted_iota(jnp.int32, (rows, cols), 0)
    c = lax.broadcasted_iota(jnp.int32, (rows, cols), 1)
    for t, base in enumerate(bases):
        idx = jnp.clip(base + r - c, -REL_CLIP, REL_CLIP) + REL_CLIP
        lo = min(max(base - (cols - 1), max(dmin, -REL_CLIP)), REL_CLIP) + REL_CLIP
        hi = min(max(base + rows - 1, -REL_CLIP), REL_CLIP) + REL_CLIP

        def body(e, acc):
            return jnp.where(idx == e, tab_ref[h, e], acc)

        o_ref[t] = lax.fori_loop(lo, hi + 1, body, jnp.zeros((rows, cols), F32))


def _relbias(table, bases, rows, cols, dmin):
    nh = table.shape[0]
    return pl.pallas_call(
        functools.partial(_relbias_kernel, bases=tuple(bases), rows=rows, cols=cols, dmin=dmin),
        out_shape=jax.ShapeDtypeStruct((nh, len(bases), rows, cols), F32),
        grid=(nh,),
        in_specs=[pl.BlockSpec(memory_space=pltpu.SMEM)],
        out_specs=pl.BlockSpec((None, len(bases), rows, cols), lambda h: (h, 0, 0, 0)),
        compiler_params=_cparams("arbitrary"),
        name="rel_bias_tiles",
    )(table)


def _online_update(s, v, m_ref, l_ref, acc_ref, idx):
    m_prev = m_ref[idx]
    m_new = jnp.maximum(m_prev, jnp.max(s, axis=-1, keepdims=True))
    alpha = jnp.exp(m_prev - m_new)
    p = jnp.exp(s - m_new)
    l_ref[idx] = alpha * l_ref[idx] + jnp.sum(p, axis=-1, keepdims=True)
    acc_ref[idx] = alpha * acc_ref[idx] + _dot(p.astype(BF), v)
    m_ref[idx] = m_new


def _diff_lambda(lam_ref, layer_idx):
    lam_init = 0.8 - 0.6 * math.exp(-0.3 * layer_idx)
    lp = lam_ref[...]
    a = jnp.sum(lp[0:1] * lp[1:2], keepdims=True)
    b = jnp.sum(lp[2:3] * lp[3:4], keepdims=True)
    return jnp.exp(a) - jnp.exp(b) + lam_init, lam_init


ALIBI_SLOPES = tuple(2.0 ** (-8.0 * (h + 1) / H) for h in range(H))


def _pattn_kernel(*refs, mode, tq, tk, nk, layer_idx, back):
    if mode == "a":
        q_ref, k_ref, v_ref, o_ref, m_ref, l_ref, acc_ref = refs
        dq, scale = A_QK_PAD, (A_NOPE + A_ROPE) ** -0.5
    elif mode == "b":
        q_ref, k_ref, v_ref, lam_ref, sg_ref, o_ref, m_ref, l_ref, acc_ref = refs
        dq, scale = 2 * B_DH, B_DH ** -0.5
    elif mode == "c":
        q_ref, k_ref, v_ref, bias_ref, o_ref, m_ref, l_ref, acc_ref = refs
        dq, scale = C_DH, C_DH ** -0.5
    else:
        q_ref, k_ref, v_ref, fq_ref, fk_ref, o_ref, m_ref, l_ref, acc_ref = refs
        dq, scale = D_DH, D_DH ** -0.5
    dv = HW // H
    i = pl.program_id(1)
    j = pl.program_id(2)
    if mode == "c":
        kb = i - back + j
        valid = kb >= 0
        first = jnp.maximum(back - i, 0)
        last = nk - 1
    else:
        last = ((i + 1) * tq - 1) // tk
        kb = j
        valid = j <= last
        first = 0

    @pl.when(j == first)
    def _():
        m_ref[...] = jnp.full(m_ref.shape, -jnp.inf, F32)
        l_ref[...] = jnp.zeros(l_ref.shape, F32)
        acc_ref[...] = jnp.zeros(acc_ref.shape, F32)

    @pl.when(valid)
    def _():
        qpos = i * tq + lax.broadcasted_iota(jnp.int32, (tq, tk), 0)
        kpos = kb * tk + lax.broadcasted_iota(jnp.int32, (tq, tk), 1)
        if mode in ("a", "b"):
            mask = _chunk(kpos) <= _chunk(qpos)
        elif mode == "c":
            qc = _chunk(qpos)
            kc = _chunk(kpos)
            mask = (kc <= qc) & (kc >= qc - BAND_CHUNKS)
        else:
            mask = kpos <= qpos
        if mode == "b":
            dist = jnp.abs(qpos - kpos).astype(F32)
            lo = lax.broadcasted_iota(jnp.int32, (1, dq), 1) < B_DH
        for h in range(H):
            q = q_ref[:, h * dq:(h + 1) * dq]
            k = k_ref[:, h * dq:(h + 1) * dq].astype(BF)
            v = v_ref[:, h * dv:(h + 1) * dv].astype(BF)
            if mode == "b":
                bias = -ALIBI_SLOPES[h] * dist
                zero = jnp.zeros_like(q)
                s1 = _nt_dot(jnp.where(lo, q, zero), k) * scale + bias
                s2 = _nt_dot(jnp.where(lo, zero, q), k) * scale + bias
                _online_update(jnp.where(mask, s1, NEG_INF), v, m_ref, l_ref, acc_ref, h)
                _online_update(jnp.where(mask, s2, NEG_INF), v, m_ref, l_ref, acc_ref, H + h)
                continue
            s = _nt_dot(q, k) * scale
            if mode == "c":
                s = s + bias_ref[h]
            elif mode == "d":
                s = s + (fq_ref[:, h:h + 1] - fk_ref[h:h + 1, :])
            _online_update(jnp.where(mask, s, NEG_INF), v, m_ref, l_ref, acc_ref, h)

    @pl.when(j == last)
    def _():
        for h in range(H):
            o = acc_ref[h] / l_ref[h]
            if mode == "b":
                lam, lam_init = _diff_lambda(lam_ref, layer_idx)
                o = o - lam * (acc_ref[H + h] / l_ref[H + h])
                o = (_rms(o, dv) * sg_ref[...]) * (1.0 - lam_init)
            o_ref[:, h * dv:(h + 1) * dv] = o.astype(BF)


def _pattn(mode, q, k, v, extras, layer_idx=0):
    b, s, qw = q.shape
    kw = k.shape[-1]
    back = 0
    if mode == "c":
        tq = tk = 256
        back = BAND_ROWS // tk
        nk = back + 1
        kmap = lambda bi, i, j: (bi, jnp.maximum(i - back + j, 0), 0)
    else:
        tq = tk = 512
        nk = s // tk
        kmap = lambda bi, i, j: (bi, jnp.minimum(j, ((i + 1) * tq - 1) // tk), 0)
    qmap = lambda bi, i, j: (bi, i, 0)
    in_specs = [pl.BlockSpec((None, tq, qw), qmap),
                pl.BlockSpec((None, tk, kw), kmap),
                pl.BlockSpec((None, tk, HW), kmap)]
    if mode == "b":
        lam, sg = extras
        in_specs += [pl.BlockSpec(lam.shape, lambda bi, i, j: (0, 0)),
                     pl.BlockSpec(sg.shape, lambda bi, i, j: (0, 0))]
    elif mode == "c":
        (bias,) = extras
        in_specs += [pl.BlockSpec((H, None, tq, tk), lambda bi, i, j: (0, j, 0, 0))]
    elif mode == "d":
        fcol, frow = extras
        in_specs += [pl.BlockSpec((None, tq, H), qmap),
                     pl.BlockSpec((None, 8, tk), lambda bi, i, j: (bi, 0, kmap(bi, i, j)[1]))]
    nstate = 2 * H if mode == "b" else H
    return pl.pallas_call(
        functools.partial(_pattn_kernel, mode=mode, tq=tq, tk=tk, nk=nk, layer_idx=layer_idx, back=back),
        out_shape=jax.ShapeDtypeStruct((b, s, HW), BF),
        grid=(b, s // tq, nk),
        in_specs=in_specs,
        out_specs=pl.BlockSpec((None, tq, HW), qmap),
        scratch_shapes=[pltpu.VMEM((nstate, tq, 1), F32),
                        pltpu.VMEM((nstate, tq, 1), F32),
                        pltpu.VMEM((nstate, tq, HW // H), F32)],
        compiler_params=_cparams("arbitrary", "arbitrary", "arbitrary"),
        name="prompt_attn_" + mode,
    )(q, k, v, *extras)


def _pad_rows(x, rows):
    return jnp.concatenate([x, jnp.zeros((rows - x.shape[0], x.shape[1]), x.dtype)], axis=0)


def _prefetch_head_caches(k_hbm, v_hbm, kbuf, vbuf, sem, layer, nb):
    b = pl.program_id(0)
    slot = b % 2
    dh = kbuf.shape[-1] // H

    def copies(bidx, slot_):
        out = []
        for h in range(H):
            cols = pl.ds(h * dh, dh)
            out.append(pltpu.make_async_copy(k_hbm.at[layer, bidx, :, h, :], kbuf.at[slot_, :, cols],
                                             sem.at[slot_, 0]))
            out.append(pltpu.make_async_copy(v_hbm.at[layer, bidx, :, h, :], vbuf.at[slot_, :, cols],
                                             sem.at[slot_, 1]))
        return out

    @pl.when(b == 0)
    def _():
        for c in copies(0, 0):
            c.start()

    @pl.when(b + 1 < nb)
    def _():
        for c in copies(b + 1, 1 - slot):
            c.start()

    for c in copies(b, slot):
        c.wait()
    return slot


def _sattn_kernel(*refs, mode, t, past, layer, nb):
    if mode == "a":
        (q_ref, lat_ref, kpet_ref, kn_ref, vn_ref, wukv_ref, gkn_ref, o_ref) = refs
        dq, scale = A_QK_PAD, (A_NOPE + A_ROPE) ** -0.5
        rows_c = lat_ref.shape[0]
    else:
        if mode == "b":
            (q_ref, k_hbm, v_hbm, kn_ref, vn_ref, lam_ref, sg_ref, o_ref, kbuf, vbuf, sem) = refs
            dq, scale = 2 * B_DH, B_DH ** -0.5
        elif mode == "c":
            (q_ref, k_hbm, v_hbm, kn_ref, vn_ref, bias_ref, o_ref, kbuf, vbuf, sem) = refs
            dq, scale = C_DH, C_DH ** -0.5
        else:
            (q_ref, k_hbm, v_hbm, kn_ref, vn_ref, lfc_ref, lfn_ref, o_ref, f_scr, kbuf, vbuf, sem) = refs
            dq, scale = D_DH, D_DH ** -0.5
        rows_c = kbuf.shape[1]
        slot = _prefetch_head_caches(k_hbm, v_hbm, kbuf, vbuf, sem, layer, nb)
        kc_ref, vc_ref = kbuf.at[slot], vbuf.at[slot]
    dv = HW // H
    npad = LANES
    qpos_c = past + lax.broadcasted_iota(jnp.int32, (t, rows_c), 0)
    kpos_c = (past - rows_c) + lax.broadcasted_iota(jnp.int32, (t, rows_c), 1)
    qpos_n = past + lax.broadcasted_iota(jnp.int32, (t, npad), 0)
    col_n = lax.broadcasted_iota(jnp.int32, (t, npad), 1)
    kpos_n = past + col_n
    real_n = col_n < t
    if mode in ("a", "b"):
        mask_c = _chunk(kpos_c) <= _chunk(qpos_c)
        mask_n = real_n & (_chunk(kpos_n) <= _chunk(qpos_n))
    elif mode == "c":
        qc_c, kc_c = _chunk(qpos_c), _chunk(kpos_c)
        qc_n, kc_n = _chunk(qpos_n), _chunk(kpos_n)
        mask_c = (kpos_c >= 0) & (kc_c <= qc_c) & (kc_c >= qc_c - BAND_CHUNKS)
        mask_n = real_n & (kc_n <= qc_n) & (kc_n >= qc_n - BAND_CHUNKS)
    else:
        mask_c = kpos_c <= qpos_c
        mask_n = real_n & (kpos_n <= qpos_n)

    if mode == "a":
        lat = lat_ref[...].astype(BF)
        kpe_t = kpet_ref[...]
        kpe_t = jnp.concatenate([kpe_t, jnp.zeros_like(kpe_t)], axis=0).astype(BF)
    if mode == "b":
        dist_c = jnp.abs(qpos_c - kpos_c).astype(F32)
        dist_n = jnp.abs(qpos_n - kpos_n).astype(F32)
        lo = lax.broadcasted_iota(jnp.int32, (1, dq), 1) < B_DH
        lam, lam_init = _diff_lambda(lam_ref, layer)
    if mode == "d":
        carry = _cumsum_lanes(lfc_ref, f_scr)
        f_new = _dot3(lfn_ref[...], _tri_upper(LANES)) + carry
        eye = (lax.broadcasted_iota(jnp.int32, (t, npad), 0) == col_n)

    def softmax_pv(s_c, s_n, v_c, v_n):
        m = jnp.maximum(jnp.max(s_c, axis=-1, keepdims=True), jnp.max(s_n, axis=-1, keepdims=True))
        p_c = jnp.exp(s_c - m)
        p_n = jnp.exp(s_n - m)
        l = jnp.sum(p_c, axis=-1, keepdims=True) + jnp.sum(p_n, axis=-1, keepdims=True)
        return (_dot(p_c.astype(BF), v_c) + _dot(p_n.astype(BF), v_n)) / l

    for h in range(H):
        q = q_ref[:, h * dq:(h + 1) * dq]
        if mode == "a":
            kn = _pad_rows(kn_ref[:, h * dq:(h + 1) * dq], npad)
            vn = _pad_rows(vn_ref[:, h * dv:(h + 1) * dv], npad)
            w = wukv_ref[...]
            k_nope = _dot(lat, w[:, h * A_NOPE:(h + 1) * A_NOPE])
            k_nope = (_rms(k_nope, A_NOPE) * gkn_ref[...]).astype(BF)
            v_c = _dot(lat, w[:, H * A_NOPE + h * A_VDIM:H * A_NOPE + (h + 1) * A_VDIM]).astype(BF)
            s_c = (_nt_dot(q[:, :A_NOPE], k_nope) + _dot(q[:, A_NOPE:], kpe_t)) * scale
            s_n = _nt_dot(q, kn) * scale
        else:
            k_c = kc_ref[:, h * dq:(h + 1) * dq].astype(BF)
            v_c = vc_ref[:, h * dv:(h + 1) * dv].astype(BF)
            kn = _pad_rows(kn_ref[:, h * dq:(h + 1) * dq], npad).astype(BF)
            vn = _pad_rows(vn_ref[:, h * dv:(h + 1) * dv], npad).astype(BF)
        if mode == "b":
            zero = jnp.zeros_like(q)
            q1, q2 = jnp.where(lo, q, zero), jnp.where(lo, zero, q)
            bias_c = -ALIBI_SLOPES[h] * dist_c
            bias_n = -ALIBI_SLOPES[h] * dist_n
            outs = []
            for qq in (q1, q2):
                s_c = jnp.where(mask_c, _nt_dot(qq, k_c) * scale + bias_c, NEG_INF)
                s_n = jnp.where(mask_n, _nt_dot(qq, kn) * scale + bias_n, NEG_INF)
                outs.append(softmax_pv(s_c, s_n, v_c, vn))
            o = outs[0] - lam * outs[1]
            o = (_rms(o, dv) * sg_ref[...]) * (1.0 - lam_init)
        else:
            if mode != "a":
                s_c = _nt_dot(q, k_c) * scale
                s_n = _nt_dot(q, kn) * scale
            if mode == "c":
                s_c = s_c + bias_ref[h, :, :rows_c]
                s_n = s_n + bias_ref[h, :, rows_c:]
            elif mode == "d":
                fq = jnp.sum(jnp.where(eye, f_new[h:h + 1, :], 0.0), axis=-1, keepdims=True)
                s_c = s_c + (fq - f_scr[h:h + 1, :])
                s_n = s_n + (fq - f_new[h:h + 1, :])
            s_c = jnp.where(mask_c, s_c, NEG_INF)
            s_n = jnp.where(mask_n, s_n, NEG_INF)
            o = softmax_pv(s_c, s_n, v_c, vn)
        o_ref[:, h * dv:(h + 1) * dv] = o.astype(BF)


def _sattn(mode, layer, q, cache_k, cache_v, kn, vn, extras, t, past):
    nb = cache_k.shape[1]
    qw = q.shape[-1]
    row = lambda b: (b, 0)
    if mode == "a":
        cache_specs = [pl.BlockSpec((None, None) + cache_k.shape[2:], lambda b: (layer, b, 0, 0)),
                       pl.BlockSpec((None, None) + cache_v.shape[2:], lambda b: (layer, b, 0, 0))]
        dma_scratch = []
    else:
        rows_c = cache_k.shape[2]
        cache_specs = [pl.BlockSpec(memory_space=pl.ANY), pl.BlockSpec(memory_space=pl.ANY)]
        dma_scratch = [pltpu.VMEM((2, rows_c, HW), F32), pltpu.VMEM((2, rows_c, HW), F32),
                       pltpu.SemaphoreType.DMA((2, 2))]
    in_specs = [pl.BlockSpec((t, qw), row)] + cache_specs + [
                pl.BlockSpec((t, kn.shape[-1]), row),
                pl.BlockSpec((t, vn.shape[-1]), row)]
    scratch = []
    if mode == "d":
        lfc, lfn = extras
        in_specs += [pl.BlockSpec((None,) + lfc.shape[1:], lambda b: (b, 0, 0)),
                     pl.BlockSpec((None,) + lfn.shape[1:], lambda b: (b, 0, 0))]
        scratch = [pltpu.VMEM(lfc.shape[1:], F32)]
    elif mode == "c":
        (bias,) = extras
        in_specs += [pl.BlockSpec(bias.shape, lambda b: (0, 0, 0))]
    else:
        in_specs += [pl.BlockSpec(e.shape, lambda b: (0, 0)) for e in extras]
    return pl.pallas_call(
        functools.partial(_sattn_kernel, mode=mode, t=t, past=past, layer=layer, nb=nb),
        out_shape=jax.ShapeDtypeStruct((nb * t, HW), BF),
        grid=(nb,),
        in_specs=in_specs,
        out_specs=pl.BlockSpec((t, HW), row),
        scratch_shapes=scratch + dma_scratch,
        compiler_params=_cparams("arbitrary"),
        name="sample_attn_" + mode,
    )(q, cache_k, cache_v, kn, vn, *extras)


def _gmerge_kernel(h_ref, oa_ref, ob_ref, oc_ref, od_ref, wg0_ref, wg1_ref, wg2_ref, wg3_ref,
                   wb_ref, out_ref):
    hb = h_ref[...]
    acc = None
    for g, (o_ref, wg_ref) in enumerate(((oa_ref, wg0_ref), (ob_ref, wg1_ref),
                                         (oc_ref, wg2_ref), (od_ref, wg3_ref))):
        term = _sigmoid(_dot(hb, wg_ref[...])) * _dot(o_ref[...], wb_ref[g])
        acc = term if acc is None else acc + term
    out_ref[...] = acc.astype(BF)


def _gmerge(hb, outs, wg, wb, tm):
    m, d = hb.shape
    tn = WIN_TILE
    nt = d // tn
    row = lambda i, n: (i, 0)
    wg_specs = [pl.BlockSpec((d, tn), functools.partial(lambda i, n, g: (0, PROJ_TILES + g * nt + n), g=g))
                for g in range(N_BRANCH)]
    return pl.pallas_call(
        _gmerge_kernel,
        out_shape=jax.ShapeDtypeStruct((m, d), BF),
        grid=(m // tm, nt),
        in_specs=[pl.BlockSpec((tm, d), row)] + [pl.BlockSpec((tm, HW), row)] * N_BRANCH + wg_specs
                 + [pl.BlockSpec((N_BRANCH, HW, tn), lambda i, n: (0, 0, n))],
        out_specs=pl.BlockSpec((tm, tn), lambda i, n: (i, n)),
        compiler_params=_cparams("arbitrary", "arbitrary"),
        name="gate_merge",
    )(hb, *outs, wg, wg, wg, wg, wb)


def _oproj_kernel(mix_ref, x_ref, g_ref, w_ref, o_ref):
    o_ref[...] = x_ref[...] + g_ref[...] * _dot(mix_ref[...], w_ref[...])


def _oproj(mix, x2d, gate, w, tm):
    m, d = x2d.shape
    rows = gate.shape[1]
    tiles_per_group = (m // gate.shape[0]) // tm
    row = lambda i: (i, 0)
    return pl.pallas_call(
        _oproj_kernel,
        out_shape=jax.ShapeDtypeStruct((m, d), F32),
        grid=(m // tm,),
        in_specs=[pl.BlockSpec((tm, d), row), pl.BlockSpec((tm, d), row),
                  pl.BlockSpec((None, rows, d), lambda i: (i // tiles_per_group, 0, 0)),
                  pl.BlockSpec((d, d), lambda i: (0, 0))],
        out_specs=pl.BlockSpec((tm, d), row),
        compiler_params=_cparams("arbitrary"),
        name="out_proj",
    )(mix, x2d, gate, w)


def _ffn_kernel(x_ref, sc_ref, sh_ref, g_ref, ng_ref, wu_ref, wd_ref, o_ref, h_scr, *, nf):
    f = pl.program_id(1)

    @pl.when(f == 0)
    def _():
        x = x_ref[...]
        y = _rms(x, x.shape[-1]) * ng_ref[...]
        h_scr[...] = (y * (1.0 + sc_ref[...]) + sh_ref[...]).astype(BF)
        o_ref[...] = jnp.zeros(o_ref.shape, F32)

    u = jnp.maximum(_dot(h_scr[...], wu_ref[...]), 0.0)
    o_ref[...] += _dot((u * u).astype(BF), wd_ref[...])

    @pl.when(f == nf - 1)
    def _():
        o_ref[...] = x_ref[...] + g_ref[...] * o_ref[...]


def _ffn(x2d, sc, sh, gate, ng, wu, wd, tm):
    m, d = x2d.shape
    dff = wu.shape[1]
    tf = 1024
    rows = sc.shape[1]
    tiles_per_group = (m // sc.shape[0]) // tm
    row = lambda i, f: (i, 0)
    mod_spec = pl.BlockSpec((None, rows, d), lambda i, f: (i // tiles_per_group, 0, 0))
    return pl.pallas_call(
        functools.partial(_ffn_kernel, nf=dff // tf),
        out_shape=jax.ShapeDtypeStruct((m, d), F32),
        grid=(m // tm, dff // tf),
        in_specs=[pl.BlockSpec((tm, d), row), mod_spec, mod_spec, mod_spec,
                  pl.BlockSpec((1, d), lambda i, f: (0, 0)),
                  pl.BlockSpec((d, tf), lambda i, f: (0, f)),
                  pl.BlockSpec((tf, d), lambda i, f: (f, 0))],
        out_specs=pl.BlockSpec((tm, d), row),
        scratch_shapes=[pltpu.VMEM((tm, d), BF)],
        compiler_params=_cparams("arbitrary", "arbitrary"),
        name="ffn",
    )(x2d, sc, sh, gate, ng, wu, wd)


def _pack_layer(l, p, w_in_t):
    w_all = _win_prep(w_in_t, l, w_in_t.shape[2])

    qk = A_NOPE + A_ROPE
    wuq = p["w_a_uq"][l].reshape(A_Q_RANK, H, qk)
    wuq = jnp.pad(wuq, ((0, 0), (0, 0), (0, A_QK_PAD - qk))).reshape(A_Q_RANK, H * A_QK_PAD).astype(BF)
    wukv = p["w_a_ukv"][l].reshape(A_KV_RANK, H, A_NOPE + A_VDIM)
    wukv = jnp.concatenate([wukv[:, :, :A_NOPE].reshape(A_KV_RANK, H * A_NOPE),
                            wukv[:, :, A_NOPE:].reshape(A_KV_RANK, H * A_VDIM)], axis=1).astype(BF)

    def row(v, width=None):
        v = v.reshape(1, -1).astype(F32)
        if width is not None and v.shape[1] < width:
            v = jnp.pad(v, ((0, 0), (0, width - v.shape[1])))
        return v

    def head_rows(gq, gk, reps):
        return jnp.stack([jnp.tile(gq, reps), jnp.tile(gk, reps)]).astype(F32)

    return dict(
        w_all=w_all, wuq=wuq, wukv=wukv,
        ng1=row(p["norm1_g"][l]), ng2=row(p["norm2_g"][l]),
        gq=row(p["a_q_norm_g"][l]), gkv=row(p["a_kv_norm_g"][l]),
        gkpe=row(p["a_k_gain"][l][A_NOPE:], LANES),
        gqn=row(p["a_q_gain"][l][:A_NOPE]), gqr=row(p["a_q_gain"][l][A_NOPE:], LANES),
        gkn=row(p["a_k_gain"][l][:A_NOPE]),
        gb=head_rows(p["b_q_gain"][l], p["b_k_gain"][l], HW // B_DH),
        gc=head_rows(p["c_q_gain"][l], p["c_k_gain"][l], H),
        gd=head_rows(p["d_q_gain"][l], p["d_k_gain"][l], H),
        fb=row(p["d_forget_b"][l], LANES),
        lam=p["b_lambda"][l].astype(F32), sg=row(p["b_subln_g"][l]),
        rel=p["c_rel_bias"][l].astype(F32),
        wb=p["w_branch"][l].astype(BF), wo=p["w_out"][l].astype(BF),
        wu=p["w_up"][l].astype(BF), wd=p["w_down"][l].astype(BF),
    )


def _rope_tables(pos):
    half = A_ROPE // 2
    inv = ROPE_THETA ** (-jnp.arange(half, dtype=F32) / half)
    ang = pos.astype(F32)[:, None] * inv[None, :]
    cos, sin = jnp.cos(ang), jnp.sin(ang)
    z = jnp.zeros((pos.shape[0], LANES - A_ROPE), F32)
    return jnp.concatenate([cos, cos, z], axis=1), jnp.concatenate([-sin, sin, z], axis=1)


def _heads_to_rows(x, lanes):
    b, t, h = x.shape
    return jnp.pad(jnp.swapaxes(x, 1, 2), ((0, 0), (0, 8 - h), (0, lanes - t)))


def _front(x2d, mods, pk, tabs, tm):
    sh1, sc1 = mods[0], mods[1]
    cos_t, sin_t = tabs
    (hb, cqn, ckvn, kpe, kpep, logf, qb, kb, vb, qc, kc, vc, qd, kd, vd) = _proj(
        x2d, sc1, sh1, pk["ng1"], pk["w_all"], pk["gq"], pk["gkv"], pk["gkpe"], pk["gb"], pk["gc"],
        pk["gd"], pk["fb"], cos_t, sin_t, tm)
    qa, ka, va = _aprep(cqn, ckvn, kpep, pk["wuq"], pk["wukv"], pk["gqn"], pk["gqr"], pk["gkn"],
                        cos_t, sin_t, tm)
    return dict(h=hb, ckv=ckvn, kpe=kpe, logf=logf, qa=qa, ka=ka, va=va, qb=qb, kb=kb, vb=vb,
                qc=qc, kc=kc, vc=vc, qd=qd, kd=kd, vd=vd)


def _back(x2d, hb, outs, mods, pk, tm):
    g1, sh2, sc2, g2 = mods[2], mods[3], mods[4], mods[5]
    mix = _gmerge(hb, outs, pk["w_all"], pk["wb"], tm)
    x2d = _oproj(mix, x2d, g1, pk["wo"], tm)
    return _ffn(x2d, sc2, sh2, g2, pk["ng2"], pk["wu"], pk["wd"], tm)


def _prompt_layer(x2d, mods, pk, tabs, bias_tiles, b, s, l):
    tm = 512
    f = _front(x2d, mods, pk, tabs, tm)
    r3 = lambda a: a.reshape(b, s, a.shape[-1])
    o_a = _pattn("a", r3(f["qa"]), r3(f["ka"]), r3(f["va"]), ())
    o_b = _pattn("b", r3(f["qb"]), r3(f["kb"]), r3(f["vb"]), (pk["lam"], pk["sg"]), l)
    o_c = _pattn("c", r3(f["qc"]), r3(f["kc"]), r3(f["vc"]), (bias_tiles,))
    frow = _cumsum_rows(_heads_to_rows(r3(f["logf"]), s))
    fcol = jnp.swapaxes(frow[:, :H, :], 1, 2)
    o_d = _pattn("d", r3(f["qd"]), r3(f["kd"]), r3(f["vd"]), (fcol, frow))
    outs = [o.reshape(b * s, HW) for o in (o_a, o_b, o_c, o_d)]
    x2d = _back(x2d, f["h"], outs, mods, pk, tm)
    keep = min(BAND_ROWS, s)
    state = (r3(f["ckv"]), r3(f["kpe"]),
             f["kb"].reshape(b, s, H, 2 * B_DH), f["vb"].reshape(b, s, H, 2 * B_DH),
             f["kc"].reshape(b, s, H, C_DH)[:, s - keep:], f["vc"].reshape(b, s, H, C_DH)[:, s - keep:],
             f["kd"].reshape(b, s, H, D_DH), f["vd"].reshape(b, s, H, D_DH), r3(f["logf"]))
    return x2d, state


def _sample_layer(x2d, mods, pk, tabs, bias_tiles, caches, nb, t, l):
    lat_c, kpe_t, kb_c, vb_c, kc_c, vc_c, kd_c, vd_c, logf_c = caches
    past = lat_c.shape[2]
    tm = nb * t
    f = _front(x2d, mods, pk, tabs, tm)
    o_a = _sattn("a", l, f["qa"], lat_c, kpe_t, f["ka"], f["va"], (pk["wukv"], pk["gkn"]), t, past)
    o_b = _sattn("b", l, f["qb"], kb_c, vb_c, f["kb"], f["vb"], (pk["lam"], pk["sg"]), t, past)
    o_c = _sattn("c", l, f["qc"], kc_c, vc_c, f["kc"], f["vc"], (bias_tiles,), t, past)
    lfc = _heads_to_rows(logf_c[l].astype(F32), past)
    lfn = _heads_to_rows(f["logf"].reshape(nb, t, H), LANES)
    o_d = _sattn("d", l, f["qd"], kd_c, vd_c, f["kd"], f["vd"], (lfc, lfn), t, past)
    x2d = _back(x2d, f["h"], [o_a, o_b, o_c, o_d], mods, pk, tm)
    r3 = lambda a: a.reshape(nb, t, a.shape[-1])
    r4 = lambda a, dh: a.reshape(nb, t, H, dh)
    state = (r3(f["ckv"]), r3(f["kpe"]), r4(f["kb"], 2 * B_DH), r4(f["vb"], 2 * B_DH),
             r4(f["kc"], C_DH), r4(f["vc"], C_DH), r4(f["kd"], D_DH), r4(f["vd"], D_DH), r3(f["logf"]))
    return x2d, state


def kernel(x_prompt, x_sample, c_prompt, c_sample,
           cache_a_latent, cache_a_kpe, cache_b_k, cache_b_v, cache_c_k, cache_c_v,
           cache_d_k, cache_d_v, cache_d_logf,
           norm1_g, norm2_g, w_ada, b_ada, w_in,
           a_q_norm_g, a_kv_norm_g, w_a_uq, w_a_ukv, a_q_gain, a_k_gain,
           b_q_gain, b_k_gain, b_lambda, b_subln_g,
           c_q_gain, c_k_gain, c_rel_bias,
           d_q_gain, d_k_gain, d_forget_b,
           w_branch, w_out, w_up, w_down):
    params = dict(norm1_g=norm1_g, norm2_g=norm2_g, w_in=w_in, a_q_norm_g=a_q_norm_g,
                  a_kv_norm_g=a_kv_norm_g, w_a_uq=w_a_uq, w_a_ukv=w_a_ukv, a_q_gain=a_q_gain,
                  a_k_gain=a_k_gain, b_q_gain=b_q_gain, b_k_gain=b_k_gain, b_lambda=b_lambda,
                  b_subln_g=b_subln_g, c_q_gain=c_q_gain, c_k_gain=c_k_gain, c_rel_bias=c_rel_bias,
                  d_q_gain=d_q_gain, d_k_gain=d_k_gain, d_forget_b=d_forget_b,
                  w_branch=w_branch, w_out=w_out, w_up=w_up, w_down=w_down)
    depth = w_in.shape[0]
    b, s, d = x_prompt.shape
    nb, t, _ = x_sample.shape
    past = cache_a_latent.shape[2]
    rows_c = cache_c_k.shape[2]

    n_c = b + nb
    c_all = jnp.pad(jnp.concatenate([c_prompt, c_sample], axis=0), ((0, (-n_c) % 8), (0, 0)))
    mod_all = _ada(c_all, w_ada, b_ada)

    tabs_p = _rope_tables(jnp.arange(s, dtype=jnp.int32))
    tabs_s = _rope_tables(jnp.tile(past + jnp.arange(t, dtype=jnp.int32), nb))

    tile = 256
    p_bases = [(BAND_ROWS // tile - k) * tile for k in range(BAND_ROWS // tile + 1)]

    x_p = x_prompt.reshape(b * s, d)
    x_s = x_sample.reshape(nb * t, d)
    w_in_t = jnp.transpose(w_in, (2, 0, 1))
    caches = (cache_a_latent, jnp.swapaxes(cache_a_kpe, 2, 3), cache_b_k, cache_b_v, cache_c_k,
              cache_c_v, cache_d_k, cache_d_v, cache_d_logf)
    states_p, states_s = [], []
    for l in range(depth):
        pk = _pack_layer(l, params, w_in_t)
        mod = mod_all[l]
        mods_p = [m.reshape(b, 1, d) for m in jnp.split(mod[:b], 6, axis=-1)]
        mods_s = [jnp.repeat(m, t, axis=0).reshape(1, nb * t, d) for m in jnp.split(mod[b:n_c], 6, axis=-1)]
        bias_p = _relbias(pk["rel"], p_bases, tile, tile, -(CHUNK - 1))
        bias_s = jnp.concatenate(
            [_relbias(pk["rel"], [rows_c], t, rows_c, -REL_CLIP)[:, 0],
             _relbias(pk["rel"], [0], t, LANES, -REL_CLIP)[:, 0]], axis=-1)
        x_p, st_p = _prompt_layer(x_p, mods_p, pk, tabs_p, bias_p, b, s, l)
        x_s, st_s = _sample_layer(x_s, mods_s, pk, tabs_s, bias_s, caches, nb, t, l)
        states_p.append(st_p)
        states_s.append(st_s)
    sp = [jnp.stack(z) for z in zip(*states_p)]
    ss = [jnp.stack(z) for z in zip(*states_s)]
    out = [x_p.reshape(b, s, d), x_s.reshape(nb, t, d)]
    for a, c in zip(sp, ss):
        out += [a, c]
    return tuple(out)
```

```python
import functools
import math

import jax
import jax.numpy as jnp
from jax import lax
from jax.experimental import pallas as pl
from jax.experimental.pallas import tpu as pltpu

BF = jnp.bfloat16
F32 = jnp.float32

CHUNK = 64
EPS = 1e-6
NEG_INF = -1e30
H = 4
A_NOPE, A_ROPE, A_VDIM = 128, 64, 128
A_Q_RANK, A_KV_RANK = 512, 256
A_QK_PAD = 256
ROPE_THETA = 10000.0
B_DH = 64
C_DH = 128
BAND_CHUNKS = 8
BAND_ROWS = BAND_CHUNKS * CHUNK
REL_CLIP = 128
D_DH = 128
HW = 512
N_BRANCH = 4
LANES = 128
PROJ_TILE = 512

VMEM_LIMIT_BYTES = 56 * 1024 * 1024


def _cparams(*sem):
    return pltpu.CompilerParams(dimension_semantics=sem, vmem_limit_bytes=VMEM_LIMIT_BYTES)


def _nt_dot(a, b):
    return lax.dot_general(a, b, (((1,), (1,)), ((), ())), preferred_element_type=F32)


def _dot(a, b):
    return jnp.dot(a, b, preferred_element_type=F32)


def _rms(z, n):
    ms = jnp.sum(z * z, axis=-1, keepdims=True) * (1.0 / n)
    return z * lax.rsqrt(ms + EPS)


def _rms_groups(z, gs):
    w = z.shape[-1]
    if gs >= LANES:
        parts = [_rms(z[:, g * gs:(g + 1) * gs], gs) for g in range(w // gs)]
        return parts[0] if len(parts) == 1 else jnp.concatenate(parts, axis=-1)
    assert gs * 2 == LANES
    lo = lax.broadcasted_iota(jnp.int32, (1, LANES), 1) < gs
    parts = []
    for g in range(w // LANES):
        zz = z[:, g * LANES:(g + 1) * LANES]
        sq = zz * zz
        s_lo = jnp.sum(jnp.where(lo, sq, 0.0), axis=-1, keepdims=True)
        s_hi = jnp.sum(jnp.where(lo, 0.0, sq), axis=-1, keepdims=True)
        ms = jnp.where(lo, s_lo, s_hi) * (1.0 / gs)
        parts.append(zz * lax.rsqrt(ms + EPS))
    return jnp.concatenate(parts, axis=-1)


def _rope128(r, cos, sin):
    half = A_ROPE // 2
    lane = lax.broadcasted_iota(jnp.int32, (1, LANES), 1)
    swapped = jnp.where(lane < half, pltpu.roll(r, LANES - half, 1), pltpu.roll(r, half, 1))
    return r * cos + swapped * sin


CHUNK_SHIFT = CHUNK.bit_length() - 1
assert 1 << CHUNK_SHIFT == CHUNK


def _chunk(pos):
    return jnp.right_shift(pos, CHUNK_SHIFT)


def _log_sigmoid(x):
    return jnp.minimum(x, 0.0) - jnp.log1p(jnp.exp(-jnp.abs(x)))


def _sigmoid(x):
    return 1.0 / (1.0 + jnp.exp(-x))


def _ada_kernel(c_ref, w_ref, b_ref, o_ref):
    c = c_ref[...]
    a = (c * _sigmoid(c)).astype(BF)
    o_ref[...] = _dot(a, w_ref[...].astype(BF)) + b_ref[...]


def _ada(c_all, w_ada, b_ada):
    depth, d, n = w_ada.shape
    r = c_all.shape[0]
    tn = 1024
    return pl.pallas_call(
        _ada_kernel,
        out_shape=jax.ShapeDtypeStruct((depth, r, n), F32),
        grid=(depth, n // tn),
        in_specs=[
            pl.BlockSpec((r, d), lambda l, j: (0, 0)),
            pl.BlockSpec((None, d, tn), lambda l, j: (l, 0, j)),
            pl.BlockSpec((None, 1, tn), lambda l, j: (l, 0, j)),
        ],
        out_specs=pl.BlockSpec((None, r, tn), lambda l, j: (l, 0, j)),
        compiler_params=_cparams("arbitrary", "arbitrary"),
        name="ada_mod",
    )(c_all, w_ada, b_ada.reshape(depth, 1, n))


WIN_TILE = PROJ_TILE
WIN_PROJ_STARTS = (0, A_Q_RANK, 832, 1344, 1856, 2368, 2880, 3392, 3904, 4416, 4928, 5440)
WIN_GATE_START = 5444
PROJ_TILES = len(WIN_PROJ_STARTS)


def _winprep_kernel(off_ref, w_hbm, o_ref, buf, sem, *, layer, nsteps):
    s = pl.program_id(0)
    slot = s % 2

    def copy(step, slot_):
        return pltpu.make_async_copy(w_hbm.at[pl.ds(off_ref[step], WIN_TILE), layer, :],
                                     buf.at[slot_], sem.at[slot_])

    @pl.when(s == 0)
    def _():
        copy(0, 0).start()

    @pl.when(s + 1 < nsteps)
    def _():
        copy(s + 1, 1 - slot).start()

    copy(s, slot).wait()
    o_ref[...] = buf[slot].T.astype(BF)


def _win_prep(w_in_t, layer, d):
    gate_tiles = N_BRANCH * d // WIN_TILE
    starts = WIN_PROJ_STARTS + tuple(WIN_GATE_START + g * WIN_TILE for g in range(gate_tiles))
    assert starts[-1] + WIN_TILE == w_in_t.shape[0]
    nsteps = len(starts)
    rows = w_in_t.shape[2]
    return pl.pallas_call(
        functools.partial(_winprep_kernel, layer=layer, nsteps=nsteps),
        out_shape=jax.ShapeDtypeStruct((rows, nsteps * WIN_TILE), BF),
        grid_spec=pltpu.PrefetchScalarGridSpec(
            num_scalar_prefetch=1,
            grid=(nsteps,),
            in_specs=[pl.BlockSpec(memory_space=pl.ANY)],
            out_specs=pl.BlockSpec((rows, WIN_TILE), lambda s, off: (0, s)),
            scratch_shapes=[pltpu.VMEM((2, WIN_TILE, rows), F32), pltpu.SemaphoreType.DMA((2,))],
        ),
        compiler_params=_cparams("arbitrary"),
        name="w_in_repack",
    )(jnp.asarray(starts, jnp.int32), w_in_t)


def _proj_kernel(x_ref, sc_ref, sh_ref, ng_ref, w_ref, gq_ref, gkv_ref, gkpe_ref, gb_ref, gc_ref,
                 gd_ref, fb_ref, cos_ref, sin_ref,
                 h_out, cq_out, ckv_out, kpe_out, kpep_out, logf_out,
                 qb_out, kb_out, vb_out, qc_out, kc_out, vc_out, qd_out, kd_out, vd_out,
                 h_scr):
    j = pl.program_id(1)

    @pl.when(j == 0)
    def _():
        x = x_ref[...]
        y = _rms(x, x.shape[-1]) * ng_ref[...]
        hb = (y * (1.0 + sc_ref[...]) + sh_ref[...]).astype(BF)
        h_scr[...] = hb
        h_out[...] = hb

    z = _dot(h_scr[...], w_ref[...])

    @pl.when(j == 0)
    def _():
        cq_out[...] = (_rms(z, A_Q_RANK) * gq_ref[...]).astype(BF)

    @pl.when(j == 1)
    def _():
        ckv_out[...] = _rms(z[:, :A_KV_RANK], A_KV_RANK) * gkv_ref[...]
        lane = lax.broadcasted_iota(jnp.int32, (1, LANES), 1)
        kp = jnp.where(lane < A_ROPE, z[:, A_KV_RANK:A_KV_RANK + LANES], 0.0)
        kp = _rope128(_rms(kp, A_ROPE) * gkpe_ref[...], cos_ref[...], sin_ref[...])
        kpe_out[...] = kp[:, :A_ROPE]
        kpep_out[...] = kp.astype(BF)

    @pl.when(j == PROJ_TILES - 1)
    def _():
        logf_out[...] = _log_sigmoid(z[:, :LANES] + fb_ref[...])[:, :H]

    def normed(gs, g_ref):
        return _rms_groups(z, gs) * g_ref[...]

    @pl.when(j == 2)
    def _():
        qb_out[...] = normed(B_DH, gb_ref.at[0:1]).astype(BF)

    @pl.when(j == 3)
    def _():
        kb_out[...] = normed(B_DH, gb_ref.at[1:2])

    @pl.when(j == 4)
    def _():
        vb_out[...] = z

    @pl.when(j == 5)
    def _():
        qc_out[...] = normed(C_DH, gc_ref.at[0:1]).astype(BF)

    @pl.when(j == 6)
    def _():
        kc_out[...] = normed(C_DH, gc_ref.at[1:2])

    @pl.when(j == 7)
    def _():
        vc_out[...] = z

    @pl.when(j == 8)
    def _():
        qd_out[...] = normed(D_DH, gd_ref.at[0:1]).astype(BF)

    @pl.when(j == 9)
    def _():
        kd_out[...] = normed(D_DH, gd_ref.at[1:2])

    @pl.when(j == 10)
    def _():
        vd_out[...] = z


def _proj(x2d, sc, sh, ng, wp, gq, gkv, gkpe, gb, gc, gd, fb, cos_t, sin_t, tm):
    m, d = x2d.shape
    groups, rows = sc.shape[0], sc.shape[1]
    tiles_per_group = (m // groups) // tm
    tab_blocks = cos_t.shape[0] // tm
    row = lambda i, j: (i, 0)
    const2 = lambda i, j: (0, 0)
    mod_spec = pl.BlockSpec((None, rows, d), lambda i, j: (i // tiles_per_group, 0, 0))
    tab_spec = pl.BlockSpec((tm, LANES), lambda i, j: (i % tab_blocks, 0))

    def out(width, dtype):
        return jax.ShapeDtypeStruct((m, width), dtype), pl.BlockSpec((tm, width), row)

    outs = [out(d, BF), out(A_Q_RANK, BF), out(A_KV_RANK, F32), out(A_ROPE, F32), out(LANES, BF),
            out(H, F32),
            out(HW, BF), out(HW, F32), out(HW, F32),
            out(HW, BF), out(HW, F32), out(HW, F32),
            out(HW, BF), out(HW, F32), out(HW, F32)]
    return pl.pallas_call(
        _proj_kernel,
        out_shape=[o[0] for o in outs],
        grid=(m // tm, PROJ_TILES),
        in_specs=[
            pl.BlockSpec((tm, d), row),
            mod_spec, mod_spec,
            pl.BlockSpec((1, d), const2),
            pl.BlockSpec((d, PROJ_TILE), lambda i, j: (0, j)),
            pl.BlockSpec((1, A_Q_RANK), const2),
            pl.BlockSpec((1, A_KV_RANK), const2),
            pl.BlockSpec((1, LANES), const2),
            pl.BlockSpec((2, HW), const2),
            pl.BlockSpec((2, HW), const2),
            pl.BlockSpec((2, HW), const2),
            pl.BlockSpec((1, LANES), const2),
            tab_spec, tab_spec,
        ],
        out_specs=[o[1] for o in outs],
        scratch_shapes=[pltpu.VMEM((tm, d), BF)],
        compiler_params=_cparams("arbitrary", "arbitrary"),
        name="in_proj",
    )(x2d, sc, sh, ng, wp, gq, gkv, gkpe, gb, gc, gd, fb, cos_t, sin_t)


def _aprep_kernel(cq_ref, ckv_ref, kpep_ref, wuq_ref, wukv_ref, gqn_ref, gqr_ref, gkn_ref,
                  cos_ref, sin_ref, q_out, k_out, v_out):
    zq = _dot(cq_ref[...], wuq_ref[...])
    cos, sin = cos_ref[...], sin_ref[...]
    parts = []
    for h in range(H):
        nope = zq[:, h * A_QK_PAD:h * A_QK_PAD + A_NOPE]
        rp = zq[:, h * A_QK_PAD + A_NOPE:(h + 1) * A_QK_PAD]
        parts.append(_rms(nope, A_NOPE) * gqn_ref[...])
        parts.append(_rope128(_rms(rp, A_ROPE) * gqr_ref[...], cos, sin))
    q_out[...] = jnp.concatenate(parts, axis=-1).astype(BF)

    zkv = _dot(ckv_ref[...].astype(BF), wukv_ref[...])
    kp = kpep_ref[...]
    parts = []
    for h in range(H):
        kn = _rms(zkv[:, h * A_NOPE:(h + 1) * A_NOPE], A_NOPE) * gkn_ref[...]
        parts.append(kn.astype(BF))
        parts.append(kp)
    k_out[...] = jnp.concatenate(parts, axis=-1)
    v_out[...] = zkv[:, H * A_NOPE:].astype(BF)


def _aprep(cqn, ckvn, kpep, wuq, wukv, gqn, gqr, gkn, cos_t, sin_t, tm):
    m = cqn.shape[0]
    tab_blocks = cos_t.shape[0] // tm
    row = lambda i: (i, 0)
    const = lambda i: (0, 0)
    tab_spec = pl.BlockSpec((tm, LANES), lambda i: (i % tab_blocks, 0))
    return pl.pallas_call(
        _aprep_kernel,
        out_shape=[jax.ShapeDtypeStruct((m, H * A_QK_PAD), BF),
                   jax.ShapeDtypeStruct((m, H * A_QK_PAD), BF),
                   jax.ShapeDtypeStruct((m, HW), BF)],
        grid=(m // tm,),
        in_specs=[
            pl.BlockSpec((tm, A_Q_RANK), row),
            pl.BlockSpec((tm, A_KV_RANK), row),
            pl.BlockSpec((tm, LANES), row),
            pl.BlockSpec(wuq.shape, const),
            pl.BlockSpec(wukv.shape, const),
            pl.BlockSpec((1, LANES), const),
            pl.BlockSpec((1, LANES), const),
            pl.BlockSpec((1, LANES), const),
            tab_spec, tab_spec,
        ],
        out_specs=[pl.BlockSpec((tm, H * A_QK_PAD), row),
                   pl.BlockSpec((tm, H * A_QK_PAD), row),
                   pl.BlockSpec((tm, HW), row)],
        compiler_params=_cparams("arbitrary"),
        name="a_prep",
    )(cqn, ckvn, kpep, wuq, wukv, gqn, gqr, gkn, cos_t, sin_t)


CUM_BLOCK = 256


def _tri_upper(n):
    r = lax.broadcasted_iota(jnp.int32, (n, n), 0)
    c = lax.broadcasted_iota(jnp.int32, (n, n), 1)
    return jnp.where(r <= c, 1.0, 0.0).astype(BF)


def _dot3(x, u):
    hi = x.astype(BF)
    r1 = x - hi.astype(F32)
    mid = r1.astype(BF)
    lo = (r1 - mid.astype(F32)).astype(BF)
    return _dot(hi, u) + _dot(mid, u) + _dot(lo, u)


def _cumsum_lanes(src_ref, dst_ref):
    u = _tri_upper(CUM_BLOCK)
    carry = jnp.zeros((8, 1), F32)
    for b in range(src_ref.shape[-1] // CUM_BLOCK):
        blk = slice(b * CUM_BLOCK, (b + 1) * CUM_BLOCK)
        c = _dot3(src_ref[:, blk], u) + carry
        dst_ref[:, blk] = c
        carry = c[:, CUM_BLOCK - 1:CUM_BLOCK]
    return carry


def _cumsum_kernel(x_ref, o_ref):
    _cumsum_lanes(x_ref, o_ref)


def _cumsum_rows(x):
    b, r, s = x.shape
    return pl.pallas_call(
        _cumsum_kernel,
        out_shape=jax.ShapeDtypeStruct(x.shape, F32),
        grid=(b,),
        in_specs=[pl.BlockSpec((None, r, s), lambda i: (i, 0, 0))],
        out_specs=pl.BlockSpec((None, r, s), lambda i: (i, 0, 0)),
        compiler_params=_cparams("arbitrary"),
        name="forget_cumsum",
    )(x)


def _relbias_kernel(tab_ref, o_ref, *, bases, rows, cols, dmin, mult):
    h = pl.program_id(0)
    r = lax.broadcasted_iota(jnp.int32, (rows, cols), 0)
    c = lax.broadcasted_iota(jnp.int32, (rows, cols), 1)
    for t, base in enumerate(bases):
        idx = jnp.clip(base + r - c, -REL_CLIP, REL_CLIP) + REL_CLIP
        lo = min(max(base - (cols - 1), max(dmin, -REL_CLIP)), REL_CLIP) + REL_CLIP
        hi = min(max(base + rows - 1, -REL_CLIP), REL_CLIP) + REL_CLIP

        def body(e, acc):
            return jnp.where(idx == e, tab_ref[h, e] * mult, acc)

        o_ref[t] = lax.fori_loop(lo, hi + 1, body, jnp.zeros((rows, cols), F32))


def _relbias(table, bases, rows, cols, dmin, mult=1.0):
    nh = table.shape[0]
    return pl.pallas_call(
        functools.partial(_relbias_kernel, bases=tuple(bases), rows=rows, cols=cols, dmin=dmin, mult=mult),
        out_shape=jax.ShapeDtypeStruct((nh, len(bases), rows, cols), F32),
        grid=(nh,),
        in_specs=[pl.BlockSpec(memory_space=pltpu.SMEM)],
        out_specs=pl.BlockSpec((None, len(bases), rows, cols), lambda h: (h, 0, 0, 0)),
        compiler_params=_cparams("arbitrary"),
        name="rel_bias_tiles",
    )(table)


def _diff_lambda(lam_ref, layer_idx):
    lam_init = 0.8 - 0.6 * math.exp(-0.3 * layer_idx)
    lp = lam_ref[...]
    a = jnp.sum(lp[0:1] * lp[1:2], keepdims=True)
    b = jnp.sum(lp[2:3] * lp[3:4], keepdims=True)
    return jnp.exp(a) - jnp.exp(b) + lam_init, lam_init


ALIBI_SLOPES = tuple(2.0 ** (-8.0 * (h + 1) / H) for h in range(H))


LOG2E = math.log2(math.e)
RG = 32


def _pattn_kernel(*refs, mode, tq, tk, nk, layer_idx, back):
    if mode == "a":
        q_ref, k_ref, v_ref, o_ref, m_ref, acc_ref, s_scr, p_scr, pm_scr = refs
        dq, scale = A_QK_PAD, (A_NOPE + A_ROPE) ** -0.5
    elif mode == "b":
        q_ref, k_ref, v_ref, lam_ref, sg_ref, o_ref, m_ref, acc_ref, s_scr, p_scr, pm_scr = refs
        dq, scale = 2 * B_DH, B_DH ** -0.5
    elif mode == "c":
        q_ref, k_ref, v_ref, bias_ref, o_ref, m_ref, acc_ref, s_scr, p_scr, pm_scr = refs
        dq, scale = C_DH, C_DH ** -0.5
    else:
        q_ref, k_ref, v_ref, fq_ref, fk_ref, o_ref, m_ref, acc_ref, s_scr, p_scr, pm_scr, fq_scr = refs
        dq, scale = D_DH, D_DH ** -0.5
    assert tq == tk and CHUNK % RG == 0 and tk % LANES == 0
    dv = HW // H
    c1 = scale * LOG2E
    nchunk = tk // LANES
    i = pl.program_id(1)
    j = pl.program_id(2)
    if mode == "c":
        kb = i - back + j
        valid = kb >= 0
        first = jnp.maximum(back - i, 0)
        last = nk - 1
    else:
        last = i
        kb = j
        valid = j <= last
        first = 0

    @pl.when(j == first)
    def _():
        m_ref[...] = jnp.full(m_ref.shape, -jnp.inf, F32)
        acc_ref[...] = jnp.zeros(acc_ref.shape, F32)
        if mode == "d":
            for h in range(H):
                fq_scr[h] = jnp.broadcast_to(fq_ref[:, h:h + 1] * LOG2E, (tq, LANES))

    lane = lax.broadcasted_iota(jnp.int32, (RG, LANES), 1)
    lane_minus_row = lane - lax.broadcasted_iota(jnp.int32, (RG, LANES), 0)

    def chunk_visibility(kind, r0, c):
        c0 = c * LANES
        if kind == "none":
            return "all"
        if kind == "lt":
            bound = (r0 // CHUNK + 1) * CHUNK
            if c0 >= bound:
                return "none"
            return "all" if c0 + LANES <= bound else lane < bound - c0
        if kind == "ge":
            bound = r0 // CHUNK * CHUNK
            if c0 + LANES <= bound:
                return "none"
            return "all" if c0 >= bound else lane >= bound - c0
        assert kind == "causal"
        if c0 > r0 + RG - 1:
            return "none"
        return "all" if c0 + LANES - 1 <= r0 else lane_minus_row <= r0 - c0

    def sweep1(idxs, h, g, kind, delta):
        r0 = g * RG
        rows = slice(r0, r0 + RG)
        pm = [None] * len(idxs)
        for c in range(nchunk):
            vis = chunk_visibility(kind, r0, c)
            if isinstance(vis, str) and vis == "none":
                continue
            cols = slice(c * LANES, (c + 1) * LANES)
            if mode == "b":
                dist = jnp.abs(lane_minus_row.astype(F32) - (delta + float(r0 - c * LANES)))
                bias = dist * (-ALIBI_SLOPES[h] * LOG2E)
            elif mode == "c":
                bias = bias_ref[h, rows, cols]
            elif mode == "d":
                bias = fq_scr[h, rows, :] - fk_ref[h:h + 1, cols] * LOG2E
            for n, idx in enumerate(idxs):
                t = s_scr[idx, rows, cols] * c1
                if mode != "a":
                    t = t + bias
                if not isinstance(vis, str):
                    t = jnp.where(vis, t, NEG_INF)
                s_scr[idx, rows, cols] = t
                pm[n] = t if pm[n] is None else jnp.maximum(pm[n], t)
        for n, idx in enumerate(idxs):
            pm_scr[idx, rows, :] = pm[n]

    def sweep2(idx, g, kind):
        r0 = g * RG
        rows = slice(r0, r0 + RG)
        m_rows = m_ref[idx, rows, :]
        for c in range(nchunk):
            cols = slice(c * LANES, (c + 1) * LANES)
            vis = chunk_visibility(kind, r0, c)
            if isinstance(vis, str) and vis == "none":
                p_scr[idx, rows, cols] = jnp.zeros((RG, LANES), BF)
            else:
                p_scr[idx, rows, cols] = jnp.exp2(s_scr[idx, rows, cols] - m_rows).astype(BF)

    def block(kind):
        delta = ((i - kb) * tq).astype(F32) if mode == "b" else None
        ones = jnp.ones((tk, LANES), BF)
        for h in range(H):
            q = q_ref[:, h * dq:(h + 1) * dq]
            k = k_ref[:, h * dq:(h + 1) * dq].astype(BF)
            v1 = jnp.concatenate([v_ref[:, h * dv:(h + 1) * dv].astype(BF), ones], axis=-1)
            if mode == "b":
                lo = lax.broadcasted_iota(jnp.int32, (1, dq), 1) < B_DH
                zero = jnp.zeros_like(q)
                s_scr[h] = _nt_dot(jnp.where(lo, q, zero), k)
                s_scr[H + h] = _nt_dot(jnp.where(lo, zero, q), k)
                idxs = (h, H + h)
            else:
                s_scr[h] = _nt_dot(q, k)
                idxs = (h,)
            for g in range(tq // RG):
                sweep1(idxs, h, g, kind, delta)
            for idx in idxs:
                m_prev = m_ref[idx]
                m_new = jnp.maximum(m_prev, jnp.max(pm_scr[idx], axis=-1, keepdims=True))
                m_ref[idx] = m_new
                alpha = jnp.exp2(m_prev - m_new)
                for g in range(tq // RG):
                    sweep2(idx, g, kind)
                acc_ref[idx] = jnp.concatenate([alpha, alpha], axis=-1) * acc_ref[idx] + _dot(p_scr[idx], v1)

    if mode == "c":
        assert back == 2 and BAND_ROWS == back * tk
        for jj, kind in enumerate(("ge", "none", "lt")):
            pl.when((j == jj) & valid)(functools.partial(block, kind))
    else:
        pl.when(j < last)(functools.partial(block, "none"))
        pl.when(j == last)(functools.partial(block, "causal" if mode == "d" else "lt"))

    @pl.when(j == last)
    def _():
        for h in range(H):
            o = acc_ref[h, :, :dv] / acc_ref[h, :, dv:]
            if mode == "b":
                lam, lam_init = _diff_lambda(lam_ref, layer_idx)
                o = o - lam * (acc_ref[H + h, :, :dv] / acc_ref[H + h, :, dv:])
                o = (_rms(o, dv) * sg_ref[...]) * (1.0 - lam_init)
            o_ref[:, h * dv:(h + 1) * dv] = o.astype(BF)


def _pattn(mode, q, k, v, extras, layer_idx=0):
    b, s, qw = q.shape
    kw = k.shape[-1]
    back = 0
    if mode == "c":
        tq = tk = 256
        back = BAND_ROWS // tk
        nk = back + 1
        kmap = lambda bi, i, j: (bi, jnp.maximum(i - back + j, 0), 0)
    else:
        tq = tk = 512
        nk = s // tk
        kmap = lambda bi, i, j: (bi, jnp.minimum(j, ((i + 1) * tq - 1) // tk), 0)
    qmap = lambda bi, i, j: (bi, i, 0)
    in_specs = [pl.BlockSpec((None, tq, qw), qmap),
                pl.BlockSpec((None, tk, kw), kmap),
                pl.BlockSpec((None, tk, HW), kmap)]
    if mode == "b":
        lam, sg = extras
        in_specs += [pl.BlockSpec(lam.shape, lambda bi, i, j: (0, 0)),
                     pl.BlockSpec(sg.shape, lambda bi, i, j: (0, 0))]
    elif mode == "c":
        (bias,) = extras
        in_specs += [pl.BlockSpec((H, None, tq, tk), lambda bi, i, j: (0, j, 0, 0))]
    elif mode == "d":
        fcol, frow = extras
        in_specs += [pl.BlockSpec((None, tq, H), qmap),
                     pl.BlockSpec((None, 8, tk), lambda bi, i, j: (bi, 0, kmap(bi, i, j)[1]))]
    nstate = 2 * H if mode == "b" else H
    return pl.pallas_call(
        functools.partial(_pattn_kernel, mode=mode, tq=tq, tk=tk, nk=nk, layer_idx=layer_idx, back=back),
        out_shape=jax.ShapeDtypeStruct((b, s, HW), BF),
        grid=(b, s // tq, nk),
        in_specs=in_specs,
        out_specs=pl.BlockSpec((None, tq, HW), qmap),
        scratch_shapes=[pltpu.VMEM((nstate, tq, LANES), F32),
                        pltpu.VMEM((nstate, tq, 2 * (HW // H)), F32),
                        pltpu.VMEM((nstate, tq, tk), F32),
                        pltpu.VMEM((nstate, tq, tk), BF),
                        pltpu.VMEM((nstate, tq, LANES), F32)]
                       + ([pltpu.VMEM((H, tq, LANES), F32)] if mode == "d" else []),
        compiler_params=_cparams("arbitrary", "arbitrary", "arbitrary"),
        name="prompt_attn_" + mode,
    )(q, k, v, *extras)


def _pad_rows(x, rows):
    return jnp.concatenate([x, jnp.zeros((rows - x.shape[0], x.shape[1]), x.dtype)], axis=0)


def _prefetch_head_caches(k_hbm, v_hbm, kbuf, vbuf, sem, layer, nb):
    b = pl.program_id(0)
    slot = b % 2
    dh = kbuf.shape[-1] // H

    def copies(bidx, slot_):
        out = []
        for h in range(H):
            cols = pl.ds(h * dh, dh)
            out.append(pltpu.make_async_copy(k_hbm.at[layer, bidx, :, h, :], kbuf.at[slot_, :, cols],
                                             sem.at[slot_, 0]))
            out.append(pltpu.make_async_copy(v_hbm.at[layer, bidx, :, h, :], vbuf.at[slot_, :, cols],
                                             sem.at[slot_, 1]))
        return out

    @pl.when(b == 0)
    def _():
        for c in copies(0, 0):
            c.start()

    @pl.when(b + 1 < nb)
    def _():
        for c in copies(b + 1, 1 - slot):
            c.start()

    for c in copies(b, slot):
        c.wait()
    return slot


def _sattn_kernel(*refs, mode, t, past, layer, nb):
    if mode == "a":
        (q_ref, lat_ref, kpet_ref, kn_ref, vn_ref, wukv_ref, gkn_ref, o_ref) = refs
        dq, scale = A_QK_PAD, (A_NOPE + A_ROPE) ** -0.5
        rows_c = lat_ref.shape[0]
    else:
        if mode == "b":
            (q_ref, k_hbm, v_hbm, kn_ref, vn_ref, lam_ref, sg_ref, o_ref, kbuf, vbuf, sem) = refs
            dq, scale = 2 * B_DH, B_DH ** -0.5
        elif mode == "c":
            (q_ref, k_hbm, v_hbm, kn_ref, vn_ref, bias_ref, o_ref, kbuf, vbuf, sem) = refs
            dq, scale = C_DH, C_DH ** -0.5
        else:
            (q_ref, k_hbm, v_hbm, kn_ref, vn_ref, lfc_ref, lfn_ref, o_ref, f_scr, kbuf, vbuf, sem) = refs
            dq, scale = D_DH, D_DH ** -0.5
        rows_c = kbuf.shape[1]
        slot = _prefetch_head_caches(k_hbm, v_hbm, kbuf, vbuf, sem, layer, nb)
        kc_ref, vc_ref = kbuf.at[slot], vbuf.at[slot]
    dv = HW // H
    npad = LANES
    qpos_c = past + lax.broadcasted_iota(jnp.int32, (t, rows_c), 0)
    kpos_c = (past - rows_c) + lax.broadcasted_iota(jnp.int32, (t, rows_c), 1)
    qpos_n = past + lax.broadcasted_iota(jnp.int32, (t, npad), 0)
    col_n = lax.broadcasted_iota(jnp.int32, (t, npad), 1)
    kpos_n = past + col_n
    real_n = col_n < t
    if mode in ("a", "b"):
        mask_c = _chunk(kpos_c) <= _chunk(qpos_c)
        mask_n = real_n & (_chunk(kpos_n) <= _chunk(qpos_n))
    elif mode == "c":
        qc_c, kc_c = _chunk(qpos_c), _chunk(kpos_c)
        qc_n, kc_n = _chunk(qpos_n), _chunk(kpos_n)
        mask_c = (kpos_c >= 0) & (kc_c <= qc_c) & (kc_c >= qc_c - BAND_CHUNKS)
        mask_n = real_n & (kc_n <= qc_n) & (kc_n >= qc_n - BAND_CHUNKS)
    else:
        mask_c = kpos_c <= qpos_c
        mask_n = real_n & (kpos_n <= qpos_n)

    if mode == "a":
        lat = lat_ref[...].astype(BF)
        kpe_t = kpet_ref[...]
        kpe_t = jnp.concatenate([kpe_t, jnp.zeros_like(kpe_t)], axis=0).astype(BF)
    if mode == "b":
        dist_c = jnp.abs(qpos_c - kpos_c).astype(F32)
        dist_n = jnp.abs(qpos_n - kpos_n).astype(F32)
        lo = lax.broadcasted_iota(jnp.int32, (1, dq), 1) < B_DH
        lam, lam_init = _diff_lambda(lam_ref, layer)
    if mode == "d":
        carry = _cumsum_lanes(lfc_ref, f_scr)
        f_new = _dot3(lfn_ref[...], _tri_upper(LANES)) + carry
        eye = (lax.broadcasted_iota(jnp.int32, (t, npad), 0) == col_n)

    def softmax_pv(s_c, s_n, v_c, v_n):
        m = jnp.maximum(jnp.max(s_c, axis=-1, keepdims=True), jnp.max(s_n, axis=-1, keepdims=True))
        p_c = jnp.exp(s_c - m)
        p_n = jnp.exp(s_n - m)
        l = jnp.sum(p_c, axis=-1, keepdims=True) + jnp.sum(p_n, axis=-1, keepdims=True)
        return (_dot(p_c.astype(BF), v_c) + _dot(p_n.astype(BF), v_n)) / l

    for h in range(H):
        q = q_ref[:, h * dq:(h + 1) * dq]
        if mode == "a":
            kn = _pad_rows(kn_ref[:, h * dq:(h + 1) * dq], npad)
            vn = _pad_rows(vn_ref[:, h * dv:(h + 1) * dv], npad)
            w = wukv_ref[...]
            k_nope = _dot(lat, w[:, h * A_NOPE:(h + 1) * A_NOPE])
            k_nope = (_rms(k_nope, A_NOPE) * gkn_ref[...]).astype(BF)
            v_c = _dot(lat, w[:, H * A_NOPE + h * A_VDIM:H * A_NOPE + (h + 1) * A_VDIM]).astype(BF)
            s_c = (_nt_dot(q[:, :A_NOPE], k_nope) + _dot(q[:, A_NOPE:], kpe_t)) * scale
            s_n = _nt_dot(q, kn) * scale
        else:
            k_c = kc_ref[:, h * dq:(h + 1) * dq].astype(BF)
            v_c = vc_ref[:, h * dv:(h + 1) * dv].astype(BF)
            kn = _pad_rows(kn_ref[:, h * dq:(h + 1) * dq], npad).astype(BF)
            vn = _pad_rows(vn_ref[:, h * dv:(h + 1) * dv], npad).astype(BF)
        if mode == "b":
            zero = jnp.zeros_like(q)
            q1, q2 = jnp.where(lo, q, zero), jnp.where(lo, zero, q)
            bias_c = -ALIBI_SLOPES[h] * dist_c
            bias_n = -ALIBI_SLOPES[h] * dist_n
            outs = []
            for qq in (q1, q2):
                s_c = jnp.where(mask_c, _nt_dot(qq, k_c) * scale + bias_c, NEG_INF)
                s_n = jnp.where(mask_n, _nt_dot(qq, kn) * scale + bias_n, NEG_INF)
                outs.append(softmax_pv(s_c, s_n, v_c, vn))
            o = outs[0] - lam * outs[1]
            o = (_rms(o, dv) * sg_ref[...]) * (1.0 - lam_init)
        else:
            if mode != "a":
                s_c = _nt_dot(q, k_c) * scale
                s_n = _nt_dot(q, kn) * scale
            if mode == "c":
                s_c = s_c + bias_ref[h, :, :rows_c]
                s_n = s_n + bias_ref[h, :, rows_c:]
            elif mode == "d":
                fq = jnp.sum(jnp.where(eye, f_new[h:h + 1, :], 0.0), axis=-1, keepdims=True)
                s_c = s_c + (fq - f_scr[h:h + 1, :])
                s_n = s_n + (fq - f_new[h:h + 1, :])
            s_c = jnp.where(mask_c, s_c, NEG_INF)
            s_n = jnp.where(mask_n, s_n, NEG_INF)
            o = softmax_pv(s_c, s_n, v_c, vn)
        o_ref[:, h * dv:(h + 1) * dv] = o.astype(BF)


def _sattn(mode, layer, q, cache_k, cache_v, kn, vn, extras, t, past):
    nb = cache_k.shape[1]
    qw = q.shape[-1]
    row = lambda b: (b, 0)
    if mode == "a":
        cache_specs = [pl.BlockSpec((None, None) + cache_k.shape[2:], lambda b: (layer, b, 0, 0)),
                       pl.BlockSpec((None, None) + cache_v.shape[2:], lambda b: (layer, b, 0, 0))]
        dma_scratch = []
    else:
        rows_c = cache_k.shape[2]
        cache_specs = [pl.BlockSpec(memory_space=pl.ANY), pl.BlockSpec(memory_space=pl.ANY)]
        dma_scratch = [pltpu.VMEM((2, rows_c, HW), F32), pltpu.VMEM((2, rows_c, HW), F32),
                       pltpu.SemaphoreType.DMA((2, 2))]
    in_specs = [pl.BlockSpec((t, qw), row)] + cache_specs + [
                pl.BlockSpec((t, kn.shape[-1]), row),
                pl.BlockSpec((t, vn.shape[-1]), row)]
    scratch = []
    if mode == "d":
        lfc, lfn = extras
        in_specs += [pl.BlockSpec((None,) + lfc.shape[1:], lambda b: (b, 0, 0)),
                     pl.BlockSpec((None,) + lfn.shape[1:], lambda b: (b, 0, 0))]
        scratch = [pltpu.VMEM(lfc.shape[1:], F32)]
    elif mode == "c":
        (bias,) = extras
        in_specs += [pl.BlockSpec(bias.shape, lambda b: (0, 0, 0))]
    else:
        in_specs += [pl.BlockSpec(e.shape, lambda b: (0, 0)) for e in extras]
    return pl.pallas_call(
        functools.partial(_sattn_kernel, mode=mode, t=t, past=past, layer=layer, nb=nb),
        out_shape=jax.ShapeDtypeStruct((nb * t, HW), BF),
        grid=(nb,),
        in_specs=in_specs,
        out_specs=pl.BlockSpec((t, HW), row),
        scratch_shapes=scratch + dma_scratch,
        compiler_params=_cparams("arbitrary"),
        name="sample_attn_" + mode,
    )(q, cache_k, cache_v, kn, vn, *extras)


def _gmerge_kernel(h_ref, oa_ref, ob_ref, oc_ref, od_ref, wg0_ref, wg1_ref, wg2_ref, wg3_ref,
                   wb_ref, out_ref):
    hb = h_ref[...]
    acc = None
    for g, (o_ref, wg_ref) in enumerate(((oa_ref, wg0_ref), (ob_ref, wg1_ref),
                                         (oc_ref, wg2_ref), (od_ref, wg3_ref))):
        term = _sigmoid(_dot(hb, wg_ref[...])) * _dot(o_ref[...], wb_ref[g])
        acc = term if acc is None else acc + term
    out_ref[...] = acc.astype(BF)


def _gmerge(hb, outs, wg, wb, tm):
    m, d = hb.shape
    tn = WIN_TILE
    nt = d // tn
    row = lambda i, n: (i, 0)
    wg_specs = [pl.BlockSpec((d, tn), functools.partial(lambda i, n, g: (0, PROJ_TILES + g * nt + n), g=g))
                for g in range(N_BRANCH)]
    return pl.pallas_call(
        _gmerge_kernel,
        out_shape=jax.ShapeDtypeStruct((m, d), BF),
        grid=(m // tm, nt),
        in_specs=[pl.BlockSpec((tm, d), row)] + [pl.BlockSpec((tm, HW), row)] * N_BRANCH + wg_specs
                 + [pl.BlockSpec((N_BRANCH, HW, tn), lambda i, n: (0, 0, n))],
        out_specs=pl.BlockSpec((tm, tn), lambda i, n: (i, n)),
        compiler_params=_cparams("arbitrary", "arbitrary"),
        name="gate_merge",
    )(hb, *outs, wg, wg, wg, wg, wb)


def _oproj_kernel(mix_ref, x_ref, g_ref, w_ref, o_ref):
    o_ref[...] = x_ref[...] + g_ref[...] * _dot(mix_ref[...], w_ref[...])


def _oproj(mix, x2d, gate, w, tm):
    m, d = x2d.shape
    rows = gate.shape[1]
    tiles_per_group = (m // gate.shape[0]) // tm
    row = lambda i: (i, 0)
    return pl.pallas_call(
        _oproj_kernel,
        out_shape=jax.ShapeDtypeStruct((m, d), F32),
        grid=(m // tm,),
        in_specs=[pl.BlockSpec((tm, d), row), pl.BlockSpec((tm, d), row),
                  pl.BlockSpec((None, rows, d), lambda i: (i // tiles_per_group, 0, 0)),
                  pl.BlockSpec((d, d), lambda i: (0, 0))],
        out_specs=pl.BlockSpec((tm, d), row),
        compiler_params=_cparams("arbitrary"),
        name="out_proj",
    )(mix, x2d, gate, w)


def _ffn_kernel(x_ref, sc_ref, sh_ref, g_ref, ng_ref, wu_ref, wd_ref, o_ref, h_scr, *, nf):
    f = pl.program_id(1)

    @pl.when(f == 0)
    def _():
        x = x_ref[...]
        y = _rms(x, x.shape[-1]) * ng_ref[...]
        h_scr[...] = (y * (1.0 + sc_ref[...]) + sh_ref[...]).astype(BF)
        o_ref[...] = jnp.zeros(o_ref.shape, F32)

    u = jnp.maximum(_dot(h_scr[...], wu_ref[...]), 0.0)
    o_ref[...] += _dot((u * u).astype(BF), wd_ref[...])

    @pl.when(f == nf - 1)
    def _():
        o_ref[...] = x_ref[...] + g_ref[...] * o_ref[...]


def _ffn(x2d, sc, sh, gate, ng, wu, wd, tm):
    m, d = x2d.shape
    dff = wu.shape[1]
    tf = 1024
    rows = sc.shape[1]
    tiles_per_group = (m // sc.shape[0]) // tm
    row = lambda i, f: (i, 0)
    mod_spec = pl.BlockSpec((None, rows, d), lambda i, f: (i // tiles_per_group, 0, 0))
    return pl.pallas_call(
        functools.partial(_ffn_kernel, nf=dff // tf),
        out_shape=jax.ShapeDtypeStruct((m, d), F32),
        grid=(m // tm, dff // tf),
        in_specs=[pl.BlockSpec((tm, d), row), mod_spec, mod_spec, mod_spec,
                  pl.BlockSpec((1, d), lambda i, f: (0, 0)),
                  pl.BlockSpec((d, tf), lambda i, f: (0, f)),
                  pl.BlockSpec((tf, d), lambda i, f: (f, 0))],
        out_specs=pl.BlockSpec((tm, d), row),
        scratch_shapes=[pltpu.VMEM((tm, d), BF)],
        compiler_params=_cparams("arbitrary", "arbitrary"),
        name="ffn",
    )(x2d, sc, sh, gate, ng, wu, wd)


def _pack_layer(l, p, w_in_t):
    w_all = _win_prep(w_in_t, l, w_in_t.shape[2])

    qk = A_NOPE + A_ROPE
    wuq = p["w_a_uq"][l].reshape(A_Q_RANK, H, qk)
    wuq = jnp.pad(wuq, ((0, 0), (0, 0), (0, A_QK_PAD - qk))).reshape(A_Q_RANK, H * A_QK_PAD).astype(BF)
    wukv = p["w_a_ukv"][l].reshape(A_KV_RANK, H, A_NOPE + A_VDIM)
    wukv = jnp.concatenate([wukv[:, :, :A_NOPE].reshape(A_KV_RANK, H * A_NOPE),
                            wukv[:, :, A_NOPE:].reshape(A_KV_RANK, H * A_VDIM)], axis=1).astype(BF)

    def row(v, width=None):
        v = v.reshape(1, -1).astype(F32)
        if width is not None and v.shape[1] < width:
            v = jnp.pad(v, ((0, 0), (0, width - v.shape[1])))
        return v

    def head_rows(gq, gk, reps):
        return jnp.stack([jnp.tile(gq, reps), jnp.tile(gk, reps)]).astype(F32)

    return dict(
        w_all=w_all, wuq=wuq, wukv=wukv,
        ng1=row(p["norm1_g"][l]), ng2=row(p["norm2_g"][l]),
        gq=row(p["a_q_norm_g"][l]), gkv=row(p["a_kv_norm_g"][l]),
        gkpe=row(p["a_k_gain"][l][A_NOPE:], LANES),
        gqn=row(p["a_q_gain"][l][:A_NOPE]), gqr=row(p["a_q_gain"][l][A_NOPE:], LANES),
        gkn=row(p["a_k_gain"][l][:A_NOPE]),
        gb=head_rows(p["b_q_gain"][l], p["b_k_gain"][l], HW // B_DH),
        gc=head_rows(p["c_q_gain"][l], p["c_k_gain"][l], H),
        gd=head_rows(p["d_q_gain"][l], p["d_k_gain"][l], H),
        fb=row(p["d_forget_b"][l], LANES),
        lam=p["b_lambda"][l].astype(F32), sg=row(p["b_subln_g"][l]),
        rel=p["c_rel_bias"][l].astype(F32),
        wb=p["w_branch"][l].astype(BF), wo=p["w_out"][l].astype(BF),
        wu=p["w_up"][l].astype(BF), wd=p["w_down"][l].astype(BF),
    )


def _rope_tables(pos):
    half = A_ROPE // 2
    inv = ROPE_THETA ** (-jnp.arange(half, dtype=F32) / half)
    ang = pos.astype(F32)[:, None] * inv[None, :]
    cos, sin = jnp.cos(ang), jnp.sin(ang)
    z = jnp.zeros((pos.shape[0], LANES - A_ROPE), F32)
    return jnp.concatenate([cos, cos, z], axis=1), jnp.concatenate([-sin, sin, z], axis=1)


def _heads_to_rows(x, lanes):
    b, t, h = x.shape
    return jnp.pad(jnp.swapaxes(x, 1, 2), ((0, 0), (0, 8 - h), (0, lanes - t)))


def _front(x2d, mods, pk, tabs, tm):
    sh1, sc1 = mods[0], mods[1]
    cos_t, sin_t = tabs
    (hb, cqn, ckvn, kpe, kpep, logf, qb, kb, vb, qc, kc, vc, qd, kd, vd) = _proj(
        x2d, sc1, sh1, pk["ng1"], pk["w_all"], pk["gq"], pk["gkv"], pk["gkpe"], pk["gb"], pk["gc"],
        pk["gd"], pk["fb"], cos_t, sin_t, tm)
    qa, ka, va = _aprep(cqn, ckvn, kpep, pk["wuq"], pk["wukv"], pk["gqn"], pk["gqr"], pk["gkn"],
                        cos_t, sin_t, tm)
    return dict(h=hb, ckv=ckvn, kpe=kpe, logf=logf, qa=qa, ka=ka, va=va, qb=qb, kb=kb, vb=vb,
                qc=qc, kc=kc, vc=vc, qd=qd, kd=kd, vd=vd)


def _back(x2d, hb, outs, mods, pk, tm):
    g1, sh2, sc2, g2 = mods[2], mods[3], mods[4], mods[5]
    mix = _gmerge(hb, outs, pk["w_all"], pk["wb"], tm)
    x2d = _oproj(mix, x2d, g1, pk["wo"], tm)
    return _ffn(x2d, sc2, sh2, g2, pk["ng2"], pk["wu"], pk["wd"], tm)


def _prompt_layer(x2d, mods, pk, tabs, bias_tiles, b, s, l):
    tm = 512
    f = _front(x2d, mods, pk, tabs, tm)
    r3 = lambda a: a.reshape(b, s, a.shape[-1])
    o_a = _pattn("a", r3(f["qa"]), r3(f["ka"]), r3(f["va"]), ())
    o_b = _pattn("b", r3(f["qb"]), r3(f["kb"]), r3(f["vb"]), (pk["lam"], pk["sg"]), l)
    o_c = _pattn("c", r3(f["qc"]), r3(f["kc"]), r3(f["vc"]), (bias_tiles,))
    frow = _cumsum_rows(_heads_to_rows(r3(f["logf"]), s))
    fcol = jnp.swapaxes(frow[:, :H, :], 1, 2)
    o_d = _pattn("d", r3(f["qd"]), r3(f["kd"]), r3(f["vd"]), (fcol, frow))
    outs = [o.reshape(b * s, HW) for o in (o_a, o_b, o_c, o_d)]
    x2d = _back(x2d, f["h"], outs, mods, pk, tm)
    keep = min(BAND_ROWS, s)
    state = (r3(f["ckv"]), r3(f["kpe"]),
             f["kb"].reshape(b, s, H, 2 * B_DH), f["vb"].reshape(b, s, H, 2 * B_DH),
             f["kc"].reshape(b, s, H, C_DH)[:, s - keep:], f["vc"].reshape(b, s, H, C_DH)[:, s - keep:],
             f["kd"].reshape(b, s, H, D_DH), f["vd"].reshape(b, s, H, D_DH), r3(f["logf"]))
    return x2d, state


def _sample_layer(x2d, mods, pk, tabs, bias_tiles, caches, nb, t, l):
    lat_c, kpe_t, kb_c, vb_c, kc_c, vc_c, kd_c, vd_c, logf_c = caches
    past = lat_c.shape[2]
    tm = nb * t
    f = _front(x2d, mods, pk, tabs, tm)
    o_a = _sattn("a", l, f["qa"], lat_c, kpe_t, f["ka"], f["va"], (pk["wukv"], pk["gkn"]), t, past)
    o_b = _sattn("b", l, f["qb"], kb_c, vb_c, f["kb"], f["vb"], (pk["lam"], pk["sg"]), t, past)
    o_c = _sattn("c", l, f["qc"], kc_c, vc_c, f["kc"], f["vc"], (bias_tiles,), t, past)
    lfc = _heads_to_rows(logf_c[l].astype(F32), past)
    lfn = _heads_to_rows(f["logf"].reshape(nb, t, H), LANES)
    o_d = _sattn("d", l, f["qd"], kd_c, vd_c, f["kd"], f["vd"], (lfc, lfn), t, past)
    x2d = _back(x2d, f["h"], [o_a, o_b, o_c, o_d], mods, pk, tm)
    r3 = lambda a: a.reshape(nb, t, a.shape[-1])
    r4 = lambda a, dh: a.reshape(nb, t, H, dh)
    state = (r3(f["ckv"]), r3(f["kpe"]), r4(f["kb"], 2 * B_DH), r4(f["vb"], 2 * B_DH),
             r4(f["kc"], C_DH), r4(f["vc"], C_DH), r4(f["kd"], D_DH), r4(f["vd"], D_DH), r3(f["logf"]))
    return x2d, state


def kernel(x_prompt, x_sample, c_prompt, c_sample,
           cache_a_latent, cache_a_kpe, cache_b_k, cache_b_v, cache_c_k, cache_c_v,
           cache_d_k, cache_d_v, cache_d_logf,
           norm1_g, norm2_g, w_ada, b_ada, w_in,
           a_q_norm_g, a_kv_norm_g, w_a_uq, w_a_ukv, a_q_gain, a_k_gain,
           b_q_gain, b_k_gain, b_lambda, b_subln_g,
           c_q_gain, c_k_gain, c_rel_bias,
           d_q_gain, d_k_gain, d_forget_b,
           w_branch, w_out, w_up, w_down):
    params = dict(norm1_g=norm1_g, norm2_g=norm2_g, w_in=w_in, a_q_norm_g=a_q_norm_g,
                  a_kv_norm_g=a_kv_norm_g, w_a_uq=w_a_uq, w_a_ukv=w_a_ukv, a_q_gain=a_q_gain,
                  a_k_gain=a_k_gain, b_q_gain=b_q_gain, b_k_gain=b_k_gain, b_lambda=b_lambda,
                  b_subln_g=b_subln_g, c_q_gain=c_q_gain, c_k_gain=c_k_gain, c_rel_bias=c_rel_bias,
                  d_q_gain=d_q_gain, d_k_gain=d_k_gain, d_forget_b=d_forget_b,
                  w_branch=w_branch, w_out=w_out, w_up=w_up, w_down=w_down)
    depth = w_in.shape[0]
    b, s, d = x_prompt.shape
    nb, t, _ = x_sample.shape
    past = cache_a_latent.shape[2]
    rows_c = cache_c_k.shape[2]

    n_c = b + nb
    c_all = jnp.pad(jnp.concatenate([c_prompt, c_sample], axis=0), ((0, (-n_c) % 8), (0, 0)))
    mod_all = _ada(c_all, w_ada, b_ada)

    tabs_p = _rope_tables(jnp.arange(s, dtype=jnp.int32))
    tabs_s = _rope_tables(jnp.tile(past + jnp.arange(t, dtype=jnp.int32), nb))

    tile = 256
    p_bases = [(BAND_ROWS // tile - k) * tile for k in range(BAND_ROWS // tile + 1)]

    x_p = x_prompt.reshape(b * s, d)
    x_s = x_sample.reshape(nb * t, d)
    w_in_t = jnp.transpose(w_in, (2, 0, 1))
    caches = (cache_a_latent, jnp.swapaxes(cache_a_kpe, 2, 3), cache_b_k, cache_b_v, cache_c_k,
              cache_c_v, cache_d_k, cache_d_v, cache_d_logf)
    states_p, states_s = [], []
    for l in range(depth):
        pk = _pack_layer(l, params, w_in_t)
        mod = mod_all[l]
        mods_p = [m.reshape(b, 1, d) for m in jnp.split(mod[:b], 6, axis=-1)]
        mods_s = [jnp.repeat(m, t, axis=0).reshape(1, nb * t, d) for m in jnp.split(mod[b:n_c], 6, axis=-1)]
        bias_p = _relbias(pk["rel"], p_bases, tile, tile, -(CHUNK - 1), LOG2E)
        bias_s = jnp.concatenate(
            [_relbias(pk["rel"], [rows_c], t, rows_c, -REL_CLIP)[:, 0],
             _relbias(pk["rel"], [0], t, LANES, -REL_CLIP)[:, 0]], axis=-1)
        x_p, st_p = _prompt_layer(x_p, mods_p, pk, tabs_p, bias_p, b, s, l)
        x_s, st_s = _sample_layer(x_s, mods_s, pk, tabs_s, bias_s, caches, nb, t, l)
        states_p.append(st_p)
        states_s.append(st_s)
    sp = [jnp.stack(z) for z in zip(*states_p)]
    ss = [jnp.stack(z) for z in zip(*states_s)]
    out = [x_p.reshape(b, s, d), x_s.reshape(nb, t, d)]
    for a, c in zip(sp, ss):
        out += [a, c]
    return tuple(out)
```

```python
import functools
import math

import jax
import jax.numpy as jnp
from jax import lax
from jax.experimental import pallas as pl
from jax.experimental.pallas import tpu as pltpu

BF = jnp.bfloat16
F32 = jnp.float32

CHUNK = 64
EPS = 1e-6
NEG_INF = -1e30
H = 4
A_NOPE, A_ROPE, A_VDIM = 128, 64, 128
A_Q_RANK, A_KV_RANK = 512, 256
A_QK_PAD = 256
ROPE_THETA = 10000.0
B_DH = 64
C_DH = 128
BAND_CHUNKS = 8
BAND_ROWS = BAND_CHUNKS * CHUNK
REL_CLIP = 128
D_DH = 128
HW = 512
N_BRANCH = 4
LANES = 128
PROJ_TILE = 512

VMEM_LIMIT_BYTES = 56 * 1024 * 1024


def _cparams(*sem):
    return pltpu.CompilerParams(dimension_semantics=sem, vmem_limit_bytes=VMEM_LIMIT_BYTES)


def _nt_dot(a, b):
    return lax.dot_general(a, b, (((1,), (1,)), ((), ())), preferred_element_type=F32)


def _dot(a, b):
    return jnp.dot(a, b, preferred_element_type=F32)


def _rms(z, n):
    ms = jnp.sum(z * z, axis=-1, keepdims=True) * (1.0 / n)
    return z * lax.rsqrt(ms + EPS)


def _rms_groups(z, gs):
    w = z.shape[-1]
    if gs >= LANES:
        parts = [_rms(z[:, g * gs:(g + 1) * gs], gs) for g in range(w // gs)]
        return parts[0] if len(parts) == 1 else jnp.concatenate(parts, axis=-1)
    assert gs * 2 == LANES
    lo = lax.broadcasted_iota(jnp.int32, (1, LANES), 1) < gs
    parts = []
    for g in range(w // LANES):
        zz = z[:, g * LANES:(g + 1) * LANES]
        sq = zz * zz
        s_lo = jnp.sum(jnp.where(lo, sq, 0.0), axis=-1, keepdims=True)
        s_hi = jnp.sum(jnp.where(lo, 0.0, sq), axis=-1, keepdims=True)
        ms = jnp.where(lo, s_lo, s_hi) * (1.0 / gs)
        parts.append(zz * lax.rsqrt(ms + EPS))
    return jnp.concatenate(parts, axis=-1)


def _rope128(r, cos, sin):
    half = A_ROPE // 2
    lane = lax.broadcasted_iota(jnp.int32, (1, LANES), 1)
    swapped = jnp.where(lane < half, pltpu.roll(r, LANES - half, 1), pltpu.roll(r, half, 1))
    return r * cos + swapped * sin


CHUNK_SHIFT = CHUNK.bit_length() - 1
assert 1 << CHUNK_SHIFT == CHUNK


def _chunk(pos):
    return jnp.right_shift(pos, CHUNK_SHIFT)


def _log_sigmoid(x):
    return jnp.minimum(x, 0.0) - jnp.log1p(jnp.exp(-jnp.abs(x)))


def _sigmoid(x):
    return 1.0 / (1.0 + jnp.exp(-x))


def _ada_kernel(c_ref, w_ref, b_ref, o_ref):
    c = c_ref[...]
    a = (c * _sigmoid(c)).astype(BF)
    o_ref[...] = _dot(a, w_ref[...].astype(BF)) + b_ref[...]


def _ada(c_all, w_ada, b_ada):
    depth, d, n = w_ada.shape
    r = c_all.shape[0]
    tn = 1024
    return pl.pallas_call(
        _ada_kernel,
        out_shape=jax.ShapeDtypeStruct((depth, r, n), F32),
        grid=(depth, n // tn),
        in_specs=[
            pl.BlockSpec((r, d), lambda l, j: (0, 0)),
            pl.BlockSpec((None, d, tn), lambda l, j: (l, 0, j)),
            pl.BlockSpec((None, 1, tn), lambda l, j: (l, 0, j)),
        ],
        out_specs=pl.BlockSpec((None, r, tn), lambda l, j: (l, 0, j)),
        compiler_params=_cparams("arbitrary", "arbitrary"),
        name="ada_mod",
    )(c_all, w_ada, b_ada.reshape(depth, 1, n))


WIN_TILE = PROJ_TILE
WIN_PROJ_STARTS = (0, A_Q_RANK, 832, 1344, 1856, 2368, 2880, 3392, 3904, 4416, 4928, 5440)
WIN_GATE_START = 5444
PROJ_TILES = len(WIN_PROJ_STARTS)


def _winprep_kernel(off_ref, w_hbm, o_ref, buf, sem, *, layer, nsteps):
    s = pl.program_id(0)
    slot = s % 2

    def copy(step, slot_):
        return pltpu.make_async_copy(w_hbm.at[pl.ds(off_ref[step], WIN_TILE), layer, :],
                                     buf.at[slot_], sem.at[slot_])

    @pl.when(s == 0)
    def _():
        copy(0, 0).start()

    @pl.when(s + 1 < nsteps)
    def _():
        copy(s + 1, 1 - slot).start()

    copy(s, slot).wait()
    o_ref[...] = buf[slot].T.astype(BF)


def _win_prep(w_in_t, layer, d):
    gate_tiles = N_BRANCH * d // WIN_TILE
    starts = WIN_PROJ_STARTS + tuple(WIN_GATE_START + g * WIN_TILE for g in range(gate_tiles))
    assert starts[-1] + WIN_TILE == w_in_t.shape[0]
    nsteps = len(starts)
    rows = w_in_t.shape[2]
    return pl.pallas_call(
        functools.partial(_winprep_kernel, layer=layer, nsteps=nsteps),
        out_shape=jax.ShapeDtypeStruct((rows, nsteps * WIN_TILE), BF),
        grid_spec=pltpu.PrefetchScalarGridSpec(
            num_scalar_prefetch=1,
            grid=(nsteps,),
            in_specs=[pl.BlockSpec(memory_space=pl.ANY)],
            out_specs=pl.BlockSpec((rows, WIN_TILE), lambda s, off: (0, s)),
            scratch_shapes=[pltpu.VMEM((2, WIN_TILE, rows), F32), pltpu.SemaphoreType.DMA((2,))],
        ),
        compiler_params=_cparams("arbitrary"),
        name="w_in_repack",
    )(jnp.asarray(starts, jnp.int32), w_in_t)


PROJ_ROW_SPLIT = 2


def _row_parts(tm):
    part = tm // PROJ_ROW_SPLIT if tm % (PROJ_ROW_SPLIT * 16) == 0 else tm
    return [slice(r, r + part) for r in range(0, tm, part)]


def _mod_rows(ref, rows):
    return ref[...] if ref.shape[0] == 1 else ref[rows, :]


def _proj0_kernel(x_ref, sc_ref, sh_ref, ng_ref, w_ref, gq_ref, gkv_ref, gkpe_ref, fb_ref,
                  cos_ref, sin_ref, h_out, cq_out, ckv_out, kpe_out, kpep_out, logf_out):
    j = pl.program_id(1)
    parts = _row_parts(x_ref.shape[0])

    @pl.when(j == 0)
    def _():
        for rows in parts:
            x = x_ref[rows, :]
            y = _rms(x, x.shape[-1]) * ng_ref[...]
            h_out[rows, :] = (y * (1.0 + _mod_rows(sc_ref, rows)) + _mod_rows(sh_ref, rows)).astype(BF)

    def tile(epilogue):
        for rows in parts:
            epilogue(rows, _dot(h_out[rows, :], w_ref[...]))

    def cq_tile(rows, z):
        cq_out[rows, :] = (_rms(z, A_Q_RANK) * gq_ref[...]).astype(BF)

    def kv_tile(rows, z):
        ckv_out[rows, :] = _rms(z[:, :A_KV_RANK], A_KV_RANK) * gkv_ref[...]
        lane = lax.broadcasted_iota(jnp.int32, (1, LANES), 1)
        kp = jnp.where(lane < A_ROPE, z[:, A_KV_RANK:A_KV_RANK + LANES], 0.0)
        kp = _rope128(_rms(kp, A_ROPE) * gkpe_ref[...], cos_ref[rows, :], sin_ref[rows, :])
        kpe_out[rows, :] = kp[:, :A_ROPE]
        kpep_out[rows, :] = kp.astype(BF)

    def f_tile(rows, z):
        logf_out[rows, :] = _log_sigmoid(z[:, :LANES] + fb_ref[...])[:, :H]

    for jj, epilogue in enumerate((cq_tile, kv_tile, f_tile)):
        pl.when(j == jj)(functools.partial(tile, epilogue))


def _projg_kernel(h_ref, w_ref, g_ref, q_out, k_out, v_out, *, gs):
    j = pl.program_id(1)
    parts = _row_parts(h_ref.shape[0])

    def tile(epilogue):
        for rows in parts:
            epilogue(rows, _dot(h_ref[rows, :], w_ref[...]))

    def q_tile(rows, z):
        q_out[rows, :] = (_rms_groups(z, gs) * g_ref[0:1, :]).astype(BF)

    def k_tile(rows, z):
        k_out[rows, :] = _rms_groups(z, gs) * g_ref[1:2, :]

    def v_tile(rows, z):
        v_out[rows, :] = z

    for jj, epilogue in enumerate((q_tile, k_tile, v_tile)):
        pl.when(j == jj)(functools.partial(tile, epilogue))


def _proj0(x2d, sc, sh, ng, w_all, gq, gkv, gkpe, fb, cos_t, sin_t, tm):
    m, d = x2d.shape
    groups, rows = sc.shape[0], sc.shape[1]
    tiles_per_group = (m // groups) // tm
    tab_blocks = cos_t.shape[0] // tm
    row = lambda i, j: (i, 0)
    const2 = lambda i, j: (0, 0)
    mod_spec = pl.BlockSpec((None, rows, d), lambda i, j: (i // tiles_per_group, 0, 0))
    tab_spec = pl.BlockSpec((tm, LANES), lambda i, j: (i % tab_blocks, 0))

    def out(width, dtype):
        return jax.ShapeDtypeStruct((m, width), dtype), pl.BlockSpec((tm, width), row)

    outs = [out(d, BF), out(A_Q_RANK, BF), out(A_KV_RANK, F32), out(A_ROPE, F32), out(LANES, BF),
            out(H, F32)]
    return pl.pallas_call(
        _proj0_kernel,
        out_shape=[o[0] for o in outs],
        grid=(m // tm, 3),
        in_specs=[
            pl.BlockSpec((tm, d), row),
            mod_spec, mod_spec,
            pl.BlockSpec((1, d), const2),
            pl.BlockSpec((d, PROJ_TILE), lambda i, j: (0, jnp.where(j == 2, PROJ_TILES - 1, j))),
            pl.BlockSpec((1, A_Q_RANK), const2),
            pl.BlockSpec((1, A_KV_RANK), const2),
            pl.BlockSpec((1, LANES), const2),
            pl.BlockSpec((1, LANES), const2),
            tab_spec, tab_spec,
        ],
        out_specs=[o[1] for o in outs],
        compiler_params=_cparams("arbitrary", "arbitrary"),
        name="in_proj_a",
    )(x2d, sc, sh, ng, w_all, gq, gkv, gkpe, fb, cos_t, sin_t)


def _projg(hb, w_all, gains, first_tile, gs, tm, name):
    m, d = hb.shape
    row = lambda i, j: (i, 0)
    return pl.pallas_call(
        functools.partial(_projg_kernel, gs=gs),
        out_shape=[jax.ShapeDtypeStruct((m, HW), BF), jax.ShapeDtypeStruct((m, HW), F32),
                   jax.ShapeDtypeStruct((m, HW), F32)],
        grid=(m // tm, 3),
        in_specs=[pl.BlockSpec((tm, d), row),
                  pl.BlockSpec((d, PROJ_TILE), lambda i, j: (0, first_tile + j)),
                  pl.BlockSpec((2, HW), lambda i, j: (0, 0))],
        out_specs=[pl.BlockSpec((tm, HW), row)] * 3,
        compiler_params=_cparams("arbitrary", "arbitrary"),
        name=name,
    )(hb, w_all, gains)


def _aprep_kernel(cq_ref, ckv_ref, kpep_ref, wuq_ref, wukv_ref, gqn_ref, gqr_ref, gkn_ref,
                  cos_ref, sin_ref, q_out, k_out, v_out):
    zq = _dot(cq_ref[...], wuq_ref[...])
    cos, sin = cos_ref[...], sin_ref[...]
    parts = []
    for h in range(H):
        nope = zq[:, h * A_QK_PAD:h * A_QK_PAD + A_NOPE]
        rp = zq[:, h * A_QK_PAD + A_NOPE:(h + 1) * A_QK_PAD]
        parts.append(_rms(nope, A_NOPE) * gqn_ref[...])
        parts.append(_rope128(_rms(rp, A_ROPE) * gqr_ref[...], cos, sin))
    q_out[...] = jnp.concatenate(parts, axis=-1).astype(BF)

    zkv = _dot(ckv_ref[...].astype(BF), wukv_ref[...])
    kp = kpep_ref[...]
    parts = []
    for h in range(H):
        kn = _rms(zkv[:, h * A_NOPE:(h + 1) * A_NOPE], A_NOPE) * gkn_ref[...]
        parts.append(kn.astype(BF))
        parts.append(kp)
    k_out[...] = jnp.concatenate(parts, axis=-1)
    v_out[...] = zkv[:, H * A_NOPE:].astype(BF)


def _aprep(cqn, ckvn, kpep, wuq, wukv, gqn, gqr, gkn, cos_t, sin_t, tm):
    m = cqn.shape[0]
    tab_blocks = cos_t.shape[0] // tm
    row = lambda i: (i, 0)
    const = lambda i: (0, 0)
    tab_spec = pl.BlockSpec((tm, LANES), lambda i: (i % tab_blocks, 0))
    return pl.pallas_call(
        _aprep_kernel,
        out_shape=[jax.ShapeDtypeStruct((m, H * A_QK_PAD), BF),
                   jax.ShapeDtypeStruct((m, H * A_QK_PAD), BF),
                   jax.ShapeDtypeStruct((m, HW), BF)],
        grid=(m // tm,),
        in_specs=[
            pl.BlockSpec((tm, A_Q_RANK), row),
            pl.BlockSpec((tm, A_KV_RANK), row),
            pl.BlockSpec((tm, LANES), row),
            pl.BlockSpec(wuq.shape, const),
            pl.BlockSpec(wukv.shape, const),
            pl.BlockSpec((1, LANES), const),
            pl.BlockSpec((1, LANES), const),
            pl.BlockSpec((1, LANES), const),
            tab_spec, tab_spec,
        ],
        out_specs=[pl.BlockSpec((tm, H * A_QK_PAD), row),
                   pl.BlockSpec((tm, H * A_QK_PAD), row),
                   pl.BlockSpec((tm, HW), row)],
        compiler_params=_cparams("arbitrary"),
        name="a_prep",
    )(cqn, ckvn, kpep, wuq, wukv, gqn, gqr, gkn, cos_t, sin_t)


CUM_BLOCK = 256


def _tri_upper(n):
    r = lax.broadcasted_iota(jnp.int32, (n, n), 0)
    c = lax.broadcasted_iota(jnp.int32, (n, n), 1)
    return jnp.where(r <= c, 1.0, 0.0).astype(BF)


def _dot3(x, u):
    hi = x.astype(BF)
    r1 = x - hi.astype(F32)
    mid = r1.astype(BF)
    lo = (r1 - mid.astype(F32)).astype(BF)
    return _dot(hi, u) + _dot(mid, u) + _dot(lo, u)


def _cumsum_lanes(src_ref, dst_ref):
    u = _tri_upper(CUM_BLOCK)
    carry = jnp.zeros((8, 1), F32)
    for b in range(src_ref.shape[-1] // CUM_BLOCK):
        blk = slice(b * CUM_BLOCK, (b + 1) * CUM_BLOCK)
        c = _dot3(src_ref[:, blk], u) + carry
        dst_ref[:, blk] = c
        carry = c[:, CUM_BLOCK - 1:CUM_BLOCK]
    return carry


def _cumsum_kernel(x_ref, o_ref):
    _cumsum_lanes(x_ref, o_ref)


def _cumsum_rows(x):
    b, r, s = x.shape
    return pl.pallas_call(
        _cumsum_kernel,
        out_shape=jax.ShapeDtypeStruct(x.shape, F32),
        grid=(b,),
        in_specs=[pl.BlockSpec((None, r, s), lambda i: (i, 0, 0))],
        out_specs=pl.BlockSpec((None, r, s), lambda i: (i, 0, 0)),
        compiler_params=_cparams("arbitrary"),
        name="forget_cumsum",
    )(x)


def _relbias_kernel(tab_ref, o_ref, *, bases, rows, cols, dmin, mult):
    h = pl.program_id(0)
    r = lax.broadcasted_iota(jnp.int32, (rows, cols), 0)
    c = lax.broadcasted_iota(jnp.int32, (rows, cols), 1)
    for t, base in enumerate(bases):
        idx = jnp.clip(base + r - c, -REL_CLIP, REL_CLIP) + REL_CLIP
        lo = min(max(base - (cols - 1), max(dmin, -REL_CLIP)), REL_CLIP) + REL_CLIP
        hi = min(max(base + rows - 1, -REL_CLIP), REL_CLIP) + REL_CLIP

        def body(e, acc):
            return jnp.where(idx == e, tab_ref[h, e] * mult, acc)

        o_ref[t] = lax.fori_loop(lo, hi + 1, body, jnp.zeros((rows, cols), F32))


def _relbias(table, bases, rows, cols, dmin, mult=1.0):
    nh = table.shape[0]
    return pl.pallas_call(
        functools.partial(_relbias_kernel, bases=tuple(bases), rows=rows, cols=cols, dmin=dmin, mult=mult),
        out_shape=jax.ShapeDtypeStruct((nh, len(bases), rows, cols), F32),
        grid=(nh,),
        in_specs=[pl.BlockSpec(memory_space=pltpu.SMEM)],
        out_specs=pl.BlockSpec((None, len(bases), rows, cols), lambda h: (h, 0, 0, 0)),
        compiler_params=_cparams("arbitrary"),
        name="rel_bias_tiles",
    )(table)


def _diff_lambda(lam_ref, layer_idx):
    lam_init = 0.8 - 0.6 * math.exp(-0.3 * layer_idx)
    lp = lam_ref[...]
    a = jnp.sum(lp[0:1] * lp[1:2], keepdims=True)
    b = jnp.sum(lp[2:3] * lp[3:4], keepdims=True)
    return jnp.exp(a) - jnp.exp(b) + lam_init, lam_init


ALIBI_SLOPES = tuple(2.0 ** (-8.0 * (h + 1) / H) for h in range(H))


LOG2E = math.log2(math.e)
RG = 32


def _pattn_kernel(*refs, mode, tq, tk, nk, layer_idx, back):
    if mode == "a":
        q_ref, k_ref, v_ref, o_ref, m_ref, acc_ref, s_scr, p_scr, pm_scr = refs
        dq, scale = A_QK_PAD, (A_NOPE + A_ROPE) ** -0.5
    elif mode == "b":
        q_ref, k_ref, v_ref, lam_ref, sg_ref, o_ref, m_ref, acc_ref, s_scr, p_scr, pm_scr = refs
        dq, scale = 2 * B_DH, B_DH ** -0.5
    elif mode == "c":
        q_ref, k_ref, v_ref, bias_ref, o_ref, m_ref, acc_ref, s_scr, p_scr, pm_scr = refs
        dq, scale = C_DH, C_DH ** -0.5
    else:
        q_ref, k_ref, v_ref, fq_ref, fk_ref, o_ref, m_ref, acc_ref, s_scr, p_scr, pm_scr, fq_scr = refs
        dq, scale = D_DH, D_DH ** -0.5
    assert tq == tk and CHUNK % RG == 0 and tk % LANES == 0
    dv = HW // H
    c1 = scale * LOG2E
    nchunk = tk // LANES
    i = pl.program_id(1)
    j = pl.program_id(2)
    if mode == "c":
        kb = i - back + j
        valid = kb >= 0
        first = jnp.maximum(back - i, 0)
        last = nk - 1
    else:
        last = i
        kb = j
        valid = j <= last
        first = 0

    @pl.when(j == first)
    def _():
        m_ref[...] = jnp.full(m_ref.shape, -jnp.inf, F32)
        acc_ref[...] = jnp.zeros(acc_ref.shape, F32)
        if mode == "d":
            for h in range(H):
                fq_scr[h] = jnp.broadcast_to(fq_ref[:, h:h + 1] * LOG2E, (tq, LANES))

    lane = lax.broadcasted_iota(jnp.int32, (RG, LANES), 1)
    lane_minus_row = lane - lax.broadcasted_iota(jnp.int32, (RG, LANES), 0)

    def chunk_visibility(kind, r0, c):
        c0 = c * LANES
        if kind == "none":
            return "all"
        if kind == "lt":
            bound = (r0 // CHUNK + 1) * CHUNK
            if c0 >= bound:
                return "none"
            return "all" if c0 + LANES <= bound else lane < bound - c0
        if kind == "ge":
            bound = r0 // CHUNK * CHUNK
            if c0 + LANES <= bound:
                return "none"
            return "all" if c0 >= bound else lane >= bound - c0
        assert kind == "causal"
        if c0 > r0 + RG - 1:
            return "none"
        return "all" if c0 + LANES - 1 <= r0 else lane_minus_row <= r0 - c0

    def sweep1(idxs, h, g, kind, delta):
        r0 = g * RG
        rows = slice(r0, r0 + RG)
        pm = [None] * len(idxs)
        for c in range(nchunk):
            vis = chunk_visibility(kind, r0, c)
            if isinstance(vis, str) and vis == "none":
                continue
            cols = slice(c * LANES, (c + 1) * LANES)
            if mode == "b":
                dist = jnp.abs(lane_minus_row.astype(F32) - (delta + float(r0 - c * LANES)))
                bias = dist * (-ALIBI_SLOPES[h] * LOG2E)
            elif mode == "c":
                bias = bias_ref[h, rows, cols]
            elif mode == "d":
                bias = fq_scr[h, rows, :] - fk_ref[h:h + 1, cols] * LOG2E
            for n, idx in enumerate(idxs):
                t = s_scr[idx, rows, cols] * c1
                if mode != "a":
                    t = t + bias
                if not isinstance(vis, str):
                    t = jnp.where(vis, t, NEG_INF)
                s_scr[idx, rows, cols] = t
                pm[n] = t if pm[n] is None else jnp.maximum(pm[n], t)
        for n, idx in enumerate(idxs):
            pm_scr[idx, rows, :] = pm[n]

    def sweep2(idx, g, kind):
        r0 = g * RG
        rows = slice(r0, r0 + RG)
        m_rows = m_ref[idx, rows, :]
        for c in range(nchunk):
            cols = slice(c * LANES, (c + 1) * LANES)
            vis = chunk_visibility(kind, r0, c)
            if isinstance(vis, str) and vis == "none":
                p_scr[idx, rows, cols] = jnp.zeros((RG, LANES), BF)
            else:
                p_scr[idx, rows, cols] = jnp.exp2(s_scr[idx, rows, cols] - m_rows).astype(BF)

    def block(kind):
        delta = ((i - kb) * tq).astype(F32) if mode == "b" else None
        ones = jnp.ones((tk, LANES), BF)
        for h in range(H):
            q = q_ref[:, h * dq:(h + 1) * dq]
            k = k_ref[:, h * dq:(h + 1) * dq].astype(BF)
            v1 = jnp.concatenate([v_ref[:, h * dv:(h + 1) * dv].astype(BF), ones], axis=-1)
            if mode == "b":
                lo = lax.broadcasted_iota(jnp.int32, (1, dq), 1) < B_DH
                zero = jnp.zeros_like(q)
                s_scr[h] = _nt_dot(jnp.where(lo, q, zero), k)
                s_scr[H + h] = _nt_dot(jnp.where(lo, zero, q), k)
                idxs = (h, H + h)
            else:
                s_scr[h] = _nt_dot(q, k)
                idxs = (h,)
            for g in range(tq // RG):
                sweep1(idxs, h, g, kind, delta)
            for idx in idxs:
                m_prev = m_ref[idx]
                m_new = jnp.maximum(m_prev, jnp.max(pm_scr[idx], axis=-1, keepdims=True))
                m_ref[idx] = m_new
                alpha = jnp.exp2(m_prev - m_new)
                for g in range(tq // RG):
                    sweep2(idx, g, kind)
                acc_ref[idx] = jnp.concatenate([alpha, alpha], axis=-1) * acc_ref[idx] + _dot(p_scr[idx], v1)

    if mode == "c":
        assert back == 2 and BAND_ROWS == back * tk
        for jj, kind in enumerate(("ge", "none", "lt")):
            pl.when((j == jj) & valid)(functools.partial(block, kind))
    else:
        pl.when(j < last)(functools.partial(block, "none"))
        pl.when(j == last)(functools.partial(block, "causal" if mode == "d" else "lt"))

    @pl.when(j == last)
    def _():
        for h in range(H):
            o = acc_ref[h, :, :dv] / acc_ref[h, :, dv:]
            if mode == "b":
                lam, lam_init = _diff_lambda(lam_ref, layer_idx)
                o = o - lam * (acc_ref[H + h, :, :dv] / acc_ref[H + h, :, dv:])
                o = (_rms(o, dv) * sg_ref[...]) * (1.0 - lam_init)
            o_ref[:, h * dv:(h + 1) * dv] = o.astype(BF)


def _pattn(mode, q, k, v, extras, layer_idx=0):
    b, s, qw = q.shape
    kw = k.shape[-1]
    back = 0
    if mode == "c":
        tq = tk = 256
        back = BAND_ROWS // tk
        nk = back + 1
        kmap = lambda bi, i, j: (bi, jnp.maximum(i - back + j, 0), 0)
    else:
        tq = tk = 512
        nk = s // tk
        kmap = lambda bi, i, j: (bi, jnp.minimum(j, ((i + 1) * tq - 1) // tk), 0)
    qmap = lambda bi, i, j: (bi, i, 0)
    in_specs = [pl.BlockSpec((None, tq, qw), qmap),
                pl.BlockSpec((None, tk, kw), kmap),
                pl.BlockSpec((None, tk, HW), kmap)]
    if mode == "b":
        lam, sg = extras
        in_specs += [pl.BlockSpec(lam.shape, lambda bi, i, j: (0, 0)),
                     pl.BlockSpec(sg.shape, lambda bi, i, j: (0, 0))]
    elif mode == "c":
        (bias,) = extras
        in_specs += [pl.BlockSpec((H, None, tq, tk), lambda bi, i, j: (0, j, 0, 0))]
    elif mode == "d":
        fcol, frow = extras
        in_specs += [pl.BlockSpec((None, tq, H), qmap),
                     pl.BlockSpec((None, 8, tk), lambda bi, i, j: (bi, 0, kmap(bi, i, j)[1]))]
    nstate = 2 * H if mode == "b" else H
    return pl.pallas_call(
        functools.partial(_pattn_kernel, mode=mode, tq=tq, tk=tk, nk=nk, layer_idx=layer_idx, back=back),
        out_shape=jax.ShapeDtypeStruct((b, s, HW), BF),
        grid=(b, s // tq, nk),
        in_specs=in_specs,
        out_specs=pl.BlockSpec((None, tq, HW), qmap),
        scratch_shapes=[pltpu.VMEM((nstate, tq, LANES), F32),
                        pltpu.VMEM((nstate, tq, 2 * (HW // H)), F32),
                        pltpu.VMEM((nstate, tq, tk), F32),
                        pltpu.VMEM((nstate, tq, tk), BF),
                        pltpu.VMEM((nstate, tq, LANES), F32)]
                       + ([pltpu.VMEM((H, tq, LANES), F32)] if mode == "d" else []),
        compiler_params=_cparams("arbitrary", "arbitrary", "arbitrary"),
        name="prompt_attn_" + mode,
    )(q, k, v, *extras)


def _pad_rows(x, rows):
    return jnp.concatenate([x, jnp.zeros((rows - x.shape[0], x.shape[1]), x.dtype)], axis=0)


def _prefetch_head_caches(k_hbm, v_hbm, kbuf, vbuf, sem, layer, nb):
    b = pl.program_id(0)
    slot = b % 2
    dh = kbuf.shape[-1] // H

    def copies(bidx, slot_):
        out = []
        for h in range(H):
            cols = pl.ds(h * dh, dh)
            out.append(pltpu.make_async_copy(k_hbm.at[layer, bidx, :, h, :], kbuf.at[slot_, :, cols],
                                             sem.at[slot_, 0]))
            out.append(pltpu.make_async_copy(v_hbm.at[layer, bidx, :, h, :], vbuf.at[slot_, :, cols],
                                             sem.at[slot_, 1]))
        return out

    @pl.when(b == 0)
    def _():
        for c in copies(0, 0):
            c.start()

    @pl.when(b + 1 < nb)
    def _():
        for c in copies(b + 1, 1 - slot):
            c.start()

    for c in copies(b, slot):
        c.wait()
    return slot


def _sattn_kernel(*refs, mode, t, past, layer, nb):
    if mode == "a":
        (q_ref, lat_ref, kpet_ref, kn_ref, vn_ref, wukv_ref, gkn_ref, o_ref) = refs
        dq, scale = A_QK_PAD, (A_NOPE + A_ROPE) ** -0.5
        rows_c = lat_ref.shape[0]
    else:
        if mode == "b":
            (q_ref, k_hbm, v_hbm, kn_ref, vn_ref, lam_ref, sg_ref, o_ref, kbuf, vbuf, sem) = refs
            dq, scale = 2 * B_DH, B_DH ** -0.5
        elif mode == "c":
            (q_ref, k_hbm, v_hbm, kn_ref, vn_ref, bias_ref, o_ref, kbuf, vbuf, sem) = refs
            dq, scale = C_DH, C_DH ** -0.5
        else:
            (q_ref, k_hbm, v_hbm, kn_ref, vn_ref, lfc_ref, lfn_ref, o_ref, f_scr, kbuf, vbuf, sem) = refs
            dq, scale = D_DH, D_DH ** -0.5
        rows_c = kbuf.shape[1]
        slot = _prefetch_head_caches(k_hbm, v_hbm, kbuf, vbuf, sem, layer, nb)
        kc_ref, vc_ref = kbuf.at[slot], vbuf.at[slot]
    dv = HW // H
    npad = LANES
    qpos_c = past + lax.broadcasted_iota(jnp.int32, (t, rows_c), 0)
    kpos_c = (past - rows_c) + lax.broadcasted_iota(jnp.int32, (t, rows_c), 1)
    qpos_n = past + lax.broadcasted_iota(jnp.int32, (t, npad), 0)
    col_n = lax.broadcasted_iota(jnp.int32, (t, npad), 1)
    kpos_n = past + col_n
    real_n = col_n < t
    if mode in ("a", "b"):
        mask_c = _chunk(kpos_c) <= _chunk(qpos_c)
        mask_n = real_n & (_chunk(kpos_n) <= _chunk(qpos_n))
    elif mode == "c":
        qc_c, kc_c = _chunk(qpos_c), _chunk(kpos_c)
        qc_n, kc_n = _chunk(qpos_n), _chunk(kpos_n)
        mask_c = (kpos_c >= 0) & (kc_c <= qc_c) & (kc_c >= qc_c - BAND_CHUNKS)
        mask_n = real_n & (kc_n <= qc_n) & (kc_n >= qc_n - BAND_CHUNKS)
    else:
        mask_c = kpos_c <= qpos_c
        mask_n = real_n & (kpos_n <= qpos_n)

    if mode == "a":
        lat = lat_ref[...].astype(BF)
        kpe_t = kpet_ref[...]
        kpe_t = jnp.concatenate([kpe_t, jnp.zeros_like(kpe_t)], axis=0).astype(BF)
    if mode == "b":
        dist_c = jnp.abs(qpos_c - kpos_c).astype(F32)
        dist_n = jnp.abs(qpos_n - kpos_n).astype(F32)
        lo = lax.broadcasted_iota(jnp.int32, (1, dq), 1) < B_DH
        lam, lam_init = _diff_lambda(lam_ref, layer)
    if mode == "d":
        carry = _cumsum_lanes(lfc_ref, f_scr)
        f_new = _dot3(lfn_ref[...], _tri_upper(LANES)) + carry
        eye = (lax.broadcasted_iota(jnp.int32, (t, npad), 0) == col_n)

    def softmax_pv(s_c, s_n, v_c, v_n):
        m = jnp.maximum(jnp.max(s_c, axis=-1, keepdims=True), jnp.max(s_n, axis=-1, keepdims=True))
        p_c = jnp.exp(s_c - m)
        p_n = jnp.exp(s_n - m)
        l = jnp.sum(p_c, axis=-1, keepdims=True) + jnp.sum(p_n, axis=-1, keepdims=True)
        return (_dot(p_c.astype(BF), v_c) + _dot(p_n.astype(BF), v_n)) / l

    for h in range(H):
        q = q_ref[:, h * dq:(h + 1) * dq]
        if mode == "a":
            kn = _pad_rows(kn_ref[:, h * dq:(h + 1) * dq], npad)
            vn = _pad_rows(vn_ref[:, h * dv:(h + 1) * dv], npad)
            w = wukv_ref[...]
            k_nope = _dot(lat, w[:, h * A_NOPE:(h + 1) * A_NOPE])
            k_nope = (_rms(k_nope, A_NOPE) * gkn_ref[...]).astype(BF)
            v_c = _dot(lat, w[:, H * A_NOPE + h * A_VDIM:H * A_NOPE + (h + 1) * A_VDIM]).astype(BF)
            s_c = (_nt_dot(q[:, :A_NOPE], k_nope) + _dot(q[:, A_NOPE:], kpe_t)) * scale
            s_n = _nt_dot(q, kn) * scale
        else:
            k_c = kc_ref[:, h * dq:(h + 1) * dq].astype(BF)
            v_c = vc_ref[:, h * dv:(h + 1) * dv].astype(BF)
            kn = _pad_rows(kn_ref[:, h * dq:(h + 1) * dq], npad).astype(BF)
            vn = _pad_rows(vn_ref[:, h * dv:(h + 1) * dv], npad).astype(BF)
        if mode == "b":
            zero = jnp.zeros_like(q)
            q1, q2 = jnp.where(lo, q, zero), jnp.where(lo, zero, q)
            bias_c = -ALIBI_SLOPES[h] * dist_c
            bias_n = -ALIBI_SLOPES[h] * dist_n
            outs = []
            for qq in (q1, q2):
                s_c = jnp.where(mask_c, _nt_dot(qq, k_c) * scale + bias_c, NEG_INF)
                s_n = jnp.where(mask_n, _nt_dot(qq, kn) * scale + bias_n, NEG_INF)
                outs.append(softmax_pv(s_c, s_n, v_c, vn))
            o = outs[0] - lam * outs[1]
            o = (_rms(o, dv) * sg_ref[...]) * (1.0 - lam_init)
        else:
            if mode != "a":
                s_c = _nt_dot(q, k_c) * scale
                s_n = _nt_dot(q, kn) * scale
            if mode == "c":
                s_c = s_c + bias_ref[h, :, :rows_c]
                s_n = s_n + bias_ref[h, :, rows_c:]
            elif mode == "d":
                fq = jnp.sum(jnp.where(eye, f_new[h:h + 1, :], 0.0), axis=-1, keepdims=True)
                s_c = s_c + (fq - f_scr[h:h + 1, :])
                s_n = s_n + (fq - f_new[h:h + 1, :])
            s_c = jnp.where(mask_c, s_c, NEG_INF)
            s_n = jnp.where(mask_n, s_n, NEG_INF)
            o = softmax_pv(s_c, s_n, v_c, vn)
        o_ref[:, h * dv:(h + 1) * dv] = o.astype(BF)


def _sattn(mode, layer, q, cache_k, cache_v, kn, vn, extras, t, past):
    nb = cache_k.shape[1]
    qw = q.shape[-1]
    row = lambda b: (b, 0)
    if mode == "a":
        cache_specs = [pl.BlockSpec((None, None) + cache_k.shape[2:], lambda b: (layer, b, 0, 0)),
                       pl.BlockSpec((None, None) + cache_v.shape[2:], lambda b: (layer, b, 0, 0))]
        dma_scratch = []
    else:
        rows_c = cache_k.shape[2]
        cache_specs = [pl.BlockSpec(memory_space=pl.ANY), pl.BlockSpec(memory_space=pl.ANY)]
        dma_scratch = [pltpu.VMEM((2, rows_c, HW), F32), pltpu.VMEM((2, rows_c, HW), F32),
                       pltpu.SemaphoreType.DMA((2, 2))]
    in_specs = [pl.BlockSpec((t, qw), row)] + cache_specs + [
                pl.BlockSpec((t, kn.shape[-1]), row),
                pl.BlockSpec((t, vn.shape[-1]), row)]
    scratch = []
    if mode == "d":
        lfc, lfn = extras
        in_specs += [pl.BlockSpec((None,) + lfc.shape[1:], lambda b: (b, 0, 0)),
                     pl.BlockSpec((None,) + lfn.shape[1:], lambda b: (b, 0, 0))]
        scratch = [pltpu.VMEM(lfc.shape[1:], F32)]
    elif mode == "c":
        (bias,) = extras
        in_specs += [pl.BlockSpec(bias.shape, lambda b: (0, 0, 0))]
    else:
        in_specs += [pl.BlockSpec(e.shape, lambda b: (0, 0)) for e in extras]
    return pl.pallas_call(
        functools.partial(_sattn_kernel, mode=mode, t=t, past=past, layer=layer, nb=nb),
        out_shape=jax.ShapeDtypeStruct((nb * t, HW), BF),
        grid=(nb,),
        in_specs=in_specs,
        out_specs=pl.BlockSpec((t, HW), row),
        scratch_shapes=scratch + dma_scratch,
        compiler_params=_cparams("arbitrary"),
        name="sample_attn_" + mode,
    )(q, cache_k, cache_v, kn, vn, *extras)


def _gmerge_kernel(h_ref, oa_ref, ob_ref, oc_ref, od_ref, wg0_ref, wg1_ref, wg2_ref, wg3_ref,
                   wb_ref, out_ref):
    hb = h_ref[...]
    acc = None
    for g, (o_ref, wg_ref) in enumerate(((oa_ref, wg0_ref), (ob_ref, wg1_ref),
                                         (oc_ref, wg2_ref), (od_ref, wg3_ref))):
        term = _sigmoid(_dot(hb, wg_ref[...])) * _dot(o_ref[...], wb_ref[g])
        acc = term if acc is None else acc + term
    out_ref[...] = acc.astype(BF)


def _gmerge(hb, outs, wg, wb, tm):
    m, d = hb.shape
    tn = WIN_TILE
    nt = d // tn
    row = lambda i, n: (i, 0)
    wg_specs = [pl.BlockSpec((d, tn), functools.partial(lambda i, n, g: (0, PROJ_TILES + g * nt + n), g=g))
                for g in range(N_BRANCH)]
    return pl.pallas_call(
        _gmerge_kernel,
        out_shape=jax.ShapeDtypeStruct((m, d), BF),
        grid=(m // tm, nt),
        in_specs=[pl.BlockSpec((tm, d), row)] + [pl.BlockSpec((tm, HW), row)] * N_BRANCH + wg_specs
                 + [pl.BlockSpec((N_BRANCH, HW, tn), lambda i, n: (0, 0, n))],
        out_specs=pl.BlockSpec((tm, tn), lambda i, n: (i, n)),
        compiler_params=_cparams("arbitrary", "arbitrary"),
        name="gate_merge",
    )(hb, *outs, wg, wg, wg, wg, wb)


def _oproj_kernel(mix_ref, x_ref, g_ref, w_ref, o_ref):
    o_ref[...] = x_ref[...] + g_ref[...] * _dot(mix_ref[...], w_ref[...])


def _oproj(mix, x2d, gate, w, tm):
    m, d = x2d.shape
    rows = gate.shape[1]
    tiles_per_group = (m // gate.shape[0]) // tm
    row = lambda i: (i, 0)
    return pl.pallas_call(
        _oproj_kernel,
        out_shape=jax.ShapeDtypeStruct((m, d), F32),
        grid=(m // tm,),
        in_specs=[pl.BlockSpec((tm, d), row), pl.BlockSpec((tm, d), row),
                  pl.BlockSpec((None, rows, d), lambda i: (i // tiles_per_group, 0, 0)),
                  pl.BlockSpec((d, d), lambda i: (0, 0))],
        out_specs=pl.BlockSpec((tm, d), row),
        compiler_params=_cparams("arbitrary"),
        name="out_proj",
    )(mix, x2d, gate, w)


def _ffn_kernel(x_ref, sc_ref, sh_ref, g_ref, ng_ref, wu_ref, wd_ref, o_ref, h_scr, *, nf):
    f = pl.program_id(1)

    @pl.when(f == 0)
    def _():
        x = x_ref[...]
        y = _rms(x, x.shape[-1]) * ng_ref[...]
        h_scr[...] = (y * (1.0 + sc_ref[...]) + sh_ref[...]).astype(BF)
        o_ref[...] = jnp.zeros(o_ref.shape, F32)

    u = jnp.maximum(_dot(h_scr[...], wu_ref[...]), 0.0)
    o_ref[...] += _dot((u * u).astype(BF), wd_ref[...])

    @pl.when(f == nf - 1)
    def _():
        o_ref[...] = x_ref[...] + g_ref[...] * o_ref[...]


def _ffn(x2d, sc, sh, gate, ng, wu, wd, tm):
    m, d = x2d.shape
    dff = wu.shape[1]
    tf = 512
    rows = sc.shape[1]
    tiles_per_group = (m // sc.shape[0]) // tm
    row = lambda i, f: (i, 0)
    mod_spec = pl.BlockSpec((None, rows, d), lambda i, f: (i // tiles_per_group, 0, 0))
    return pl.pallas_call(
        functools.partial(_ffn_kernel, nf=dff // tf),
        out_shape=jax.ShapeDtypeStruct((m, d), F32),
        grid=(m // tm, dff // tf),
        in_specs=[pl.BlockSpec((tm, d), row), mod_spec, mod_spec, mod_spec,
                  pl.BlockSpec((1, d), lambda i, f: (0, 0)),
                  pl.BlockSpec((d, tf), lambda i, f: (0, f)),
                  pl.BlockSpec((tf, d), lambda i, f: (f, 0))],
        out_specs=pl.BlockSpec((tm, d), row),
        scratch_shapes=[pltpu.VMEM((tm, d), BF)],
        compiler_params=_cparams("arbitrary", "arbitrary"),
        name="ffn",
    )(x2d, sc, sh, gate, ng, wu, wd)


def _pack_layer(l, p, w_in_t):
    w_all = _win_prep(w_in_t, l, w_in_t.shape[2])

    qk = A_NOPE + A_ROPE
    wuq = p["w_a_uq"][l].reshape(A_Q_RANK, H, qk)
    wuq = jnp.pad(wuq, ((0, 0), (0, 0), (0, A_QK_PAD - qk))).reshape(A_Q_RANK, H * A_QK_PAD).astype(BF)
    wukv = p["w_a_ukv"][l].reshape(A_KV_RANK, H, A_NOPE + A_VDIM)
    wukv = jnp.concatenate([wukv[:, :, :A_NOPE].reshape(A_KV_RANK, H * A_NOPE),
                            wukv[:, :, A_NOPE:].reshape(A_KV_RANK, H * A_VDIM)], axis=1).astype(BF)

    def row(v, width=None):
        v = v.reshape(1, -1).astype(F32)
        if width is not None and v.shape[1] < width:
            v = jnp.pad(v, ((0, 0), (0, width - v.shape[1])))
        return v

    def head_rows(gq, gk, reps):
        return jnp.stack([jnp.tile(gq, reps), jnp.tile(gk, reps)]).astype(F32)

    return dict(
        w_all=w_all, wuq=wuq, wukv=wukv,
        ng1=row(p["norm1_g"][l]), ng2=row(p["norm2_g"][l]),
        gq=row(p["a_q_norm_g"][l]), gkv=row(p["a_kv_norm_g"][l]),
        gkpe=row(p["a_k_gain"][l][A_NOPE:], LANES),
        gqn=row(p["a_q_gain"][l][:A_NOPE]), gqr=row(p["a_q_gain"][l][A_NOPE:], LANES),
        gkn=row(p["a_k_gain"][l][:A_NOPE]),
        gb=head_rows(p["b_q_gain"][l], p["b_k_gain"][l], HW // B_DH),
        gc=head_rows(p["c_q_gain"][l], p["c_k_gain"][l], H),
        gd=head_rows(p["d_q_gain"][l], p["d_k_gain"][l], H),
        fb=row(p["d_forget_b"][l], LANES),
        lam=p["b_lambda"][l].astype(F32), sg=row(p["b_subln_g"][l]),
        rel=p["c_rel_bias"][l].astype(F32),
        wb=p["w_branch"][l].astype(BF), wo=p["w_out"][l].astype(BF),
        wu=p["w_up"][l].astype(BF), wd=p["w_down"][l].astype(BF),
    )


def _rope_tables(pos):
    half = A_ROPE // 2
    inv = ROPE_THETA ** (-jnp.arange(half, dtype=F32) / half)
    ang = pos.astype(F32)[:, None] * inv[None, :]
    cos, sin = jnp.cos(ang), jnp.sin(ang)
    z = jnp.zeros((pos.shape[0], LANES - A_ROPE), F32)
    return jnp.concatenate([cos, cos, z], axis=1), jnp.concatenate([-sin, sin, z], axis=1)


def _heads_to_rows(x, lanes):
    b, t, h = x.shape
    return jnp.pad(jnp.swapaxes(x, 1, 2), ((0, 0), (0, 8 - h), (0, lanes - t)))


def _front(x2d, mods, pk, tabs, tm):
    sh1, sc1 = mods[0], mods[1]
    cos_t, sin_t = tabs
    hb, cqn, ckvn, kpe, kpep, logf = _proj0(
        x2d, sc1, sh1, pk["ng1"], pk["w_all"], pk["gq"], pk["gkv"], pk["gkpe"], pk["fb"], cos_t, sin_t, tm)
    qb, kb, vb = _projg(hb, pk["w_all"], pk["gb"], 2, B_DH, tm, "in_proj_b")
    qc, kc, vc = _projg(hb, pk["w_all"], pk["gc"], 5, C_DH, tm, "in_proj_c")
    qd, kd, vd = _projg(hb, pk["w_all"], pk["gd"], 8, D_DH, tm, "in_proj_d")
    qa, ka, va = _aprep(cqn, ckvn, kpep, pk["wuq"], pk["wukv"], pk["gqn"], pk["gqr"], pk["gkn"],
                        cos_t, sin_t, tm)
    return dict(h=hb, ckv=ckvn, kpe=kpe, logf=logf, qa=qa, ka=ka, va=va, qb=qb, kb=kb, vb=vb,
                qc=qc, kc=kc, vc=vc, qd=qd, kd=kd, vd=vd)


def _back(x2d, hb, outs, mods, pk, tm):
    g1, sh2, sc2, g2 = mods[2], mods[3], mods[4], mods[5]
    mix = _gmerge(hb, outs, pk["w_all"], pk["wb"], tm)
    x2d = _oproj(mix, x2d, g1, pk["wo"], min(tm, 512))
    return _ffn(x2d, sc2, sh2, g2, pk["ng2"], pk["wu"], pk["wd"], tm)


def _prompt_layer(x2d, mods, pk, tabs, bias_tiles, b, s, l):
    tm = min(1024, s)
    f = _front(x2d, mods, pk, tabs, tm)
    r3 = lambda a: a.reshape(b, s, a.shape[-1])
    o_a = _pattn("a", r3(f["qa"]), r3(f["ka"]), r3(f["va"]), ())
    o_b = _pattn("b", r3(f["qb"]), r3(f["kb"]), r3(f["vb"]), (pk["lam"], pk["sg"]), l)
    o_c = _pattn("c", r3(f["qc"]), r3(f["kc"]), r3(f["vc"]), (bias_tiles,))
    frow = _cumsum_rows(_heads_to_rows(r3(f["logf"]), s))
    fcol = jnp.swapaxes(frow[:, :H, :], 1, 2)
    o_d = _pattn("d", r3(f["qd"]), r3(f["kd"]), r3(f["vd"]), (fcol, frow))
    outs = [o.reshape(b * s, HW) for o in (o_a, o_b, o_c, o_d)]
    x2d = _back(x2d, f["h"], outs, mods, pk, tm)
    keep = min(BAND_ROWS, s)
    state = (r3(f["ckv"]), r3(f["kpe"]),
             f["kb"].reshape(b, s, H, 2 * B_DH), f["vb"].reshape(b, s, H, 2 * B_DH),
             f["kc"].reshape(b, s, H, C_DH)[:, s - keep:], f["vc"].reshape(b, s, H, C_DH)[:, s - keep:],
             f["kd"].reshape(b, s, H, D_DH), f["vd"].reshape(b, s, H, D_DH), r3(f["logf"]))
    return x2d, state


def _sample_layer(x2d, mods, pk, tabs, bias_tiles, caches, nb, t, l):
    lat_c, kpe_t, kb_c, vb_c, kc_c, vc_c, kd_c, vd_c, logf_c = caches
    past = lat_c.shape[2]
    tm = nb * t
    f = _front(x2d, mods, pk, tabs, tm)
    o_a = _sattn("a", l, f["qa"], lat_c, kpe_t, f["ka"], f["va"], (pk["wukv"], pk["gkn"]), t, past)
    o_b = _sattn("b", l, f["qb"], kb_c, vb_c, f["kb"], f["vb"], (pk["lam"], pk["sg"]), t, past)
    o_c = _sattn("c", l, f["qc"], kc_c, vc_c, f["kc"], f["vc"], (bias_tiles,), t, past)
    lfc = _heads_to_rows(logf_c[l].astype(F32), past)
    lfn = _heads_to_rows(f["logf"].reshape(nb, t, H), LANES)
    o_d = _sattn("d", l, f["qd"], kd_c, vd_c, f["kd"], f["vd"], (lfc, lfn), t, past)
    x2d = _back(x2d, f["h"], [o_a, o_b, o_c, o_d], mods, pk, tm)
    r3 = lambda a: a.reshape(nb, t, a.shape[-1])
    r4 = lambda a, dh: a.reshape(nb, t, H, dh)
    state = (r3(f["ckv"]), r3(f["kpe"]), r4(f["kb"], 2 * B_DH), r4(f["vb"], 2 * B_DH),
             r4(f["kc"], C_DH), r4(f["vc"], C_DH), r4(f["kd"], D_DH), r4(f["vd"], D_DH), r3(f["logf"]))
    return x2d, state


def kernel(x_prompt, x_sample, c_prompt, c_sample,
           cache_a_latent, cache_a_kpe, cache_b_k, cache_b_v, cache_c_k, cache_c_v,
           cache_d_k, cache_d_v, cache_d_logf,
           norm1_g, norm2_g, w_ada, b_ada, w_in,
           a_q_norm_g, a_kv_norm_g, w_a_uq, w_a_ukv, a_q_gain, a_k_gain,
           b_q_gain, b_k_gain, b_lambda, b_subln_g,
           c_q_gain, c_k_gain, c_rel_bias,
           d_q_gain, d_k_gain, d_forget_b,
           w_branch, w_out, w_up, w_down):
    params = dict(norm1_g=norm1_g, norm2_g=norm2_g, w_in=w_in, a_q_norm_g=a_q_norm_g,
                  a_kv_norm_g=a_kv_norm_g, w_a_uq=w_a_uq, w_a_ukv=w_a_ukv, a_q_gain=a_q_gain,
                  a_k_gain=a_k_gain, b_q_gain=b_q_gain, b_k_gain=b_k_gain, b_lambda=b_lambda,
                  b_subln_g=b_subln_g, c_q_gain=c_q_gain, c_k_gain=c_k_gain, c_rel_bias=c_rel_bias,
                  d_q_gain=d_q_gain, d_k_gain=d_k_gain, d_forget_b=d_forget_b,
                  w_branch=w_branch, w_out=w_out, w_up=w_up, w_down=w_down)
    depth = w_in.shape[0]
    b, s, d = x_prompt.shape
    nb, t, _ = x_sample.shape
    past = cache_a_latent.shape[2]
    rows_c = cache_c_k.shape[2]

    n_c = b + nb
    c_all = jnp.pad(jnp.concatenate([c_prompt, c_sample], axis=0), ((0, (-n_c) % 8), (0, 0)))
    mod_all = _ada(c_all, w_ada, b_ada)

    tabs_p = _rope_tables(jnp.arange(s, dtype=jnp.int32))
    tabs_s = _rope_tables(jnp.tile(past + jnp.arange(t, dtype=jnp.int32), nb))

    tile = 256
    p_bases = [(BAND_ROWS // tile - k) * tile for k in range(BAND_ROWS // tile + 1)]

    x_p = x_prompt.reshape(b * s, d)
    x_s = x_sample.reshape(nb * t, d)
    w_in_t = jnp.transpose(w_in, (2, 0, 1))
    caches = (cache_a_latent, jnp.swapaxes(cache_a_kpe, 2, 3), cache_b_k, cache_b_v, cache_c_k,
              cache_c_v, cache_d_k, cache_d_v, cache_d_logf)
    states_p, states_s = [], []
    for l in range(depth):
        pk = _pack_layer(l, params, w_in_t)
        mod = mod_all[l]
        mods_p = [m.reshape(b, 1, d) for m in jnp.split(mod[:b], 6, axis=-1)]
        mods_s = [jnp.repeat(m, t, axis=0).reshape(1, nb * t, d) for m in jnp.split(mod[b:n_c], 6, axis=-1)]
        bias_p = _relbias(pk["rel"], p_bases, tile, tile, -(CHUNK - 1), LOG2E)
        bias_s = jnp.concatenate(
            [_relbias(pk["rel"], [rows_c], t, rows_c, -REL_CLIP)[:, 0],
             _relbias(pk["rel"], [0], t, LANES, -REL_CLIP)[:, 0]], axis=-1)
        x_p, st_p = _prompt_layer(x_p, mods_p, pk, tabs_p, bias_p, b, s, l)
        x_s, st_s = _sample_layer(x_s, mods_s, pk, tabs_s, bias_s, caches, nb, t, l)
        states_p.append(st_p)
        states_s.append(st_s)
    sp = [jnp.stack(z) for z in zip(*states_p)]
    ss = [jnp.stack(z) for z in zip(*states_s)]
    out = [x_p.reshape(b, s, d), x_s.reshape(nb, t, d)]
    for a, c in zip(sp, ss):
        out += [a, c]
    return tuple(out)
```

```python
import functools
import math

import jax
import jax.numpy as jnp
from jax import lax
from jax.experimental import pallas as pl
from jax.experimental.pallas import tpu as pltpu

BF = jnp.bfloat16
F32 = jnp.float32

CHUNK = 64
EPS = 1e-6
NEG_INF = -1e30
H = 4
A_NOPE, A_ROPE, A_VDIM = 128, 64, 128
A_Q_RANK, A_KV_RANK = 512, 256
A_QK_PAD = 256
ROPE_THETA = 10000.0
B_DH = 64
C_DH = 128
BAND_CHUNKS = 8
BAND_ROWS = BAND_CHUNKS * CHUNK
REL_CLIP = 128
D_DH = 128
HW = 512
N_BRANCH = 4
LANES = 128
PROJ_TILE = 512

VMEM_LIMIT_BYTES = 56 * 1024 * 1024


def _cparams(*sem):
    return pltpu.CompilerParams(dimension_semantics=sem, vmem_limit_bytes=VMEM_LIMIT_BYTES)


def _nt_dot(a, b):
    return lax.dot_general(a, b, (((1,), (1,)), ((), ())), preferred_element_type=F32)


def _dot(a, b):
    return jnp.dot(a, b, preferred_element_type=F32)


def _rms(z, n):
    ms = jnp.sum(z * z, axis=-1, keepdims=True) * (1.0 / n)
    return z * lax.rsqrt(ms + EPS)


def _rms_groups(z, gs):
    w = z.shape[-1]
    if gs >= LANES:
        parts = [_rms(z[:, g * gs:(g + 1) * gs], gs) for g in range(w // gs)]
        return parts[0] if len(parts) == 1 else jnp.concatenate(parts, axis=-1)
    assert gs * 2 == LANES
    lo = lax.broadcasted_iota(jnp.int32, (1, LANES), 1) < gs
    parts = []
    for g in range(w // LANES):
        zz = z[:, g * LANES:(g + 1) * LANES]
        sq = zz * zz
        s_lo = jnp.sum(jnp.where(lo, sq, 0.0), axis=-1, keepdims=True)
        s_hi = jnp.sum(jnp.where(lo, 0.0, sq), axis=-1, keepdims=True)
        ms = jnp.where(lo, s_lo, s_hi) * (1.0 / gs)
        parts.append(zz * lax.rsqrt(ms + EPS))
    return jnp.concatenate(parts, axis=-1)


def _rope128(r, cos, sin):
    half = A_ROPE // 2
    lane = lax.broadcasted_iota(jnp.int32, (1, LANES), 1)
    swapped = jnp.where(lane < half, pltpu.roll(r, LANES - half, 1), pltpu.roll(r, half, 1))
    return r * cos + swapped * sin


CHUNK_SHIFT = CHUNK.bit_length() - 1
assert 1 << CHUNK_SHIFT == CHUNK


def _chunk(pos):
    return jnp.right_shift(pos, CHUNK_SHIFT)


def _log_sigmoid(x):
    return jnp.minimum(x, 0.0) - jnp.log1p(jnp.exp(-jnp.abs(x)))


def _sigmoid(x):
    return 1.0 / (1.0 + jnp.exp(-x))


def _ada_kernel(c_ref, w_ref, b_ref, o_ref):
    c = c_ref[...]
    a = (c * _sigmoid(c)).astype(BF)
    o_ref[...] = _dot(a, w_ref[...].astype(BF)) + b_ref[...]


def _ada(c_all, w_ada, b_ada):
    depth, d, n = w_ada.shape
    r = c_all.shape[0]
    tn = 1024
    return pl.pallas_call(
        _ada_kernel,
        out_shape=jax.ShapeDtypeStruct((depth, r, n), F32),
        grid=(depth, n // tn),
        in_specs=[
            pl.BlockSpec((r, d), lambda l, j: (0, 0)),
            pl.BlockSpec((None, d, tn), lambda l, j: (l, 0, j)),
            pl.BlockSpec((None, 1, tn), lambda l, j: (l, 0, j)),
        ],
        out_specs=pl.BlockSpec((None, r, tn), lambda l, j: (l, 0, j)),
        compiler_params=_cparams("arbitrary", "arbitrary"),
        name="ada_mod",
    )(c_all, w_ada, b_ada.reshape(depth, 1, n))


WIN_TILE = PROJ_TILE
WIN_PROJ_STARTS = (0, A_Q_RANK, 832, 1344, 1856, 2368, 2880, 3392, 3904, 4416, 4928, 5440)
WIN_GATE_START = 5444
PROJ_TILES = len(WIN_PROJ_STARTS)


def _winprep_kernel(off_ref, w_hbm, o_ref, buf, sem, *, layer, nsteps):
    s = pl.program_id(0)
    slot = s % 2

    def copy(step, slot_):
        return pltpu.make_async_copy(w_hbm.at[pl.ds(off_ref[step], WIN_TILE), layer, :],
                                     buf.at[slot_], sem.at[slot_])

    @pl.when(s == 0)
    def _():
        copy(0, 0).start()

    @pl.when(s + 1 < nsteps)
    def _():
        copy(s + 1, 1 - slot).start()

    copy(s, slot).wait()
    o_ref[...] = buf[slot].T.astype(BF)


def _win_prep(w_in_t, layer, d):
    gate_tiles = N_BRANCH * d // WIN_TILE
    starts = WIN_PROJ_STARTS + tuple(WIN_GATE_START + g * WIN_TILE for g in range(gate_tiles))
    assert starts[-1] + WIN_TILE == w_in_t.shape[0]
    nsteps = len(starts)
    rows = w_in_t.shape[2]
    return pl.pallas_call(
        functools.partial(_winprep_kernel, layer=layer, nsteps=nsteps),
        out_shape=jax.ShapeDtypeStruct((rows, nsteps * WIN_TILE), BF),
        grid_spec=pltpu.PrefetchScalarGridSpec(
            num_scalar_prefetch=1,
            grid=(nsteps,),
            in_specs=[pl.BlockSpec(memory_space=pl.ANY)],
            out_specs=pl.BlockSpec((rows, WIN_TILE), lambda s, off: (0, s)),
            scratch_shapes=[pltpu.VMEM((2, WIN_TILE, rows), F32), pltpu.SemaphoreType.DMA((2,))],
        ),
        compiler_params=_cparams("arbitrary"),
        name="w_in_repack",
    )(jnp.asarray(starts, jnp.int32), w_in_t)


PROJ_ROW_SPLIT = 2


def _row_parts(tm):
    part = tm // PROJ_ROW_SPLIT if tm % (PROJ_ROW_SPLIT * 16) == 0 else tm
    return [slice(r, r + part) for r in range(0, tm, part)]


def _mod_rows(ref, rows):
    return ref[...] if ref.shape[0] == 1 else ref[rows, :]


def _proj0_kernel(x_ref, sc_ref, sh_ref, ng_ref, w_ref, gq_ref, gkv_ref, gkpe_ref, fb_ref,
                  cos_ref, sin_ref, h_out, cq_out, ckv_out, kpe_out, kpep_out, logf_out):
    j = pl.program_id(1)
    parts = _row_parts(x_ref.shape[0])

    @pl.when(j == 0)
    def _():
        for rows in parts:
            x = x_ref[rows, :]
            y = _rms(x, x.shape[-1]) * ng_ref[...]
            h_out[rows, :] = (y * (1.0 + _mod_rows(sc_ref, rows)) + _mod_rows(sh_ref, rows)).astype(BF)

    def tile(epilogue):
        for rows in parts:
            epilogue(rows, _dot(h_out[rows, :], w_ref[...]))

    def cq_tile(rows, z):
        cq_out[rows, :] = (_rms(z, A_Q_RANK) * gq_ref[...]).astype(BF)

    def kv_tile(rows, z):
        ckv_out[rows, :] = _rms(z[:, :A_KV_RANK], A_KV_RANK) * gkv_ref[...]
        lane = lax.broadcasted_iota(jnp.int32, (1, LANES), 1)
        kp = jnp.where(lane < A_ROPE, z[:, A_KV_RANK:A_KV_RANK + LANES], 0.0)
        kp = _rope128(_rms(kp, A_ROPE) * gkpe_ref[...], cos_ref[rows, :], sin_ref[rows, :])
        kpe_out[rows, :] = kp[:, :A_ROPE]
        kpep_out[rows, :] = kp.astype(BF)

    def f_tile(rows, z):
        logf_out[rows, :] = _log_sigmoid(z[:, :LANES] + fb_ref[...])[:, :H]

    for jj, epilogue in enumerate((cq_tile, kv_tile, f_tile)):
        pl.when(j == jj)(functools.partial(tile, epilogue))


def _projg_kernel(*refs, gs, layer, aliased, keep, tiles_per_batch, n_tiles):
    if aliased:
        refs = refs[:3] + refs[5:]
    h_ref, w_ref, g_ref, q_out, k_out, v_out, ks_hbm, vs_hbm, kbuf, vbuf, sem = refs
    i = pl.program_id(0)
    j = pl.program_id(1)
    tm = h_ref.shape[0]
    parts = _row_parts(tm)
    dh = kbuf.shape[-1] // H

    def copies(tile_idx, buf, dst_hbm, s):
        if keep is None:
            src_rows, dst0 = slice(None), tile_idx * tm
        else:
            src_rows, dst0 = slice(tm - keep, tm), (tile_idx // tiles_per_batch) * keep
        n = tm if keep is None else keep
        layers = (layer,) if aliased else range(layer, dst_hbm.shape[0])
        return [pltpu.make_async_copy(buf.at[src_rows, pl.ds(hd * dh, dh)],
                                      dst_hbm.at[lyr, pl.ds(dst0, n), hd, :], s)
                for lyr in layers for hd in range(H)]

    def has_state(tile_idx):
        return True if keep is None else tile_idx % tiles_per_batch == tiles_per_batch - 1

    def tile(epilogue):
        for rows in parts:
            epilogue(rows, _dot(h_ref[rows, :], w_ref[...]))

    def q_tile(rows, z):
        q_out[rows, :] = (_rms_groups(z, gs) * g_ref[0:1, :]).astype(BF)

    def k_tile(rows, z):
        kn = _rms_groups(z, gs) * g_ref[1:2, :]
        kbuf[rows, :] = kn
        k_out[rows, :] = kn.astype(BF)

    def v_tile(rows, z):
        vbuf[rows, :] = z
        v_out[rows, :] = z.astype(BF)

    @pl.when(j == 0)
    def _():
        tile(q_tile)

        @pl.when((i > 0) & has_state(i - 1))
        def _():
            for c in copies(i - 1, vbuf, vs_hbm, sem.at[1]):
                c.wait()

    @pl.when(j == 1)
    def _():
        tile(k_tile)

        @pl.when(has_state(i))
        def _():
            for c in copies(i, kbuf, ks_hbm, sem.at[0]):
                c.start()

    @pl.when(j == 2)
    def _():
        tile(v_tile)

        @pl.when(has_state(i))
        def _():
            for c in copies(i, vbuf, vs_hbm, sem.at[1]):
                c.start()
            for c in copies(i, kbuf, ks_hbm, sem.at[0]):
                c.wait()

        @pl.when((i == n_tiles - 1) & has_state(i))
        def _():
            for c in copies(i, vbuf, vs_hbm, sem.at[1]):
                c.wait()


def _proj0(x2d, sc, sh, ng, w_all, gq, gkv, gkpe, fb, cos_t, sin_t, tm):
    m, d = x2d.shape
    groups, rows = sc.shape[0], sc.shape[1]
    tiles_per_group = (m // groups) // tm
    tab_blocks = cos_t.shape[0] // tm
    row = lambda i, j: (i, 0)
    const2 = lambda i, j: (0, 0)
    mod_spec = pl.BlockSpec((None, rows, d), lambda i, j: (i // tiles_per_group, 0, 0))
    tab_spec = pl.BlockSpec((tm, LANES), lambda i, j: (i % tab_blocks, 0))

    def out(width, dtype):
        return jax.ShapeDtypeStruct((m, width), dtype), pl.BlockSpec((tm, width), row)

    outs = [out(d, BF), out(A_Q_RANK, BF), out(A_KV_RANK, F32), out(A_ROPE, F32), out(LANES, BF),
            out(H, F32)]
    return pl.pallas_call(
        _proj0_kernel,
        out_shape=[o[0] for o in outs],
        grid=(m // tm, 3),
        in_specs=[
            pl.BlockSpec((tm, d), row),
            mod_spec, mod_spec,
            pl.BlockSpec((1, d), const2),
            pl.BlockSpec((d, PROJ_TILE), lambda i, j: (0, jnp.where(j == 2, PROJ_TILES - 1, j))),
            pl.BlockSpec((1, A_Q_RANK), const2),
            pl.BlockSpec((1, A_KV_RANK), const2),
            pl.BlockSpec((1, LANES), const2),
            pl.BlockSpec((1, LANES), const2),
            tab_spec, tab_spec,
        ],
        out_specs=[o[1] for o in outs],
        compiler_params=_cparams("arbitrary", "arbitrary"),
        name="in_proj_a",
    )(x2d, sc, sh, ng, w_all, gq, gkv, gkpe, fb, cos_t, sin_t)


def _projg(hb, w_all, gains, first_tile, gs, tm, name, layer, depth, prev_states, keep, rows_per_batch):
    m, d = hb.shape
    n_tiles = m // tm
    if keep is None:
        state_rows, tiles_per_batch = m, 1
    else:
        assert rows_per_batch % tm == 0 and keep <= tm
        state_rows, tiles_per_batch = (m // rows_per_batch) * keep, rows_per_batch // tm
    state = jax.ShapeDtypeStruct((depth, state_rows, H, HW // H), F32)
    aliased = prev_states is not None
    row = lambda i, j: (i, 0)
    hbm = pl.BlockSpec(memory_space=pl.ANY)
    return pl.pallas_call(
        functools.partial(_projg_kernel, gs=gs, layer=layer, aliased=aliased, keep=keep,
                          tiles_per_batch=tiles_per_batch, n_tiles=n_tiles),
        out_shape=[jax.ShapeDtypeStruct((m, HW), BF)] * 3 + [state, state],
        grid=(n_tiles, 3),
        in_specs=[pl.BlockSpec((tm, d), row),
                  pl.BlockSpec((d, PROJ_TILE), lambda i, j: (0, first_tile + j)),
                  pl.BlockSpec((2, HW), lambda i, j: (0, 0))] + ([hbm, hbm] if aliased else []),
        out_specs=[pl.BlockSpec((tm, HW), row)] * 3 + [hbm, hbm],
        scratch_shapes=[pltpu.VMEM((tm, HW), F32), pltpu.VMEM((tm, HW), F32), pltpu.SemaphoreType.DMA((2,))],
        input_output_aliases={3: 3, 4: 4} if aliased else {},
        compiler_params=_cparams("arbitrary", "arbitrary"),
        name=name,
    )(hb, w_all, gains, *(prev_states if aliased else ()))


def _aprep_kernel(cq_ref, ckv_ref, kpep_ref, wuq_ref, wukv_ref, gqn_ref, gqr_ref, gkn_ref,
                  cos_ref, sin_ref, q_out, k_out, v_out):
    zq = _dot(cq_ref[...], wuq_ref[...])
    cos, sin = cos_ref[...], sin_ref[...]
    parts = []
    for h in range(H):
        nope = zq[:, h * A_QK_PAD:h * A_QK_PAD + A_NOPE]
        rp = zq[:, h * A_QK_PAD + A_NOPE:(h + 1) * A_QK_PAD]
        parts.append(_rms(nope, A_NOPE) * gqn_ref[...])
        parts.append(_rope128(_rms(rp, A_ROPE) * gqr_ref[...], cos, sin))
    q_out[...] = jnp.concatenate(parts, axis=-1).astype(BF)

    zkv = _dot(ckv_ref[...].astype(BF), wukv_ref[...])
    kp = kpep_ref[...]
    parts = []
    for h in range(H):
        kn = _rms(zkv[:, h * A_NOPE:(h + 1) * A_NOPE], A_NOPE) * gkn_ref[...]
        parts.append(kn.astype(BF))
        parts.append(kp)
    k_out[...] = jnp.concatenate(parts, axis=-1)
    v_out[...] = zkv[:, H * A_NOPE:].astype(BF)


def _aprep(cqn, ckvn, kpep, wuq, wukv, gqn, gqr, gkn, cos_t, sin_t, tm):
    m = cqn.shape[0]
    tab_blocks = cos_t.shape[0] // tm
    row = lambda i: (i, 0)
    const = lambda i: (0, 0)
    tab_spec = pl.BlockSpec((tm, LANES), lambda i: (i % tab_blocks, 0))
    return pl.pallas_call(
        _aprep_kernel,
        out_shape=[jax.ShapeDtypeStruct((m, H * A_QK_PAD), BF),
                   jax.ShapeDtypeStruct((m, H * A_QK_PAD), BF),
                   jax.ShapeDtypeStruct((m, HW), BF)],
        grid=(m // tm,),
        in_specs=[
            pl.BlockSpec((tm, A_Q_RANK), row),
            pl.BlockSpec((tm, A_KV_RANK), row),
            pl.BlockSpec((tm, LANES), row),
            pl.BlockSpec(wuq.shape, const),
            pl.BlockSpec(wukv.shape, const),
            pl.BlockSpec((1, LANES), const),
            pl.BlockSpec((1, LANES), const),
            pl.BlockSpec((1, LANES), const),
            tab_spec, tab_spec,
        ],
        out_specs=[pl.BlockSpec((tm, H * A_QK_PAD), row),
                   pl.BlockSpec((tm, H * A_QK_PAD), row),
                   pl.BlockSpec((tm, HW), row)],
        compiler_params=_cparams("arbitrary"),
        name="a_prep",
    )(cqn, ckvn, kpep, wuq, wukv, gqn, gqr, gkn, cos_t, sin_t)


CUM_BLOCK = 256


def _tri_upper(n):
    r = lax.broadcasted_iota(jnp.int32, (n, n), 0)
    c = lax.broadcasted_iota(jnp.int32, (n, n), 1)
    return jnp.where(r <= c, 1.0, 0.0).astype(BF)


def _dot3(x, u):
    hi = x.astype(BF)
    r1 = x - hi.astype(F32)
    mid = r1.astype(BF)
    lo = (r1 - mid.astype(F32)).astype(BF)
    return _dot(hi, u) + _dot(mid, u) + _dot(lo, u)


def _cumsum_lanes(src_ref, dst_ref):
    u = _tri_upper(CUM_BLOCK)
    carry = jnp.zeros((8, 1), F32)
    for b in range(src_ref.shape[-1] // CUM_BLOCK):
        blk = slice(b * CUM_BLOCK, (b + 1) * CUM_BLOCK)
        c = _dot3(src_ref[:, blk], u) + carry
        dst_ref[:, blk] = c
        carry = c[:, CUM_BLOCK - 1:CUM_BLOCK]
    return carry


def _cumsum_kernel(x_ref, o_ref):
    _cumsum_lanes(x_ref, o_ref)


def _cumsum_rows(x):
    b, r, s = x.shape
    return pl.pallas_call(
        _cumsum_kernel,
        out_shape=jax.ShapeDtypeStruct(x.shape, F32),
        grid=(b,),
        in_specs=[pl.BlockSpec((None, r, s), lambda i: (i, 0, 0))],
        out_specs=pl.BlockSpec((None, r, s), lambda i: (i, 0, 0)),
        compiler_params=_cparams("arbitrary"),
        name="forget_cumsum",
    )(x)


def _relbias_kernel(tab_ref, o_ref, *, bases, rows, cols, dmin, mult):
    h = pl.program_id(0)
    r = lax.broadcasted_iota(jnp.int32, (rows, cols), 0)
    c = lax.broadcasted_iota(jnp.int32, (rows, cols), 1)
    for t, base in enumerate(bases):
        idx = jnp.clip(base + r - c, -REL_CLIP, REL_CLIP) + REL_CLIP
        lo = min(max(base - (cols - 1), max(dmin, -REL_CLIP)), REL_CLIP) + REL_CLIP
        hi = min(max(base + rows - 1, -REL_CLIP), REL_CLIP) + REL_CLIP

        def body(e, acc):
            return jnp.where(idx == e, tab_ref[h, e] * mult, acc)

        o_ref[t] = lax.fori_loop(lo, hi + 1, body, jnp.zeros((rows, cols), F32))


def _relbias(table, bases, rows, cols, dmin, mult=1.0):
    nh = table.shape[0]
    return pl.pallas_call(
        functools.partial(_relbias_kernel, bases=tuple(bases), rows=rows, cols=cols, dmin=dmin, mult=mult),
        out_shape=jax.ShapeDtypeStruct((nh, len(bases), rows, cols), F32),
        grid=(nh,),
        in_specs=[pl.BlockSpec(memory_space=pltpu.SMEM)],
        out_specs=pl.BlockSpec((None, len(bases), rows, cols), lambda h: (h, 0, 0, 0)),
        compiler_params=_cparams("arbitrary"),
        name="rel_bias_tiles",
    )(table)


def _diff_lambda(lam_ref, layer_idx):
    lam_init = 0.8 - 0.6 * math.exp(-0.3 * layer_idx)
    lp = lam_ref[...]
    a = jnp.sum(lp[0:1] * lp[1:2], keepdims=True)
    b = jnp.sum(lp[2:3] * lp[3:4], keepdims=True)
    return jnp.exp(a) - jnp.exp(b) + lam_init, lam_init


ALIBI_SLOPES = tuple(2.0 ** (-8.0 * (h + 1) / H) for h in range(H))


LOG2E = math.log2(math.e)
RG = 32


def _pattn_kernel(*refs, mode, tq, tk, nk, layer_idx, back):
    if mode == "a":
        q_ref, k_ref, v_ref, o_ref, m_ref, acc_ref, s_scr, p_scr, pm_scr = refs
        dq, scale = A_QK_PAD, (A_NOPE + A_ROPE) ** -0.5
    elif mode == "b":
        q_ref, k_ref, v_ref, lam_ref, sg_ref, o_ref, m_ref, acc_ref, s_scr, p_scr, pm_scr = refs
        dq, scale = 2 * B_DH, B_DH ** -0.5
    elif mode == "c":
        q_ref, k_ref, v_ref, bias_ref, o_ref, m_ref, acc_ref, s_scr, p_scr, pm_scr = refs
        dq, scale = C_DH, C_DH ** -0.5
    else:
        q_ref, k_ref, v_ref, fq_ref, fk_ref, o_ref, m_ref, acc_ref, s_scr, p_scr, pm_scr, fq_scr = refs
        dq, scale = D_DH, D_DH ** -0.5
    assert tq == tk and CHUNK % RG == 0 and tk % LANES == 0
    dv = HW // H
    c1 = scale * LOG2E
    nchunk = tk // LANES
    i = pl.program_id(1)
    j = pl.program_id(2)
    if mode == "c":
        kb = i - back + j
        valid = kb >= 0
        first = jnp.maximum(back - i, 0)
        last = nk - 1
    else:
        last = i
        kb = j
        valid = j <= last
        first = 0

    @pl.when(j == first)
    def _():
        m_ref[...] = jnp.full(m_ref.shape, -jnp.inf, F32)
        acc_ref[...] = jnp.zeros(acc_ref.shape, F32)
        if mode == "d":
            for h in range(H):
                fq_scr[h] = jnp.broadcast_to(fq_ref[:, h:h + 1] * LOG2E, (tq, LANES))

    lane = lax.broadcasted_iota(jnp.int32, (RG, LANES), 1)
    lane_minus_row = lane - lax.broadcasted_iota(jnp.int32, (RG, LANES), 0)

    def chunk_visibility(kind, r0, c):
        c0 = c * LANES
        if kind == "none":
            return "all"
        if kind == "lt":
            bound = (r0 // CHUNK + 1) * CHUNK
            if c0 >= bound:
                return "none"
            return "all" if c0 + LANES <= bound else lane < bound - c0
        if kind == "ge":
            bound = r0 // CHUNK * CHUNK
            if c0 + LANES <= bound:
                return "none"
            return "all" if c0 >= bound else lane >= bound - c0
        assert kind == "causal"
        if c0 > r0 + RG - 1:
            return "none"
        return "all" if c0 + LANES - 1 <= r0 else lane_minus_row <= r0 - c0

    def sweep1(idxs, h, g, kind, delta):
        r0 = g * RG
        rows = slice(r0, r0 + RG)
        pm = [None] * len(idxs)
        for c in range(nchunk):
            vis = chunk_visibility(kind, r0, c)
            if isinstance(vis, str) and vis == "none":
                continue
            cols = slice(c * LANES, (c + 1) * LANES)
            if mode == "b":
                dist = jnp.abs(lane_minus_row.astype(F32) - (delta + float(r0 - c * LANES)))
                bias = dist * (-ALIBI_SLOPES[h] * LOG2E)
            elif mode == "c":
                bias = bias_ref[h, rows, cols]
            elif mode == "d":
                bias = fq_scr[h, rows, :] - fk_ref[h:h + 1, cols] * LOG2E
            for n, idx in enumerate(idxs):
                t = s_scr[idx, rows, cols] * c1
                if mode != "a":
                    t = t + bias
                if not isinstance(vis, str):
                    t = jnp.where(vis, t, NEG_INF)
                s_scr[idx, rows, cols] = t
                pm[n] = t if pm[n] is None else jnp.maximum(pm[n], t)
        for n, idx in enumerate(idxs):
            pm_scr[idx, rows, :] = pm[n]

    def sweep2(idx, g, kind):
        r0 = g * RG
        rows = slice(r0, r0 + RG)
        m_rows = m_ref[idx, rows, :]
        for c in range(nchunk):
            cols = slice(c * LANES, (c + 1) * LANES)
            vis = chunk_visibility(kind, r0, c)
            if isinstance(vis, str) and vis == "none":
                p_scr[idx, rows, cols] = jnp.zeros((RG, LANES), BF)
            else:
                p_scr[idx, rows, cols] = jnp.exp2(s_scr[idx, rows, cols] - m_rows).astype(BF)

    def block(kind):
        delta = ((i - kb) * tq).astype(F32) if mode == "b" else None
        ones = jnp.ones((tk, LANES), BF)
        for h in range(H):
            q = q_ref[:, h * dq:(h + 1) * dq]
            k = k_ref[:, h * dq:(h + 1) * dq].astype(BF)
            v1 = jnp.concatenate([v_ref[:, h * dv:(h + 1) * dv].astype(BF), ones], axis=-1)
            if mode == "b":
                lo = lax.broadcasted_iota(jnp.int32, (1, dq), 1) < B_DH
                zero = jnp.zeros_like(q)
                s_scr[h] = _nt_dot(jnp.where(lo, q, zero), k)
                s_scr[H + h] = _nt_dot(jnp.where(lo, zero, q), k)
                idxs = (h, H + h)
            else:
                s_scr[h] = _nt_dot(q, k)
                idxs = (h,)
            for g in range(tq // RG):
                sweep1(idxs, h, g, kind, delta)
            for idx in idxs:
                m_prev = m_ref[idx]
                m_new = jnp.maximum(m_prev, jnp.max(pm_scr[idx], axis=-1, keepdims=True))
                m_ref[idx] = m_new
                alpha = jnp.exp2(m_prev - m_new)
                for g in range(tq // RG):
                    sweep2(idx, g, kind)
                acc_ref[idx] = jnp.concatenate([alpha, alpha], axis=-1) * acc_ref[idx] + _dot(p_scr[idx], v1)

    if mode == "c":
        assert back == 2 and BAND_ROWS == back * tk
        for jj, kind in enumerate(("ge", "none", "lt")):
            pl.when((j == jj) & valid)(functools.partial(block, kind))
    else:
        pl.when(j < last)(functools.partial(block, "none"))
        pl.when(j == last)(functools.partial(block, "causal" if mode == "d" else "lt"))

    @pl.when(j == last)
    def _():
        for h in range(H):
            o = acc_ref[h, :, :dv] / acc_ref[h, :, dv:]
            if mode == "b":
                lam, lam_init = _diff_lambda(lam_ref, layer_idx)
                o = o - lam * (acc_ref[H + h, :, :dv] / acc_ref[H + h, :, dv:])
                o = (_rms(o, dv) * sg_ref[...]) * (1.0 - lam_init)
            o_ref[:, h * dv:(h + 1) * dv] = o.astype(BF)


def _pattn(mode, q, k, v, extras, layer_idx=0):
    b, s, qw = q.shape
    kw = k.shape[-1]
    back = 0
    if mode == "c":
        tq = tk = 256
        back = BAND_ROWS // tk
        nk = back + 1
        kmap = lambda bi, i, j: (bi, jnp.maximum(i - back + j, 0), 0)
    else:
        tq = tk = 512
        nk = s // tk
        kmap = lambda bi, i, j: (bi, jnp.minimum(j, ((i + 1) * tq - 1) // tk), 0)
    qmap = lambda bi, i, j: (bi, i, 0)
    in_specs = [pl.BlockSpec((None, tq, qw), qmap),
                pl.BlockSpec((None, tk, kw), kmap),
                pl.BlockSpec((None, tk, HW), kmap)]
    if mode == "b":
        lam, sg = extras
        in_specs += [pl.BlockSpec(lam.shape, lambda bi, i, j: (0, 0)),
                     pl.BlockSpec(sg.shape, lambda bi, i, j: (0, 0))]
    elif mode == "c":
        (bias,) = extras
        in_specs += [pl.BlockSpec((H, None, tq, tk), lambda bi, i, j: (0, j, 0, 0))]
    elif mode == "d":
        fcol, frow = extras
        in_specs += [pl.BlockSpec((None, tq, H), qmap),
                     pl.BlockSpec((None, 8, tk), lambda bi, i, j: (bi, 0, kmap(bi, i, j)[1]))]
    nstate = 2 * H if mode == "b" else H
    return pl.pallas_call(
        functools.partial(_pattn_kernel, mode=mode, tq=tq, tk=tk, nk=nk, layer_idx=layer_idx, back=back),
        out_shape=jax.ShapeDtypeStruct((b, s, HW), BF),
        grid=(b, s // tq, nk),
        in_specs=in_specs,
        out_specs=pl.BlockSpec((None, tq, HW), qmap),
        scratch_shapes=[pltpu.VMEM((nstate, tq, LANES), F32),
                        pltpu.VMEM((nstate, tq, 2 * (HW // H)), F32),
                        pltpu.VMEM((nstate, tq, tk), F32),
                        pltpu.VMEM((nstate, tq, tk), BF),
                        pltpu.VMEM((nstate, tq, LANES), F32)]
                       + ([pltpu.VMEM((H, tq, LANES), F32)] if mode == "d" else []),
        compiler_params=_cparams("arbitrary", "arbitrary", "arbitrary"),
        name="prompt_attn_" + mode,
    )(q, k, v, *extras)


def _pad_rows(x, rows):
    return jnp.concatenate([x, jnp.zeros((rows - x.shape[0], x.shape[1]), x.dtype)], axis=0)


def _prefetch_head_caches(k_hbm, v_hbm, kbuf, vbuf, sem, layer, nb):
    b = pl.program_id(0)
    slot = b % 2
    dh = kbuf.shape[-1] // H

    def copies(bidx, slot_):
        out = []
        for h in range(H):
            cols = pl.ds(h * dh, dh)
            out.append(pltpu.make_async_copy(k_hbm.at[layer, bidx, :, h, :], kbuf.at[slot_, :, cols],
                                             sem.at[slot_, 0]))
            out.append(pltpu.make_async_copy(v_hbm.at[layer, bidx, :, h, :], vbuf.at[slot_, :, cols],
                                             sem.at[slot_, 1]))
        return out

    @pl.when(b == 0)
    def _():
        for c in copies(0, 0):
            c.start()

    @pl.when(b + 1 < nb)
    def _():
        for c in copies(b + 1, 1 - slot):
            c.start()

    for c in copies(b, slot):
        c.wait()
    return slot


def _sattn_kernel(*refs, mode, t, past, layer, nb):
    if mode == "a":
        (q_ref, lat_ref, kpet_ref, kn_ref, vn_ref, wukv_ref, gkn_ref, o_ref) = refs
        dq, scale = A_QK_PAD, (A_NOPE + A_ROPE) ** -0.5
        rows_c = lat_ref.shape[0]
    else:
        if mode == "b":
            (q_ref, k_hbm, v_hbm, kn_ref, vn_ref, lam_ref, sg_ref, o_ref, kbuf, vbuf, sem) = refs
            dq, scale = 2 * B_DH, B_DH ** -0.5
        elif mode == "c":
            (q_ref, k_hbm, v_hbm, kn_ref, vn_ref, bias_ref, o_ref, kbuf, vbuf, sem) = refs
            dq, scale = C_DH, C_DH ** -0.5
        else:
            (q_ref, k_hbm, v_hbm, kn_ref, vn_ref, lfc_ref, lfn_ref, o_ref, f_scr, kbuf, vbuf, sem) = refs
            dq, scale = D_DH, D_DH ** -0.5
        rows_c = kbuf.shape[1]
        slot = _prefetch_head_caches(k_hbm, v_hbm, kbuf, vbuf, sem, layer, nb)
        kc_ref, vc_ref = kbuf.at[slot], vbuf.at[slot]
    dv = HW // H
    npad = LANES
    qpos_c = past + lax.broadcasted_iota(jnp.int32, (t, rows_c), 0)
    kpos_c = (past - rows_c) + lax.broadcasted_iota(jnp.int32, (t, rows_c), 1)
    qpos_n = past + lax.broadcasted_iota(jnp.int32, (t, npad), 0)
    col_n = lax.broadcasted_iota(jnp.int32, (t, npad), 1)
    kpos_n = past + col_n
    real_n = col_n < t
    if mode in ("a", "b"):
        mask_c = _chunk(kpos_c) <= _chunk(qpos_c)
        mask_n = real_n & (_chunk(kpos_n) <= _chunk(qpos_n))
    elif mode == "c":
        qc_c, kc_c = _chunk(qpos_c), _chunk(kpos_c)
        qc_n, kc_n = _chunk(qpos_n), _chunk(kpos_n)
        mask_c = (kpos_c >= 0) & (kc_c <= qc_c) & (kc_c >= qc_c - BAND_CHUNKS)
        mask_n = real_n & (kc_n <= qc_n) & (kc_n >= qc_n - BAND_CHUNKS)
    else:
        mask_c = kpos_c <= qpos_c
        mask_n = real_n & (kpos_n <= qpos_n)

    if mode == "a":
        lat = lat_ref[...].astype(BF)
        kpe_t = kpet_ref[...]
        kpe_t = jnp.concatenate([kpe_t, jnp.zeros_like(kpe_t)], axis=0).astype(BF)
    if mode == "b":
        dist_c = jnp.abs(qpos_c - kpos_c).astype(F32)
        dist_n = jnp.abs(qpos_n - kpos_n).astype(F32)
        lo = lax.broadcasted_iota(jnp.int32, (1, dq), 1) < B_DH
        lam, lam_init = _diff_lambda(lam_ref, layer)
    if mode == "d":
        carry = _cumsum_lanes(lfc_ref, f_scr)
        f_new = _dot3(lfn_ref[...], _tri_upper(LANES)) + carry
        eye = (lax.broadcasted_iota(jnp.int32, (t, npad), 0) == col_n)

    def softmax_pv(s_c, s_n, v_c, v_n):
        m = jnp.maximum(jnp.max(s_c, axis=-1, keepdims=True), jnp.max(s_n, axis=-1, keepdims=True))
        p_c = jnp.exp(s_c - m)
        p_n = jnp.exp(s_n - m)
        l = jnp.sum(p_c, axis=-1, keepdims=True) + jnp.sum(p_n, axis=-1, keepdims=True)
        return (_dot(p_c.astype(BF), v_c) + _dot(p_n.astype(BF), v_n)) / l

    for h in range(H):
        q = q_ref[:, h * dq:(h + 1) * dq]
        if mode == "a":
            kn = _pad_rows(kn_ref[:, h * dq:(h + 1) * dq], npad)
            vn = _pad_rows(vn_ref[:, h * dv:(h + 1) * dv], npad)
            w = wukv_ref[...]
            k_nope = _dot(lat, w[:, h * A_NOPE:(h + 1) * A_NOPE])
            k_nope = (_rms(k_nope, A_NOPE) * gkn_ref[...]).astype(BF)
            v_c = _dot(lat, w[:, H * A_NOPE + h * A_VDIM:H * A_NOPE + (h + 1) * A_VDIM]).astype(BF)
            s_c = (_nt_dot(q[:, :A_NOPE], k_nope) + _dot(q[:, A_NOPE:], kpe_t)) * scale
            s_n = _nt_dot(q, kn) * scale
        else:
            k_c = kc_ref[:, h * dq:(h + 1) * dq].astype(BF)
            v_c = vc_ref[:, h * dv:(h + 1) * dv].astype(BF)
            kn = _pad_rows(kn_ref[:, h * dq:(h + 1) * dq], npad).astype(BF)
            vn = _pad_rows(vn_ref[:, h * dv:(h + 1) * dv], npad).astype(BF)
        if mode == "b":
            zero = jnp.zeros_like(q)
            q1, q2 = jnp.where(lo, q, zero), jnp.where(lo, zero, q)
            bias_c = -ALIBI_SLOPES[h] * dist_c
            bias_n = -ALIBI_SLOPES[h] * dist_n
            outs = []
            for qq in (q1, q2):
                s_c = jnp.where(mask_c, _nt_dot(qq, k_c) * scale + bias_c, NEG_INF)
                s_n = jnp.where(mask_n, _nt_dot(qq, kn) * scale + bias_n, NEG_INF)
                outs.append(softmax_pv(s_c, s_n, v_c, vn))
            o = outs[0] - lam * outs[1]
            o = (_rms(o, dv) * sg_ref[...]) * (1.0 - lam_init)
        else:
            if mode != "a":
                s_c = _nt_dot(q, k_c) * scale
                s_n = _nt_dot(q, kn) * scale
            if mode == "c":
                s_c = s_c + bias_ref[h, :, :rows_c]
                s_n = s_n + bias_ref[h, :, rows_c:]
            elif mode == "d":
                fq = jnp.sum(jnp.where(eye, f_new[h:h + 1, :], 0.0), axis=-1, keepdims=True)
                s_c = s_c + (fq - f_scr[h:h + 1, :])
                s_n = s_n + (fq - f_new[h:h + 1, :])
            s_c = jnp.where(mask_c, s_c, NEG_INF)
            s_n = jnp.where(mask_n, s_n, NEG_INF)
            o = softmax_pv(s_c, s_n, v_c, vn)
        o_ref[:, h * dv:(h + 1) * dv] = o.astype(BF)


def _sattn(mode, layer, q, cache_k, cache_v, kn, vn, extras, t, past):
    nb = cache_k.shape[1]
    qw = q.shape[-1]
    row = lambda b: (b, 0)
    if mode == "a":
        cache_specs = [pl.BlockSpec((None, None) + cache_k.shape[2:], lambda b: (layer, b, 0, 0)),
                       pl.BlockSpec((None, None) + cache_v.shape[2:], lambda b: (layer, b, 0, 0))]
        dma_scratch = []
    else:
        rows_c = cache_k.shape[2]
        cache_specs = [pl.BlockSpec(memory_space=pl.ANY), pl.BlockSpec(memory_space=pl.ANY)]
        dma_scratch = [pltpu.VMEM((2, rows_c, HW), F32), pltpu.VMEM((2, rows_c, HW), F32),
                       pltpu.SemaphoreType.DMA((2, 2))]
    in_specs = [pl.BlockSpec((t, qw), row)] + cache_specs + [
                pl.BlockSpec((t, kn.shape[-1]), row),
                pl.BlockSpec((t, vn.shape[-1]), row)]
    scratch = []
    if mode == "d":
        lfc, lfn = extras
        in_specs += [pl.BlockSpec((None,) + lfc.shape[1:], lambda b: (b, 0, 0)),
                     pl.BlockSpec((None,) + lfn.shape[1:], lambda b: (b, 0, 0))]
        scratch = [pltpu.VMEM(lfc.shape[1:], F32)]
    elif mode == "c":
        (bias,) = extras
        in_specs += [pl.BlockSpec(bias.shape, lambda b: (0, 0, 0))]
    else:
        in_specs += [pl.BlockSpec(e.shape, lambda b: (0, 0)) for e in extras]
    return pl.pallas_call(
        functools.partial(_sattn_kernel, mode=mode, t=t, past=past, layer=layer, nb=nb),
        out_shape=jax.ShapeDtypeStruct((nb * t, HW), BF),
        grid=(nb,),
        in_specs=in_specs,
        out_specs=pl.BlockSpec((t, HW), row),
        scratch_shapes=scratch + dma_scratch,
        compiler_params=_cparams("arbitrary"),
        name="sample_attn_" + mode,
    )(q, cache_k, cache_v, kn, vn, *extras)


def _gmerge_kernel(h_ref, oa_ref, ob_ref, oc_ref, od_ref, wg0_ref, wg1_ref, wg2_ref, wg3_ref,
                   wb_ref, out_ref):
    hb = h_ref[...]
    acc = None
    for g, (o_ref, wg_ref) in enumerate(((oa_ref, wg0_ref), (ob_ref, wg1_ref),
                                         (oc_ref, wg2_ref), (od_ref, wg3_ref))):
        term = _sigmoid(_dot(hb, wg_ref[...])) * _dot(o_ref[...], wb_ref[g])
        acc = term if acc is None else acc + term
    out_ref[...] = acc.astype(BF)


def _gmerge(hb, outs, wg, wb, tm):
    m, d = hb.shape
    tn = WIN_TILE
    nt = d // tn
    row = lambda i, n: (i, 0)
    wg_specs = [pl.BlockSpec((d, tn), functools.partial(lambda i, n, g: (0, PROJ_TILES + g * nt + n), g=g))
                for g in range(N_BRANCH)]
    return pl.pallas_call(
        _gmerge_kernel,
        out_shape=jax.ShapeDtypeStruct((m, d), BF),
        grid=(m // tm, nt),
        in_specs=[pl.BlockSpec((tm, d), row)] + [pl.BlockSpec((tm, HW), row)] * N_BRANCH + wg_specs
                 + [pl.BlockSpec((N_BRANCH, HW, tn), lambda i, n: (0, 0, n))],
        out_specs=pl.BlockSpec((tm, tn), lambda i, n: (i, n)),
        compiler_params=_cparams("arbitrary", "arbitrary"),
        name="gate_merge",
    )(hb, *outs, wg, wg, wg, wg, wb)


def _oproj_kernel(mix_ref, x_ref, g_ref, w_ref, o_ref):
    o_ref[...] = x_ref[...] + g_ref[...] * _dot(mix_ref[...], w_ref[...])


def _oproj(mix, x2d, gate, w, tm):
    m, d = x2d.shape
    rows = gate.shape[1]
    tiles_per_group = (m // gate.shape[0]) // tm
    row = lambda i: (i, 0)
    return pl.pallas_call(
        _oproj_kernel,
        out_shape=jax.ShapeDtypeStruct((m, d), F32),
        grid=(m // tm,),
        in_specs=[pl.BlockSpec((tm, d), row), pl.BlockSpec((tm, d), row),
                  pl.BlockSpec((None, rows, d), lambda i: (i // tiles_per_group, 0, 0)),
                  pl.BlockSpec((d, d), lambda i: (0, 0))],
        out_specs=pl.BlockSpec((tm, d), row),
        compiler_params=_cparams("arbitrary"),
        name="out_proj",
    )(mix, x2d, gate, w)


def _ffn_kernel(x_ref, sc_ref, sh_ref, g_ref, ng_ref, wu_ref, wd_ref, o_ref, h_scr, *, nf):
    f = pl.program_id(1)

    @pl.when(f == 0)
    def _():
        x = x_ref[...]
        y = _rms(x, x.shape[-1]) * ng_ref[...]
        h_scr[...] = (y * (1.0 + sc_ref[...]) + sh_ref[...]).astype(BF)
        o_ref[...] = jnp.zeros(o_ref.shape, F32)

    u = jnp.maximum(_dot(h_scr[...], wu_ref[...]), 0.0)
    o_ref[...] += _dot((u * u).astype(BF), wd_ref[...])

    @pl.when(f == nf - 1)
    def _():
        o_ref[...] = x_ref[...] + g_ref[...] * o_ref[...]


def _ffn(x2d, sc, sh, gate, ng, wu, wd, tm):
    m, d = x2d.shape
    dff = wu.shape[1]
    tf = 512
    rows = sc.shape[1]
    tiles_per_group = (m // sc.shape[0]) // tm
    row = lambda i, f: (i, 0)
    mod_spec = pl.BlockSpec((None, rows, d), lambda i, f: (i // tiles_per_group, 0, 0))
    return pl.pallas_call(
        functools.partial(_ffn_kernel, nf=dff // tf),
        out_shape=jax.ShapeDtypeStruct((m, d), F32),
        grid=(m // tm, dff // tf),
        in_specs=[pl.BlockSpec((tm, d), row), mod_spec, mod_spec, mod_spec,
                  pl.BlockSpec((1, d), lambda i, f: (0, 0)),
                  pl.BlockSpec((d, tf), lambda i, f: (0, f)),
                  pl.BlockSpec((tf, d), lambda i, f: (f, 0))],
        out_specs=pl.BlockSpec((tm, d), row),
        scratch_shapes=[pltpu.VMEM((tm, d), BF)],
        compiler_params=_cparams("arbitrary", "arbitrary"),
        name="ffn",
    )(x2d, sc, sh, gate, ng, wu, wd)


def _pack_layer(l, p, w_in_t):
    w_all = _win_prep(w_in_t, l, w_in_t.shape[2])

    qk = A_NOPE + A_ROPE
    wuq = p["w_a_uq"][l].reshape(A_Q_RANK, H, qk)
    wuq = jnp.pad(wuq, ((0, 0), (0, 0), (0, A_QK_PAD - qk))).reshape(A_Q_RANK, H * A_QK_PAD).astype(BF)
    wukv = p["w_a_ukv"][l].reshape(A_KV_RANK, H, A_NOPE + A_VDIM)
    wukv = jnp.concatenate([wukv[:, :, :A_NOPE].reshape(A_KV_RANK, H * A_NOPE),
                            wukv[:, :, A_NOPE:].reshape(A_KV_RANK, H * A_VDIM)], axis=1).astype(BF)

    def row(v, width=None):
        v = v.reshape(1, -1).astype(F32)
        if width is not None and v.shape[1] < width:
            v = jnp.pad(v, ((0, 0), (0, width - v.shape[1])))
        return v

    def head_rows(gq, gk, reps):
        return jnp.stack([jnp.tile(gq, reps), jnp.tile(gk, reps)]).astype(F32)

    return dict(
        w_all=w_all, wuq=wuq, wukv=wukv,
        ng1=row(p["norm1_g"][l]), ng2=row(p["norm2_g"][l]),
        gq=row(p["a_q_norm_g"][l]), gkv=row(p["a_kv_norm_g"][l]),
        gkpe=row(p["a_k_gain"][l][A_NOPE:], LANES),
        gqn=row(p["a_q_gain"][l][:A_NOPE]), gqr=row(p["a_q_gain"][l][A_NOPE:], LANES),
        gkn=row(p["a_k_gain"][l][:A_NOPE]),
        gb=head_rows(p["b_q_gain"][l], p["b_k_gain"][l], HW // B_DH),
        gc=head_rows(p["c_q_gain"][l], p["c_k_gain"][l], H),
        gd=head_rows(p["d_q_gain"][l], p["d_k_gain"][l], H),
        fb=row(p["d_forget_b"][l], LANES),
        lam=p["b_lambda"][l].astype(F32), sg=row(p["b_subln_g"][l]),
        rel=p["c_rel_bias"][l].astype(F32),
        wb=p["w_branch"][l].astype(BF), wo=p["w_out"][l].astype(BF),
        wu=p["w_up"][l].astype(BF), wd=p["w_down"][l].astype(BF),
    )


def _rope_tables(pos):
    half = A_ROPE // 2
    inv = ROPE_THETA ** (-jnp.arange(half, dtype=F32) / half)
    ang = pos.astype(F32)[:, None] * inv[None, :]
    cos, sin = jnp.cos(ang), jnp.sin(ang)
    z = jnp.zeros((pos.shape[0], LANES - A_ROPE), F32)
    return jnp.concatenate([cos, cos, z], axis=1), jnp.concatenate([-sin, sin, z], axis=1)


def _heads_to_rows(x, lanes):
    b, t, h = x.shape
    return jnp.pad(jnp.swapaxes(x, 1, 2), ((0, 0), (0, 8 - h), (0, lanes - t)))


def _front(x2d, mods, pk, tabs, tm, l, depth, kv_states, keep_c, rows_per_batch):
    sh1, sc1 = mods[0], mods[1]
    cos_t, sin_t = tabs
    hb, cqn, ckvn, kpe, kpep, logf = _proj0(
        x2d, sc1, sh1, pk["ng1"], pk["w_all"], pk["gq"], pk["gkv"], pk["gkpe"], pk["fb"], cos_t, sin_t, tm)
    f = dict(h=hb, ckv=ckvn, kpe=kpe, logf=logf)
    new_states = {}
    for mixer, first_tile, gs, keep in (("b", 2, B_DH, None), ("c", 5, C_DH, keep_c), ("d", 8, D_DH, None)):
        prev = None if kv_states is None else kv_states[mixer]
        q, k, v, ks, vs = _projg(hb, pk["w_all"], pk["g" + mixer], first_tile, gs, tm, "in_proj_" + mixer,
                                 l, depth, prev, keep, rows_per_batch)
        f["q" + mixer], f["k" + mixer], f["v" + mixer] = q, k, v
        new_states[mixer] = (ks, vs)
    f["qa"], f["ka"], f["va"] = _aprep(cqn, ckvn, kpep, pk["wuq"], pk["wukv"], pk["gqn"], pk["gqr"],
                                       pk["gkn"], cos_t, sin_t, tm)
    return f, new_states


def _back(x2d, hb, outs, mods, pk, tm):
    g1, sh2, sc2, g2 = mods[2], mods[3], mods[4], mods[5]
    mix = _gmerge(hb, outs, pk["w_all"], pk["wb"], tm)
    x2d = _oproj(mix, x2d, g1, pk["wo"], min(tm, 512))
    return _ffn(x2d, sc2, sh2, g2, pk["ng2"], pk["wu"], pk["wd"], tm)


def _prompt_layer(x2d, mods, pk, tabs, bias_tiles, b, s, l, depth, kv_states):
    tm = min(1024, s)
    f, kv_states = _front(x2d, mods, pk, tabs, tm, l, depth, kv_states, min(BAND_ROWS, s), s)
    r3 = lambda a: a.reshape(b, s, a.shape[-1])
    o_a = _pattn("a", r3(f["qa"]), r3(f["ka"]), r3(f["va"]), ())
    o_b = _pattn("b", r3(f["qb"]), r3(f["kb"]), r3(f["vb"]), (pk["lam"], pk["sg"]), l)
    o_c = _pattn("c", r3(f["qc"]), r3(f["kc"]), r3(f["vc"]), (bias_tiles,))
    frow = _cumsum_rows(_heads_to_rows(r3(f["logf"]), s))
    fcol = jnp.swapaxes(frow[:, :H, :], 1, 2)
    o_d = _pattn("d", r3(f["qd"]), r3(f["kd"]), r3(f["vd"]), (fcol, frow))
    outs = [o.reshape(b * s, HW) for o in (o_a, o_b, o_c, o_d)]
    x2d = _back(x2d, f["h"], outs, mods, pk, tm)
    return x2d, (r3(f["ckv"]), r3(f["kpe"]), r3(f["logf"])), kv_states


def _sample_layer(x2d, mods, pk, tabs, bias_tiles, caches, nb, t, l, depth, kv_states):
    lat_c, kpe_t, kb_c, vb_c, kc_c, vc_c, kd_c, vd_c, logf_c = caches
    past = lat_c.shape[2]
    tm = nb * t
    f, kv_states = _front(x2d, mods, pk, tabs, tm, l, depth, kv_states, None, t)
    o_a = _sattn("a", l, f["qa"], lat_c, kpe_t, f["ka"], f["va"], (pk["wukv"], pk["gkn"]), t, past)
    o_b = _sattn("b", l, f["qb"], kb_c, vb_c, f["kb"], f["vb"], (pk["lam"], pk["sg"]), t, past)
    o_c = _sattn("c", l, f["qc"], kc_c, vc_c, f["kc"], f["vc"], (bias_tiles,), t, past)
    lfc = _heads_to_rows(logf_c[l].astype(F32), past)
    lfn = _heads_to_rows(f["logf"].reshape(nb, t, H), LANES)
    o_d = _sattn("d", l, f["qd"], kd_c, vd_c, f["kd"], f["vd"], (lfc, lfn), t, past)
    x2d = _back(x2d, f["h"], [o_a, o_b, o_c, o_d], mods, pk, tm)
    r3 = lambda a: a.reshape(nb, t, a.shape[-1])
    return x2d, (r3(f["ckv"]), r3(f["kpe"]), r3(f["logf"])), kv_states


def kernel(x_prompt, x_sample, c_prompt, c_sample,
           cache_a_latent, cache_a_kpe, cache_b_k, cache_b_v, cache_c_k, cache_c_v,
           cache_d_k, cache_d_v, cache_d_logf,
           norm1_g, norm2_g, w_ada, b_ada, w_in,
           a_q_norm_g, a_kv_norm_g, w_a_uq, w_a_ukv, a_q_gain, a_k_gain,
           b_q_gain, b_k_gain, b_lambda, b_subln_g,
           c_q_gain, c_k_gain, c_rel_bias,
           d_q_gain, d_k_gain, d_forget_b,
           w_branch, w_out, w_up, w_down):
    params = dict(norm1_g=norm1_g, norm2_g=norm2_g, w_in=w_in, a_q_norm_g=a_q_norm_g,
                  a_kv_norm_g=a_kv_norm_g, w_a_uq=w_a_uq, w_a_ukv=w_a_ukv, a_q_gain=a_q_gain,
                  a_k_gain=a_k_gain, b_q_gain=b_q_gain, b_k_gain=b_k_gain, b_lambda=b_lambda,
                  b_subln_g=b_subln_g, c_q_gain=c_q_gain, c_k_gain=c_k_gain, c_rel_bias=c_rel_bias,
                  d_q_gain=d_q_gain, d_k_gain=d_k_gain, d_forget_b=d_forget_b,
                  w_branch=w_branch, w_out=w_out, w_up=w_up, w_down=w_down)
    depth = w_in.shape[0]
    b, s, d = x_prompt.shape
    nb, t, _ = x_sample.shape
    past = cache_a_latent.shape[2]
    rows_c = cache_c_k.shape[2]

    n_c = b + nb
    c_all = jnp.pad(jnp.concatenate([c_prompt, c_sample], axis=0), ((0, (-n_c) % 8), (0, 0)))
    mod_all = _ada(c_all, w_ada, b_ada)

    tabs_p = _rope_tables(jnp.arange(s, dtype=jnp.int32))
    tabs_s = _rope_tables(jnp.tile(past + jnp.arange(t, dtype=jnp.int32), nb))

    tile = 256
    p_bases = [(BAND_ROWS // tile - k) * tile for k in range(BAND_ROWS // tile + 1)]

    x_p = x_prompt.reshape(b * s, d)
    x_s = x_sample.reshape(nb * t, d)
    w_in_t = jnp.transpose(w_in, (2, 0, 1))
    caches = (cache_a_latent, jnp.swapaxes(cache_a_kpe, 2, 3), cache_b_k, cache_b_v, cache_c_k,
              cache_c_v, cache_d_k, cache_d_v, cache_d_logf)
    states_p, states_s = [], []
    kv_p = kv_s = None
    for l in range(depth):
        pk = _pack_layer(l, params, w_in_t)
        mod = mod_all[l]
        mods_p = [m.reshape(b, 1, d) for m in jnp.split(mod[:b], 6, axis=-1)]
        mods_s = [jnp.repeat(m, t, axis=0).reshape(1, nb * t, d) for m in jnp.split(mod[b:n_c], 6, axis=-1)]
        bias_p = _relbias(pk["rel"], p_bases, tile, tile, -(CHUNK - 1), LOG2E)
        bias_s = jnp.concatenate(
            [_relbias(pk["rel"], [rows_c], t, rows_c, -REL_CLIP)[:, 0],
             _relbias(pk["rel"], [0], t, LANES, -REL_CLIP)[:, 0]], axis=-1)
        x_p, st_p, kv_p = _prompt_layer(x_p, mods_p, pk, tabs_p, bias_p, b, s, l, depth, kv_p)
        x_s, st_s, kv_s = _sample_layer(x_s, mods_s, pk, tabs_s, bias_s, caches, nb, t, l, depth, kv_s)
        states_p.append(st_p)
        states_s.append(st_s)
    lat_p, kpe_p, logf_p = [jnp.stack(z) for z in zip(*states_p)]
    lat_s, kpe_s, logf_s = [jnp.stack(z) for z in zip(*states_s)]

    def kv_out(states, mixer, which, nbatch):
        a = states[mixer][which]
        return a.reshape(depth, nbatch, a.shape[1] // nbatch, H, a.shape[-1])

    out = [x_p.reshape(b, s, d), x_s.reshape(nb, t, d), lat_p, lat_s, kpe_p, kpe_s]
    for mixer in ("b", "c", "d"):
        for which in (0, 1):
            out += [kv_out(kv_p, mixer, which, b), kv_out(kv_s, mixer, which, nb)]
    out += [logf_p, logf_s]
    return tuple(out)
```

```python
import functools
import math

import jax
import jax.numpy as jnp
from jax import lax
from jax.experimental import pallas as pl
from jax.experimental.pallas import tpu as pltpu

BF = jnp.bfloat16
F32 = jnp.float32

CHUNK = 64
EPS = 1e-6
NEG_INF = -1e30
H = 4
A_NOPE, A_ROPE, A_VDIM = 128, 64, 128
A_Q_RANK, A_KV_RANK = 512, 256
A_QK_PAD = 256
ROPE_THETA = 10000.0
B_DH = 64
C_DH = 128
BAND_CHUNKS = 8
BAND_ROWS = BAND_CHUNKS * CHUNK
REL_CLIP = 128
D_DH = 128
HW = 512
N_BRANCH = 4
LANES = 128
PROJ_TILE = 512

VMEM_LIMIT_BYTES = 56 * 1024 * 1024


def _cparams(*sem):
    return pltpu.CompilerParams(dimension_semantics=sem, vmem_limit_bytes=VMEM_LIMIT_BYTES)


def _nt_dot(a, b):
    return lax.dot_general(a, b, (((1,), (1,)), ((), ())), preferred_element_type=F32)


def _dot(a, b):
    return jnp.dot(a, b, preferred_element_type=F32)


def _rms(z, n):
    ms = jnp.sum(z * z, axis=-1, keepdims=True) * (1.0 / n)
    return z * lax.rsqrt(ms + EPS)


def _rms_groups(z, gs):
    w = z.shape[-1]
    if gs >= LANES:
        parts = [_rms(z[:, g * gs:(g + 1) * gs], gs) for g in range(w // gs)]
        return parts[0] if len(parts) == 1 else jnp.concatenate(parts, axis=-1)
    assert gs * 2 == LANES
    lo = lax.broadcasted_iota(jnp.int32, (1, LANES), 1) < gs
    parts = []
    for g in range(w // LANES):
        zz = z[:, g * LANES:(g + 1) * LANES]
        sq = zz * zz
        s_lo = jnp.sum(jnp.where(lo, sq, 0.0), axis=-1, keepdims=True)
        s_hi = jnp.sum(jnp.where(lo, 0.0, sq), axis=-1, keepdims=True)
        ms = jnp.where(lo, s_lo, s_hi) * (1.0 / gs)
        parts.append(zz * lax.rsqrt(ms + EPS))
    return jnp.concatenate(parts, axis=-1)


def _rope128(r, cos, sin):
    half = A_ROPE // 2
    lane = lax.broadcasted_iota(jnp.int32, (1, LANES), 1)
    swapped = jnp.where(lane < half, pltpu.roll(r, LANES - half, 1), pltpu.roll(r, half, 1))
    return r * cos + swapped * sin


CHUNK_SHIFT = CHUNK.bit_length() - 1
assert 1 << CHUNK_SHIFT == CHUNK


def _chunk(pos):
    return jnp.right_shift(pos, CHUNK_SHIFT)


def _log_sigmoid(x):
    return jnp.minimum(x, 0.0) - jnp.log1p(jnp.exp(-jnp.abs(x)))


def _sigmoid(x):
    return 1.0 / (1.0 + jnp.exp(-x))


def _ada_kernel(c_ref, w_ref, b_ref, o_ref):
    c = c_ref[...]
    a = (c * _sigmoid(c)).astype(BF)
    o_ref[...] = _dot(a, w_ref[...].astype(BF)) + b_ref[...]


def _ada(c_all, w_ada, b_ada):
    depth, d, n = w_ada.shape
    r = c_all.shape[0]
    tn = 1024
    return pl.pallas_call(
        _ada_kernel,
        out_shape=jax.ShapeDtypeStruct((depth, r, n), F32),
        grid=(depth, n // tn),
        in_specs=[
            pl.BlockSpec((r, d), lambda l, j: (0, 0)),
            pl.BlockSpec((None, d, tn), lambda l, j: (l, 0, j)),
            pl.BlockSpec((None, 1, tn), lambda l, j: (l, 0, j)),
        ],
        out_specs=pl.BlockSpec((None, r, tn), lambda l, j: (l, 0, j)),
        compiler_params=_cparams("arbitrary", "arbitrary"),
        name="ada_mod",
    )(c_all, w_ada, b_ada.reshape(depth, 1, n))


WIN_TILE = PROJ_TILE
WIN_PROJ_STARTS = (0, A_Q_RANK, 832, 1344, 1856, 2368, 2880, 3392, 3904, 4416, 4928, 5440)
WIN_GATE_START = 5444
PROJ_TILES = len(WIN_PROJ_STARTS)


def _winprep_kernel(off_ref, w_hbm, o_ref, buf, sem, *, layer, nsteps):
    s = pl.program_id(0)
    slot = s % 2

    def copy(step, slot_):
        return pltpu.make_async_copy(w_hbm.at[pl.ds(off_ref[step], WIN_TILE), layer, :],
                                     buf.at[slot_], sem.at[slot_])

    @pl.when(s == 0)
    def _():
        copy(0, 0).start()

    @pl.when(s + 1 < nsteps)
    def _():
        copy(s + 1, 1 - slot).start()

    copy(s, slot).wait()
    o_ref[...] = buf[slot].T.astype(BF)


def _win_prep(w_in_t, layer, d):
    gate_tiles = N_BRANCH * d // WIN_TILE
    starts = WIN_PROJ_STARTS + tuple(WIN_GATE_START + g * WIN_TILE for g in range(gate_tiles))
    assert starts[-1] + WIN_TILE == w_in_t.shape[0]
    nsteps = len(starts)
    rows = w_in_t.shape[2]
    return pl.pallas_call(
        functools.partial(_winprep_kernel, layer=layer, nsteps=nsteps),
        out_shape=jax.ShapeDtypeStruct((rows, nsteps * WIN_TILE), BF),
        grid_spec=pltpu.PrefetchScalarGridSpec(
            num_scalar_prefetch=1,
            grid=(nsteps,),
            in_specs=[pl.BlockSpec(memory_space=pl.ANY)],
            out_specs=pl.BlockSpec((rows, WIN_TILE), lambda s, off: (0, s)),
            scratch_shapes=[pltpu.VMEM((2, WIN_TILE, rows), F32), pltpu.SemaphoreType.DMA((2,))],
        ),
        compiler_params=_cparams("arbitrary"),
        name="w_in_repack",
    )(jnp.asarray(starts, jnp.int32), w_in_t)


PROJ_ROW_SPLIT = 2


def _row_parts(tm):
    part = tm // PROJ_ROW_SPLIT if tm % (PROJ_ROW_SPLIT * 16) == 0 else tm
    return [slice(r, r + part) for r in range(0, tm, part)]


def _mod_rows(ref, rows):
    return ref[...] if ref.shape[0] == 1 else ref[rows, :]


def _proj0_kernel(x_ref, sc_ref, sh_ref, ng_ref, w_ref, gq_ref, gkv_ref, gkpe_ref, fb_ref,
                  cos_ref, sin_ref, h_out, cq_out, ckv_out, kpe_out, kpep_out, logf_out):
    j = pl.program_id(1)
    parts = _row_parts(x_ref.shape[0])

    @pl.when(j == 0)
    def _():
        for rows in parts:
            x = x_ref[rows, :]
            y = _rms(x, x.shape[-1]) * ng_ref[...]
            h_out[rows, :] = (y * (1.0 + _mod_rows(sc_ref, rows)) + _mod_rows(sh_ref, rows)).astype(BF)

    def tile(epilogue):
        for rows in parts:
            epilogue(rows, _dot(h_out[rows, :], w_ref[...]))

    def cq_tile(rows, z):
        cq_out[rows, :] = (_rms(z, A_Q_RANK) * gq_ref[...]).astype(BF)

    def kv_tile(rows, z):
        ckv_out[rows, :] = _rms(z[:, :A_KV_RANK], A_KV_RANK) * gkv_ref[...]
        lane = lax.broadcasted_iota(jnp.int32, (1, LANES), 1)
        kp = jnp.where(lane < A_ROPE, z[:, A_KV_RANK:A_KV_RANK + LANES], 0.0)
        kp = _rope128(_rms(kp, A_ROPE) * gkpe_ref[...], cos_ref[rows, :], sin_ref[rows, :])
        kpe_out[rows, :] = kp[:, :A_ROPE]
        kpep_out[rows, :] = kp.astype(BF)

    def f_tile(rows, z):
        logf_out[rows, :] = _log_sigmoid(z[:, :LANES] + fb_ref[...])[:, :H]

    for jj, epilogue in enumerate((cq_tile, kv_tile, f_tile)):
        pl.when(j == jj)(functools.partial(tile, epilogue))


def _projg_kernel(*refs, gs, layer, aliased, keep, tiles_per_batch, n_tiles):
    if aliased:
        refs = refs[:3] + refs[5:]
    h_ref, w_ref, g_ref, q_out, k_out, v_out, ks_hbm, vs_hbm, kbuf, vbuf, sem = refs
    i = pl.program_id(0)
    j = pl.program_id(1)
    tm = h_ref.shape[0]
    parts = _row_parts(tm)
    dh = kbuf.shape[-1] // H

    def copies(tile_idx, buf, dst_hbm, s):
        if keep is None:
            src_rows, dst0 = slice(None), tile_idx * tm
        else:
            src_rows, dst0 = slice(tm - keep, tm), (tile_idx // tiles_per_batch) * keep
        n = tm if keep is None else keep
        layers = (layer,) if aliased else range(layer, dst_hbm.shape[0])
        return [pltpu.make_async_copy(buf.at[src_rows, pl.ds(hd * dh, dh)],
                                      dst_hbm.at[lyr, pl.ds(dst0, n), hd, :], s)
                for lyr in layers for hd in range(H)]

    def has_state(tile_idx):
        return True if keep is None else tile_idx % tiles_per_batch == tiles_per_batch - 1

    def tile(epilogue):
        for rows in parts:
            epilogue(rows, _dot(h_ref[rows, :], w_ref[...]))

    def q_tile(rows, z):
        q_out[rows, :] = (_rms_groups(z, gs) * g_ref[0:1, :]).astype(BF)

    def k_tile(rows, z):
        kn = _rms_groups(z, gs) * g_ref[1:2, :]
        kbuf[rows, :] = kn
        k_out[rows, :] = kn.astype(BF)

    def v_tile(rows, z):
        vbuf[rows, :] = z
        v_out[rows, :] = z.astype(BF)

    @pl.when(j == 0)
    def _():
        tile(q_tile)

        @pl.when((i > 0) & has_state(i - 1))
        def _():
            for c in copies(i - 1, vbuf, vs_hbm, sem.at[1]):
                c.wait()

    @pl.when(j == 1)
    def _():
        tile(k_tile)

        @pl.when(has_state(i))
        def _():
            for c in copies(i, kbuf, ks_hbm, sem.at[0]):
                c.start()

    @pl.when(j == 2)
    def _():
        tile(v_tile)

        @pl.when(has_state(i))
        def _():
            for c in copies(i, vbuf, vs_hbm, sem.at[1]):
                c.start()
            for c in copies(i, kbuf, ks_hbm, sem.at[0]):
                c.wait()

        @pl.when((i == n_tiles - 1) & has_state(i))
        def _():
            for c in copies(i, vbuf, vs_hbm, sem.at[1]):
                c.wait()


def _proj0(x2d, sc, sh, ng, w_all, gq, gkv, gkpe, fb, cos_t, sin_t, tm):
    m, d = x2d.shape
    groups, rows = sc.shape[0], sc.shape[1]
    tiles_per_group = (m // groups) // tm
    tab_blocks = cos_t.shape[0] // tm
    row = lambda i, j: (i, 0)
    const2 = lambda i, j: (0, 0)
    mod_spec = pl.BlockSpec((None, rows, d), lambda i, j: (i // tiles_per_group, 0, 0))
    tab_spec = pl.BlockSpec((tm, LANES), lambda i, j: (i % tab_blocks, 0))

    def out(width, dtype):
        return jax.ShapeDtypeStruct((m, width), dtype), pl.BlockSpec((tm, width), row)

    outs = [out(d, BF), out(A_Q_RANK, BF), out(A_KV_RANK, F32), out(A_ROPE, F32), out(LANES, BF),
            out(H, F32)]
    return pl.pallas_call(
        _proj0_kernel,
        out_shape=[o[0] for o in outs],
        grid=(m // tm, 3),
        in_specs=[
            pl.BlockSpec((tm, d), row),
            mod_spec, mod_spec,
            pl.BlockSpec((1, d), const2),
            pl.BlockSpec((d, PROJ_TILE), lambda i, j: (0, jnp.where(j == 2, PROJ_TILES - 1, j))),
            pl.BlockSpec((1, A_Q_RANK), const2),
            pl.BlockSpec((1, A_KV_RANK), const2),
            pl.BlockSpec((1, LANES), const2),
            pl.BlockSpec((1, LANES), const2),
            tab_spec, tab_spec,
        ],
        out_specs=[o[1] for o in outs],
        compiler_params=_cparams("arbitrary", "arbitrary"),
        name="in_proj_a",
    )(x2d, sc, sh, ng, w_all, gq, gkv, gkpe, fb, cos_t, sin_t)


def _projg(hb, w_all, gains, first_tile, gs, tm, name, layer, depth, prev_states, keep, rows_per_batch):
    m, d = hb.shape
    n_tiles = m // tm
    if keep is None:
        state_rows, tiles_per_batch = m, 1
    else:
        assert rows_per_batch % tm == 0 and keep <= tm
        state_rows, tiles_per_batch = (m // rows_per_batch) * keep, rows_per_batch // tm
    state = jax.ShapeDtypeStruct((depth, state_rows, H, HW // H), F32)
    aliased = prev_states is not None
    row = lambda i, j: (i, 0)
    hbm = pl.BlockSpec(memory_space=pl.ANY)
    return pl.pallas_call(
        functools.partial(_projg_kernel, gs=gs, layer=layer, aliased=aliased, keep=keep,
                          tiles_per_batch=tiles_per_batch, n_tiles=n_tiles),
        out_shape=[jax.ShapeDtypeStruct((m, HW), BF)] * 3 + [state, state],
        grid=(n_tiles, 3),
        in_specs=[pl.BlockSpec((tm, d), row),
                  pl.BlockSpec((d, PROJ_TILE), lambda i, j: (0, first_tile + j)),
                  pl.BlockSpec((2, HW), lambda i, j: (0, 0))] + ([hbm, hbm] if aliased else []),
        out_specs=[pl.BlockSpec((tm, HW), row)] * 3 + [hbm, hbm],
        scratch_shapes=[pltpu.VMEM((tm, HW), F32), pltpu.VMEM((tm, HW), F32), pltpu.SemaphoreType.DMA((2,))],
        input_output_aliases={3: 3, 4: 4} if aliased else {},
        compiler_params=_cparams("arbitrary", "arbitrary"),
        name=name,
    )(hb, w_all, gains, *(prev_states if aliased else ()))


def _aprep_kernel(cq_ref, ckv_ref, kpep_ref, wuq_ref, wukv_ref, gqn_ref, gqr_ref, gkn_ref,
                  cos_ref, sin_ref, q_out, k_out, v_out):
    zq = _dot(cq_ref[...], wuq_ref[...])
    cos, sin = cos_ref[...], sin_ref[...]
    parts = []
    for h in range(H):
        nope = zq[:, h * A_QK_PAD:h * A_QK_PAD + A_NOPE]
        rp = zq[:, h * A_QK_PAD + A_NOPE:(h + 1) * A_QK_PAD]
        parts.append(_rms(nope, A_NOPE) * gqn_ref[...])
        parts.append(_rope128(_rms(rp, A_ROPE) * gqr_ref[...], cos, sin))
    q_out[...] = jnp.concatenate(parts, axis=-1).astype(BF)

    zkv = _dot(ckv_ref[...].astype(BF), wukv_ref[...])
    kp = kpep_ref[...]
    parts = []
    for h in range(H):
        kn = _rms(zkv[:, h * A_NOPE:(h + 1) * A_NOPE], A_NOPE) * gkn_ref[...]
        parts.append(kn.astype(BF))
        parts.append(kp)
    k_out[...] = jnp.concatenate(parts, axis=-1)
    v_out[...] = zkv[:, H * A_NOPE:].astype(BF)


def _aprep(cqn, ckvn, kpep, wuq, wukv, gqn, gqr, gkn, cos_t, sin_t, tm):
    m = cqn.shape[0]
    tab_blocks = cos_t.shape[0] // tm
    row = lambda i: (i, 0)
    const = lambda i: (0, 0)
    tab_spec = pl.BlockSpec((tm, LANES), lambda i: (i % tab_blocks, 0))
    return pl.pallas_call(
        _aprep_kernel,
        out_shape=[jax.ShapeDtypeStruct((m, H * A_QK_PAD), BF),
                   jax.ShapeDtypeStruct((m, H * A_QK_PAD), BF),
                   jax.ShapeDtypeStruct((m, HW), BF)],
        grid=(m // tm,),
        in_specs=[
            pl.BlockSpec((tm, A_Q_RANK), row),
            pl.BlockSpec((tm, A_KV_RANK), row),
            pl.BlockSpec((tm, LANES), row),
            pl.BlockSpec(wuq.shape, const),
            pl.BlockSpec(wukv.shape, const),
            pl.BlockSpec((1, LANES), const),
            pl.BlockSpec((1, LANES), const),
            pl.BlockSpec((1, LANES), const),
            tab_spec, tab_spec,
        ],
        out_specs=[pl.BlockSpec((tm, H * A_QK_PAD), row),
                   pl.BlockSpec((tm, H * A_QK_PAD), row),
                   pl.BlockSpec((tm, HW), row)],
        compiler_params=_cparams("arbitrary"),
        name="a_prep",
    )(cqn, ckvn, kpep, wuq, wukv, gqn, gqr, gkn, cos_t, sin_t)


CUM_BLOCK = 256


def _tri_upper(n):
    r = lax.broadcasted_iota(jnp.int32, (n, n), 0)
    c = lax.broadcasted_iota(jnp.int32, (n, n), 1)
    return jnp.where(r <= c, 1.0, 0.0).astype(BF)


def _dot3(x, u):
    hi = x.astype(BF)
    r1 = x - hi.astype(F32)
    mid = r1.astype(BF)
    lo = (r1 - mid.astype(F32)).astype(BF)
    return _dot(hi, u) + _dot(mid, u) + _dot(lo, u)


def _cumsum_lanes(src_ref, dst_ref):
    u = _tri_upper(CUM_BLOCK)
    carry = jnp.zeros((8, 1), F32)
    for b in range(src_ref.shape[-1] // CUM_BLOCK):
        blk = slice(b * CUM_BLOCK, (b + 1) * CUM_BLOCK)
        c = _dot3(src_ref[:, blk], u) + carry
        dst_ref[:, blk] = c
        carry = c[:, CUM_BLOCK - 1:CUM_BLOCK]
    return carry


def _cumsum_kernel(x_ref, o_ref):
    _cumsum_lanes(x_ref, o_ref)


def _cumsum_rows(x):
    b, r, s = x.shape
    return pl.pallas_call(
        _cumsum_kernel,
        out_shape=jax.ShapeDtypeStruct(x.shape, F32),
        grid=(b,),
        in_specs=[pl.BlockSpec((None, r, s), lambda i: (i, 0, 0))],
        out_specs=pl.BlockSpec((None, r, s), lambda i: (i, 0, 0)),
        compiler_params=_cparams("arbitrary"),
        name="forget_cumsum",
    )(x)


def _relbias_kernel(tab_ref, o_ref, *, bases, rows, cols, dmin, mult):
    h = pl.program_id(0)
    r = lax.broadcasted_iota(jnp.int32, (rows, cols), 0)
    c = lax.broadcasted_iota(jnp.int32, (rows, cols), 1)
    for t, base in enumerate(bases):
        idx = jnp.clip(base + r - c, -REL_CLIP, REL_CLIP) + REL_CLIP
        lo = min(max(base - (cols - 1), max(dmin, -REL_CLIP)), REL_CLIP) + REL_CLIP
        hi = min(max(base + rows - 1, -REL_CLIP), REL_CLIP) + REL_CLIP

        def body(e, acc):
            return jnp.where(idx == e, tab_ref[h, e] * mult, acc)

        o_ref[t] = lax.fori_loop(lo, hi + 1, body, jnp.zeros((rows, cols), F32))


def _relbias(table, bases, rows, cols, dmin, mult=1.0):
    nh = table.shape[0]
    return pl.pallas_call(
        functools.partial(_relbias_kernel, bases=tuple(bases), rows=rows, cols=cols, dmin=dmin, mult=mult),
        out_shape=jax.ShapeDtypeStruct((nh, len(bases), rows, cols), F32),
        grid=(nh,),
        in_specs=[pl.BlockSpec(memory_space=pltpu.SMEM)],
        out_specs=pl.BlockSpec((None, len(bases), rows, cols), lambda h: (h, 0, 0, 0)),
        compiler_params=_cparams("arbitrary"),
        name="rel_bias_tiles",
    )(table)


def _diff_lambda(lam_ref, layer_idx):
    lam_init = 0.8 - 0.6 * math.exp(-0.3 * layer_idx)
    lp = lam_ref[...]
    a = jnp.sum(lp[0:1] * lp[1:2], keepdims=True)
    b = jnp.sum(lp[2:3] * lp[3:4], keepdims=True)
    return jnp.exp(a) - jnp.exp(b) + lam_init, lam_init


ALIBI_SLOPES = tuple(2.0 ** (-8.0 * (h + 1) / H) for h in range(H))


LOG2E = math.log2(math.e)
RG = 32


def _pattn_kernel(*refs, mode, tq, tk, nk, layer_idx, back):
    if mode == "a":
        q_ref, k_ref, v_ref, o_ref, m_ref, acc_ref, s_scr, p_scr, pm_scr = refs
        dq, scale = A_QK_PAD, (A_NOPE + A_ROPE) ** -0.5
    elif mode == "b":
        q_ref, k_ref, v_ref, lam_ref, sg_ref, o_ref, m_ref, acc_ref, s_scr, p_scr, pm_scr = refs
        dq, scale = 2 * B_DH, B_DH ** -0.5
    elif mode == "c":
        q_ref, k_ref, v_ref, bias_ref, o_ref, m_ref, acc_ref, s_scr, p_scr, pm_scr = refs
        dq, scale = C_DH, C_DH ** -0.5
    else:
        q_ref, k_ref, v_ref, fq_ref, fk_ref, o_ref, m_ref, acc_ref, s_scr, p_scr, pm_scr, fq_scr = refs
        dq, scale = D_DH, D_DH ** -0.5
    assert tq == tk and CHUNK % RG == 0 and tk % LANES == 0
    dv = HW // H
    c1 = scale * LOG2E
    nchunk = tk // LANES
    i = pl.program_id(1)
    j = pl.program_id(2)
    if mode == "c":
        kb = i - back + j
        valid = kb >= 0
        first = jnp.maximum(back - i, 0)
        last = nk - 1
    else:
        last = i
        kb = j
        valid = j <= last
        first = 0

    @pl.when(j == first)
    def _():
        m_ref[...] = jnp.full(m_ref.shape, -jnp.inf, F32)
        acc_ref[...] = jnp.zeros(acc_ref.shape, F32)
        if mode == "d":
            for h in range(H):
                fq_scr[h] = jnp.broadcast_to(fq_ref[:, h:h + 1] * LOG2E, (tq, LANES))

    lane = lax.broadcasted_iota(jnp.int32, (RG, LANES), 1)
    lane_minus_row = lane - lax.broadcasted_iota(jnp.int32, (RG, LANES), 0)

    def chunk_visibility(kind, r0, c):
        c0 = c * LANES
        if kind == "none":
            return "all"
        if kind == "lt":
            bound = (r0 // CHUNK + 1) * CHUNK
            if c0 >= bound:
                return "none"
            return "all" if c0 + LANES <= bound else lane < bound - c0
        if kind == "ge":
            bound = r0 // CHUNK * CHUNK
            if c0 + LANES <= bound:
                return "none"
            return "all" if c0 >= bound else lane >= bound - c0
        assert kind == "causal"
        if c0 > r0 + RG - 1:
            return "none"
        return "all" if c0 + LANES - 1 <= r0 else lane_minus_row <= r0 - c0

    def sweep1(idxs, h, g, kind, delta):
        r0 = g * RG
        rows = slice(r0, r0 + RG)
        pm = [None] * len(idxs)
        for c in range(nchunk):
            vis = chunk_visibility(kind, r0, c)
            if isinstance(vis, str) and vis == "none":
                continue
            cols = slice(c * LANES, (c + 1) * LANES)
            if mode == "b":
                dist = jnp.abs(lane_minus_row.astype(F32) - (delta + float(r0 - c * LANES)))
                bias = dist * (-ALIBI_SLOPES[h] * LOG2E)
            elif mode == "c":
                bias = bias_ref[h, rows, cols]
            elif mode == "d":
                bias = fq_scr[h, rows, :] - fk_ref[h:h + 1, cols] * LOG2E
            for n, idx in enumerate(idxs):
                t = s_scr[idx, rows, cols] * c1
                if mode != "a":
                    t = t + bias
                if not isinstance(vis, str):
                    t = jnp.where(vis, t, NEG_INF)
                s_scr[idx, rows, cols] = t
                pm[n] = t if pm[n] is None else jnp.maximum(pm[n], t)
        for n, idx in enumerate(idxs):
            pm_scr[idx, rows, :] = pm[n]

    def sweep2(idx, g, kind):
        r0 = g * RG
        rows = slice(r0, r0 + RG)
        m_rows = m_ref[idx, rows, :]
        for c in range(nchunk):
            cols = slice(c * LANES, (c + 1) * LANES)
            vis = chunk_visibility(kind, r0, c)
            if isinstance(vis, str) and vis == "none":
                p_scr[idx, rows, cols] = jnp.zeros((RG, LANES), BF)
            else:
                p_scr[idx, rows, cols] = jnp.exp2(s_scr[idx, rows, cols] - m_rows).astype(BF)

    def block(kind):
        delta = ((i - kb) * tq).astype(F32) if mode == "b" else None
        ones = jnp.ones((tk, LANES), BF)
        for h in range(H):
            q = q_ref[:, h * dq:(h + 1) * dq]
            k = k_ref[:, h * dq:(h + 1) * dq].astype(BF)
            v1 = jnp.concatenate([v_ref[:, h * dv:(h + 1) * dv].astype(BF), ones], axis=-1)
            if mode == "b":
                lo = lax.broadcasted_iota(jnp.int32, (1, dq), 1) < B_DH
                zero = jnp.zeros_like(q)
                s_scr[h] = _nt_dot(jnp.where(lo, q, zero), k)
                s_scr[H + h] = _nt_dot(jnp.where(lo, zero, q), k)
                idxs = (h, H + h)
            else:
                s_scr[h] = _nt_dot(q, k)
                idxs = (h,)
            for g in range(tq // RG):
                sweep1(idxs, h, g, kind, delta)
            for idx in idxs:
                m_prev = m_ref[idx]
                m_new = jnp.maximum(m_prev, jnp.max(pm_scr[idx], axis=-1, keepdims=True))
                m_ref[idx] = m_new
                alpha = jnp.exp2(m_prev - m_new)
                for g in range(tq // RG):
                    sweep2(idx, g, kind)
                acc_ref[idx] = jnp.concatenate([alpha, alpha], axis=-1) * acc_ref[idx] + _dot(p_scr[idx], v1)

    if mode == "c":
        assert back == 2 and BAND_ROWS == back * tk
        for jj, kind in enumerate(("ge", "none", "lt")):
            pl.when((j == jj) & valid)(functools.partial(block, kind))
    else:
        pl.when(j < last)(functools.partial(block, "none"))
        pl.when(j == last)(functools.partial(block, "causal" if mode == "d" else "lt"))

    @pl.when(j == last)
    def _():
        for h in range(H):
            o = acc_ref[h, :, :dv] / acc_ref[h, :, dv:]
            if mode == "b":
                lam, lam_init = _diff_lambda(lam_ref, layer_idx)
                o = o - lam * (acc_ref[H + h, :, :dv] / acc_ref[H + h, :, dv:])
                o = (_rms(o, dv) * sg_ref[...]) * (1.0 - lam_init)
            o_ref[:, h * dv:(h + 1) * dv] = o.astype(BF)


def _pattn(mode, q, k, v, extras, layer_idx=0):
    b, s, qw = q.shape
    kw = k.shape[-1]
    back = 0
    if mode == "c":
        tq = tk = 256
        back = BAND_ROWS // tk
        nk = back + 1
        kmap = lambda bi, i, j: (bi, jnp.maximum(i - back + j, 0), 0)
    else:
        tq = tk = 512
        nk = s // tk
        kmap = lambda bi, i, j: (bi, jnp.minimum(j, ((i + 1) * tq - 1) // tk), 0)
    qmap = lambda bi, i, j: (bi, i, 0)
    in_specs = [pl.BlockSpec((None, tq, qw), qmap),
                pl.BlockSpec((None, tk, kw), kmap),
                pl.BlockSpec((None, tk, HW), kmap)]
    if mode == "b":
        lam, sg = extras
        in_specs += [pl.BlockSpec(lam.shape, lambda bi, i, j: (0, 0)),
                     pl.BlockSpec(sg.shape, lambda bi, i, j: (0, 0))]
    elif mode == "c":
        (bias,) = extras
        in_specs += [pl.BlockSpec((H, None, tq, tk), lambda bi, i, j: (0, j, 0, 0))]
    elif mode == "d":
        fcol, frow = extras
        in_specs += [pl.BlockSpec((None, tq, H), qmap),
                     pl.BlockSpec((None, 8, tk), lambda bi, i, j: (bi, 0, kmap(bi, i, j)[1]))]
    nstate = 2 * H if mode == "b" else H
    return pl.pallas_call(
        functools.partial(_pattn_kernel, mode=mode, tq=tq, tk=tk, nk=nk, layer_idx=layer_idx, back=back),
        out_shape=jax.ShapeDtypeStruct((b, s, HW), BF),
        grid=(b, s // tq, nk),
        in_specs=in_specs,
        out_specs=pl.BlockSpec((None, tq, HW), qmap),
        scratch_shapes=[pltpu.VMEM((nstate, tq, LANES), F32),
                        pltpu.VMEM((nstate, tq, 2 * (HW // H)), F32),
                        pltpu.VMEM((nstate, tq, tk), F32),
                        pltpu.VMEM((nstate, tq, tk), BF),
                        pltpu.VMEM((nstate, tq, LANES), F32)]
                       + ([pltpu.VMEM((H, tq, LANES), F32)] if mode == "d" else []),
        compiler_params=_cparams("arbitrary", "arbitrary", "arbitrary"),
        name="prompt_attn_" + mode,
    )(q, k, v, *extras)


def _pad_rows(x, rows):
    return jnp.concatenate([x, jnp.zeros((rows - x.shape[0], x.shape[1]), x.dtype)], axis=0)


def _prefetch_head_caches(k_hbm, v_hbm, kbuf, vbuf, sem, layer, nb):
    b = pl.program_id(0)
    slot = b % 2
    dh = kbuf.shape[-1] // H

    def copies(bidx, slot_):
        out = []
        for h in range(H):
            cols = pl.ds(h * dh, dh)
            out.append(pltpu.make_async_copy(k_hbm.at[layer, bidx, :, h, :], kbuf.at[slot_, :, cols],
                                             sem.at[slot_, 0]))
            out.append(pltpu.make_async_copy(v_hbm.at[layer, bidx, :, h, :], vbuf.at[slot_, :, cols],
                                             sem.at[slot_, 1]))
        return out

    @pl.when(b == 0)
    def _():
        for c in copies(0, 0):
            c.start()

    @pl.when(b + 1 < nb)
    def _():
        for c in copies(b + 1, 1 - slot):
            c.start()

    for c in copies(b, slot):
        c.wait()
    return slot


def _sattn_kernel(*refs, mode, t, past, layer, nb):
    if mode == "a":
        (q_ref, lat_ref, kpet_ref, kn_ref, vn_ref, wukv_ref, gkn_ref, o_ref) = refs
        dq, scale = A_QK_PAD, (A_NOPE + A_ROPE) ** -0.5
        rows_c = lat_ref.shape[0]
    else:
        if mode == "b":
            (q_ref, k_hbm, v_hbm, kn_ref, vn_ref, lam_ref, sg_ref, o_ref, kbuf, vbuf, sem) = refs
            dq, scale = 2 * B_DH, B_DH ** -0.5
        elif mode == "c":
            (q_ref, k_hbm, v_hbm, kn_ref, vn_ref, bias_ref, o_ref, kbuf, vbuf, sem) = refs
            dq, scale = C_DH, C_DH ** -0.5
        else:
            (q_ref, k_hbm, v_hbm, kn_ref, vn_ref, lfc_ref, lfn_ref, o_ref, f_scr, kbuf, vbuf, sem) = refs
            dq, scale = D_DH, D_DH ** -0.5
        rows_c = kbuf.shape[1]
        slot = _prefetch_head_caches(k_hbm, v_hbm, kbuf, vbuf, sem, layer, nb)
        kc_ref, vc_ref = kbuf.at[slot], vbuf.at[slot]
    dv = HW // H
    npad = LANES
    tt = 2 * t if mode == "b" else t

    def query_index(shape):
        r = lax.broadcasted_iota(jnp.int32, shape, 0)
        return r if tt == t else jnp.where(r >= t, r - t, r)

    qpos_c = past + query_index((tt, rows_c))
    kpos_c = (past - rows_c) + lax.broadcasted_iota(jnp.int32, (tt, rows_c), 1)
    qpos_n = past + query_index((tt, npad))
    col_n = lax.broadcasted_iota(jnp.int32, (tt, npad), 1)
    kpos_n = past + col_n
    real_n = col_n < t
    if mode in ("a", "b"):
        mask_c = _chunk(kpos_c) <= _chunk(qpos_c)
        mask_n = real_n & (_chunk(kpos_n) <= _chunk(qpos_n))
    elif mode == "c":
        qc_c, kc_c = _chunk(qpos_c), _chunk(kpos_c)
        qc_n, kc_n = _chunk(qpos_n), _chunk(kpos_n)
        mask_c = (kpos_c >= 0) & (kc_c <= qc_c) & (kc_c >= qc_c - BAND_CHUNKS)
        mask_n = real_n & (kc_n <= qc_n) & (kc_n >= qc_n - BAND_CHUNKS)
    else:
        mask_c = kpos_c <= qpos_c
        mask_n = real_n & (kpos_n <= qpos_n)

    if mode == "a":
        lat = lat_ref[...].astype(BF)
        kpe_t = kpet_ref[...]
        kpe_t = jnp.concatenate([kpe_t, jnp.zeros_like(kpe_t)], axis=0).astype(BF)
    if mode == "b":
        dist_c = jnp.abs(qpos_c - kpos_c).astype(F32)
        dist_n = jnp.abs(qpos_n - kpos_n).astype(F32)
        lo = lax.broadcasted_iota(jnp.int32, (1, dq), 1) < B_DH
        lam, lam_init = _diff_lambda(lam_ref, layer)
    if mode == "d":
        carry = _cumsum_lanes(lfc_ref, f_scr)
        f_new = _dot3(lfn_ref[...], _tri_upper(LANES)) + carry
        eye = (lax.broadcasted_iota(jnp.int32, (t, npad), 0) == col_n)

    def softmax_pv(s_c, s_n, v_c, v_n):
        m = jnp.maximum(jnp.max(s_c, axis=-1, keepdims=True), jnp.max(s_n, axis=-1, keepdims=True))
        p_c = jnp.exp(s_c - m)
        p_n = jnp.exp(s_n - m)
        l = jnp.sum(p_c, axis=-1, keepdims=True) + jnp.sum(p_n, axis=-1, keepdims=True)
        return (_dot(p_c.astype(BF), v_c) + _dot(p_n.astype(BF), v_n)) / l

    for h in range(H):
        q = q_ref[:, h * dq:(h + 1) * dq]
        if mode == "a":
            kn = _pad_rows(kn_ref[:, h * dq:(h + 1) * dq], npad)
            vn = _pad_rows(vn_ref[:, h * dv:(h + 1) * dv], npad)
            w = wukv_ref[...]
            k_nope = _dot(lat, w[:, h * A_NOPE:(h + 1) * A_NOPE])
            k_nope = (_rms(k_nope, A_NOPE) * gkn_ref[...]).astype(BF)
            v_c = _dot(lat, w[:, H * A_NOPE + h * A_VDIM:H * A_NOPE + (h + 1) * A_VDIM]).astype(BF)
            s_c = (_nt_dot(q[:, :A_NOPE], k_nope) + _dot(q[:, A_NOPE:], kpe_t)) * scale
            s_n = _nt_dot(q, kn) * scale
        else:
            k_c = kc_ref[:, h * dq:(h + 1) * dq].astype(BF)
            v_c = vc_ref[:, h * dv:(h + 1) * dv].astype(BF)
            kn = _pad_rows(kn_ref[:, h * dq:(h + 1) * dq], npad).astype(BF)
            vn = _pad_rows(vn_ref[:, h * dv:(h + 1) * dv], npad).astype(BF)
        if mode == "b":
            zero = jnp.zeros_like(q)
            q12 = jnp.concatenate([jnp.where(lo, q, zero), jnp.where(lo, zero, q)], axis=0)
            s_c = _nt_dot(q12, k_c) * scale + (-ALIBI_SLOPES[h]) * dist_c
            s_n = _nt_dot(q12, kn) * scale + (-ALIBI_SLOPES[h]) * dist_n
            o12 = softmax_pv(jnp.where(mask_c, s_c, NEG_INF), jnp.where(mask_n, s_n, NEG_INF), v_c, vn)
            o = o12[:t] - lam * o12[t:]
            o = (_rms(o, dv) * sg_ref[...]) * (1.0 - lam_init)
        else:
            if mode != "a":
                s_c = _nt_dot(q, k_c) * scale
                s_n = _nt_dot(q, kn) * scale
            if mode == "c":
                s_c = s_c + bias_ref[h, :, :rows_c]
                s_n = s_n + bias_ref[h, :, rows_c:]
            elif mode == "d":
                fq = jnp.sum(jnp.where(eye, f_new[h:h + 1, :], 0.0), axis=-1, keepdims=True)
                s_c = s_c + (fq - f_scr[h:h + 1, :])
                s_n = s_n + (fq - f_new[h:h + 1, :])
            s_c = jnp.where(mask_c, s_c, NEG_INF)
            s_n = jnp.where(mask_n, s_n, NEG_INF)
            o = softmax_pv(s_c, s_n, v_c, vn)
        o_ref[:, h * dv:(h + 1) * dv] = o.astype(BF)


def _sattn(mode, layer, q, cache_k, cache_v, kn, vn, extras, t, past):
    nb = cache_k.shape[1]
    qw = q.shape[-1]
    row = lambda b: (b, 0)
    if mode == "a":
        cache_specs = [pl.BlockSpec((None, None) + cache_k.shape[2:], lambda b: (layer, b, 0, 0)),
                       pl.BlockSpec((None, None) + cache_v.shape[2:], lambda b: (layer, b, 0, 0))]
        dma_scratch = []
    else:
        rows_c = cache_k.shape[2]
        cache_specs = [pl.BlockSpec(memory_space=pl.ANY), pl.BlockSpec(memory_space=pl.ANY)]
        dma_scratch = [pltpu.VMEM((2, rows_c, HW), F32), pltpu.VMEM((2, rows_c, HW), F32),
                       pltpu.SemaphoreType.DMA((2, 2))]
    in_specs = [pl.BlockSpec((t, qw), row)] + cache_specs + [
                pl.BlockSpec((t, kn.shape[-1]), row),
                pl.BlockSpec((t, vn.shape[-1]), row)]
    scratch = []
    if mode == "d":
        lfc, lfn = extras
        in_specs += [pl.BlockSpec((None,) + lfc.shape[1:], lambda b: (b, 0, 0)),
                     pl.BlockSpec((None,) + lfn.shape[1:], lambda b: (b, 0, 0))]
        scratch = [pltpu.VMEM(lfc.shape[1:], F32)]
    elif mode == "c":
        (bias,) = extras
        in_specs += [pl.BlockSpec(bias.shape, lambda b: (0, 0, 0))]
    else:
        in_specs += [pl.BlockSpec(e.shape, lambda b: (0, 0)) for e in extras]
    return pl.pallas_call(
        functools.partial(_sattn_kernel, mode=mode, t=t, past=past, layer=layer, nb=nb),
        out_shape=jax.ShapeDtypeStruct((nb * t, HW), BF),
        grid=(nb,),
        in_specs=in_specs,
        out_specs=pl.BlockSpec((t, HW), row),
        scratch_shapes=scratch + dma_scratch,
        compiler_params=_cparams("arbitrary"),
        name="sample_attn_" + mode,
    )(q, cache_k, cache_v, kn, vn, *extras)


def _gmerge_kernel(h_ref, oa_ref, ob_ref, oc_ref, od_ref, wg0_ref, wg1_ref, wg2_ref, wg3_ref,
                   wb_ref, out_ref):
    hb = h_ref[...]
    acc = None
    for g, (o_ref, wg_ref) in enumerate(((oa_ref, wg0_ref), (ob_ref, wg1_ref),
                                         (oc_ref, wg2_ref), (od_ref, wg3_ref))):
        term = _sigmoid(_dot(hb, wg_ref[...])) * _dot(o_ref[...], wb_ref[g])
        acc = term if acc is None else acc + term
    out_ref[...] = acc.astype(BF)


def _gmerge(hb, outs, wg, wb, layer, tm):
    m, d = hb.shape
    tn = WIN_TILE
    nt = d // tn
    row = lambda i, n: (i, 0)
    wg_specs = [pl.BlockSpec((d, tn), functools.partial(lambda i, n, g: (0, PROJ_TILES + g * nt + n), g=g))
                for g in range(N_BRANCH)]
    return pl.pallas_call(
        _gmerge_kernel,
        out_shape=jax.ShapeDtypeStruct((m, d), BF),
        grid=(m // tm, nt),
        in_specs=[pl.BlockSpec((tm, d), row)] + [pl.BlockSpec((tm, HW), row)] * N_BRANCH + wg_specs
                 + [pl.BlockSpec((None, N_BRANCH, HW, tn), lambda i, n: (layer, 0, 0, n))],
        out_specs=pl.BlockSpec((tm, tn), lambda i, n: (i, n)),
        compiler_params=_cparams("arbitrary", "arbitrary"),
        name="gate_merge",
    )(hb, *outs, wg, wg, wg, wg, wb)


def _oproj_kernel(mix_ref, x_ref, g_ref, w_ref, o_ref):
    o_ref[...] = x_ref[...] + g_ref[...] * _dot(mix_ref[...], w_ref[...])


def _oproj(mix, x2d, gate, w, layer, tm):
    m, d = x2d.shape
    rows = gate.shape[1]
    tiles_per_group = (m // gate.shape[0]) // tm
    row = lambda i: (i, 0)
    return pl.pallas_call(
        _oproj_kernel,
        out_shape=jax.ShapeDtypeStruct((m, d), F32),
        grid=(m // tm,),
        in_specs=[pl.BlockSpec((tm, d), row), pl.BlockSpec((tm, d), row),
                  pl.BlockSpec((None, rows, d), lambda i: (i // tiles_per_group, 0, 0)),
                  pl.BlockSpec((None, d, d), lambda i: (layer, 0, 0))],
        out_specs=pl.BlockSpec((tm, d), row),
        compiler_params=_cparams("arbitrary"),
        name="out_proj",
    )(mix, x2d, gate, w)


def _ffn_kernel(x_ref, sc_ref, sh_ref, g_ref, ng_ref, wu_ref, wd_ref, o_ref, h_scr, *, nf):
    f = pl.program_id(1)

    @pl.when(f == 0)
    def _():
        x = x_ref[...]
        y = _rms(x, x.shape[-1]) * ng_ref[...]
        h_scr[...] = (y * (1.0 + sc_ref[...]) + sh_ref[...]).astype(BF)
        o_ref[...] = jnp.zeros(o_ref.shape, F32)

    u = jnp.maximum(_dot(h_scr[...], wu_ref[...]), 0.0)
    o_ref[...] += _dot((u * u).astype(BF), wd_ref[...])

    @pl.when(f == nf - 1)
    def _():
        o_ref[...] = x_ref[...] + g_ref[...] * o_ref[...]


def _ffn(x2d, sc, sh, gate, ng, wu, wd, layer, tm):
    m, d = x2d.shape
    dff = wu.shape[2]
    tf = 512
    rows = sc.shape[1]
    tiles_per_group = (m // sc.shape[0]) // tm
    row = lambda i, f: (i, 0)
    mod_spec = pl.BlockSpec((None, rows, d), lambda i, f: (i // tiles_per_group, 0, 0))
    return pl.pallas_call(
        functools.partial(_ffn_kernel, nf=dff // tf),
        out_shape=jax.ShapeDtypeStruct((m, d), F32),
        grid=(m // tm, dff // tf),
        in_specs=[pl.BlockSpec((tm, d), row), mod_spec, mod_spec, mod_spec,
                  pl.BlockSpec((1, d), lambda i, f: (0, 0)),
                  pl.BlockSpec((None, d, tf), lambda i, f: (layer, 0, f)),
                  pl.BlockSpec((None, tf, d), lambda i, f: (layer, f, 0))],
        out_specs=pl.BlockSpec((tm, d), row),
        scratch_shapes=[pltpu.VMEM((tm, d), BF)],
        compiler_params=_cparams("arbitrary", "arbitrary"),
        name="ffn",
    )(x2d, sc, sh, gate, ng, wu, wd)


def _pack_layer(l, p, w_in_t):
    w_all = _win_prep(w_in_t, l, w_in_t.shape[2])

    qk = A_NOPE + A_ROPE
    wuq = p["w_a_uq"][l].reshape(A_Q_RANK, H, qk)
    wuq = jnp.pad(wuq, ((0, 0), (0, 0), (0, A_QK_PAD - qk))).reshape(A_Q_RANK, H * A_QK_PAD).astype(BF)
    wukv = p["w_a_ukv"][l].reshape(A_KV_RANK, H, A_NOPE + A_VDIM)
    wukv = jnp.concatenate([wukv[:, :, :A_NOPE].reshape(A_KV_RANK, H * A_NOPE),
                            wukv[:, :, A_NOPE:].reshape(A_KV_RANK, H * A_VDIM)], axis=1).astype(BF)

    def row(v, width=None):
        v = v.reshape(1, -1).astype(F32)
        if width is not None and v.shape[1] < width:
            v = jnp.pad(v, ((0, 0), (0, width - v.shape[1])))
        return v

    def head_rows(gq, gk, reps):
        return jnp.stack([jnp.tile(gq, reps), jnp.tile(gk, reps)]).astype(F32)

    return dict(
        w_all=w_all, wuq=wuq, wukv=wukv,
        ng1=row(p["norm1_g"][l]), ng2=row(p["norm2_g"][l]),
        gq=row(p["a_q_norm_g"][l]), gkv=row(p["a_kv_norm_g"][l]),
        gkpe=row(p["a_k_gain"][l][A_NOPE:], LANES),
        gqn=row(p["a_q_gain"][l][:A_NOPE]), gqr=row(p["a_q_gain"][l][A_NOPE:], LANES),
        gkn=row(p["a_k_gain"][l][:A_NOPE]),
        gb=head_rows(p["b_q_gain"][l], p["b_k_gain"][l], HW // B_DH),
        gc=head_rows(p["c_q_gain"][l], p["c_k_gain"][l], H),
        gd=head_rows(p["d_q_gain"][l], p["d_k_gain"][l], H),
        fb=row(p["d_forget_b"][l], LANES),
        lam=p["b_lambda"][l].astype(F32), sg=row(p["b_subln_g"][l]),
        rel=p["c_rel_bias"][l].astype(F32),
        wb=p["wb16"], wo=p["wo16"], wu=p["wu16"], wd=p["wd16"], layer=l,
    )


def _rope_tables(pos):
    half = A_ROPE // 2
    inv = ROPE_THETA ** (-jnp.arange(half, dtype=F32) / half)
    ang = pos.astype(F32)[:, None] * inv[None, :]
    cos, sin = jnp.cos(ang), jnp.sin(ang)
    z = jnp.zeros((pos.shape[0], LANES - A_ROPE), F32)
    return jnp.concatenate([cos, cos, z], axis=1), jnp.concatenate([-sin, sin, z], axis=1)


def _heads_to_rows(x, lanes):
    b, t, h = x.shape
    return jnp.pad(jnp.swapaxes(x, 1, 2), ((0, 0), (0, 8 - h), (0, lanes - t)))


def _front(x2d, mods, pk, tabs, tm, l, depth, kv_states, keep_c, rows_per_batch):
    sh1, sc1 = mods[0], mods[1]
    cos_t, sin_t = tabs
    hb, cqn, ckvn, kpe, kpep, logf = _proj0(
        x2d, sc1, sh1, pk["ng1"], pk["w_all"], pk["gq"], pk["gkv"], pk["gkpe"], pk["fb"], cos_t, sin_t, tm)
    f = dict(h=hb, ckv=ckvn, kpe=kpe, logf=logf)
    new_states = {}
    for mixer, first_tile, gs, keep in (("b", 2, B_DH, None), ("c", 5, C_DH, keep_c), ("d", 8, D_DH, None)):
        prev = None if kv_states is None else kv_states[mixer]
        q, k, v, ks, vs = _projg(hb, pk["w_all"], pk["g" + mixer], first_tile, gs, tm, "in_proj_" + mixer,
                                 l, depth, prev, keep, rows_per_batch)
        f["q" + mixer], f["k" + mixer], f["v" + mixer] = q, k, v
        new_states[mixer] = (ks, vs)
    f["qa"], f["ka"], f["va"] = _aprep(cqn, ckvn, kpep, pk["wuq"], pk["wukv"], pk["gqn"], pk["gqr"],
                                       pk["gkn"], cos_t, sin_t, tm)
    return f, new_states


def _back(x2d, hb, outs, mods, pk, tm):
    g1, sh2, sc2, g2 = mods[2], mods[3], mods[4], mods[5]
    l = pk["layer"]
    mix = _gmerge(hb, outs, pk["w_all"], pk["wb"], l, tm)
    x2d = _oproj(mix, x2d, g1, pk["wo"], l, min(tm, 512))
    return _ffn(x2d, sc2, sh2, g2, pk["ng2"], pk["wu"], pk["wd"], l, tm)


def _prompt_layer(x2d, mods, pk, tabs, bias_tiles, b, s, l, depth, kv_states):
    tm = min(1024, s)
    f, kv_states = _front(x2d, mods, pk, tabs, tm, l, depth, kv_states, min(BAND_ROWS, s), s)
    r3 = lambda a: a.reshape(b, s, a.shape[-1])
    o_a = _pattn("a", r3(f["qa"]), r3(f["ka"]), r3(f["va"]), ())
    o_b = _pattn("b", r3(f["qb"]), r3(f["kb"]), r3(f["vb"]), (pk["lam"], pk["sg"]), l)
    o_c = _pattn("c", r3(f["qc"]), r3(f["kc"]), r3(f["vc"]), (bias_tiles,))
    frow = _cumsum_rows(_heads_to_rows(r3(f["logf"]), s))
    fcol = jnp.swapaxes(frow[:, :H, :], 1, 2)
    o_d = _pattn("d", r3(f["qd"]), r3(f["kd"]), r3(f["vd"]), (fcol, frow))
    outs = [o.reshape(b * s, HW) for o in (o_a, o_b, o_c, o_d)]
    x2d = _back(x2d, f["h"], outs, mods, pk, tm)
    return x2d, (r3(f["ckv"]), r3(f["kpe"]), r3(f["logf"])), kv_states


def _sample_layer(x2d, mods, pk, tabs, bias_tiles, caches, nb, t, l, depth, kv_states):
    lat_c, kpe_t, kb_c, vb_c, kc_c, vc_c, kd_c, vd_c, logf_c = caches
    past = lat_c.shape[2]
    tm = nb * t
    f, kv_states = _front(x2d, mods, pk, tabs, tm, l, depth, kv_states, None, t)
    o_a = _sattn("a", l, f["qa"], lat_c, kpe_t, f["ka"], f["va"], (pk["wukv"], pk["gkn"]), t, past)
    o_b = _sattn("b", l, f["qb"], kb_c, vb_c, f["kb"], f["vb"], (pk["lam"], pk["sg"]), t, past)
    o_c = _sattn("c", l, f["qc"], kc_c, vc_c, f["kc"], f["vc"], (bias_tiles,), t, past)
    lfc = _heads_to_rows(logf_c[l].astype(F32), past)
    lfn = _heads_to_rows(f["logf"].reshape(nb, t, H), LANES)
    o_d = _sattn("d", l, f["qd"], kd_c, vd_c, f["kd"], f["vd"], (lfc, lfn), t, past)
    x2d = _back(x2d, f["h"], [o_a, o_b, o_c, o_d], mods, pk, tm)
    r3 = lambda a: a.reshape(nb, t, a.shape[-1])
    return x2d, (r3(f["ckv"]), r3(f["kpe"]), r3(f["logf"])), kv_states


def kernel(x_prompt, x_sample, c_prompt, c_sample,
           cache_a_latent, cache_a_kpe, cache_b_k, cache_b_v, cache_c_k, cache_c_v,
           cache_d_k, cache_d_v, cache_d_logf,
           norm1_g, norm2_g, w_ada, b_ada, w_in,
           a_q_norm_g, a_kv_norm_g, w_a_uq, w_a_ukv, a_q_gain, a_k_gain,
           b_q_gain, b_k_gain, b_lambda, b_subln_g,
           c_q_gain, c_k_gain, c_rel_bias,
           d_q_gain, d_k_gain, d_forget_b,
           w_branch, w_out, w_up, w_down):
    params = dict(norm1_g=norm1_g, norm2_g=norm2_g, w_in=w_in, a_q_norm_g=a_q_norm_g,
                  a_kv_norm_g=a_kv_norm_g, w_a_uq=w_a_uq, w_a_ukv=w_a_ukv, a_q_gain=a_q_gain,
                  a_k_gain=a_k_gain, b_q_gain=b_q_gain, b_k_gain=b_k_gain, b_lambda=b_lambda,
                  b_subln_g=b_subln_g, c_q_gain=c_q_gain, c_k_gain=c_k_gain, c_rel_bias=c_rel_bias,
                  d_q_gain=d_q_gain, d_k_gain=d_k_gain, d_forget_b=d_forget_b,
                  wb16=w_branch.astype(BF), wo16=w_out.astype(BF), wu16=w_up.astype(BF),
                  wd16=w_down.astype(BF))
    depth = w_in.shape[0]
    b, s, d = x_prompt.shape
    nb, t, _ = x_sample.shape
    past = cache_a_latent.shape[2]
    rows_c = cache_c_k.shape[2]

    n_c = b + nb
    c_all = jnp.pad(jnp.concatenate([c_prompt, c_sample], axis=0), ((0, (-n_c) % 8), (0, 0)))
    mod_all = _ada(c_all, w_ada, b_ada)

    tabs_p = _rope_tables(jnp.arange(s, dtype=jnp.int32))
    tabs_s = _rope_tables(jnp.tile(past + jnp.arange(t, dtype=jnp.int32), nb))

    tile = 256
    p_bases = [(BAND_ROWS // tile - k) * tile for k in range(BAND_ROWS // tile + 1)]

    x_p = x_prompt.reshape(b * s, d)
    x_s = x_sample.reshape(nb * t, d)
    w_in_t = jnp.transpose(w_in, (2, 0, 1))
    caches = (cache_a_latent, jnp.swapaxes(cache_a_kpe, 2, 3), cache_b_k, cache_b_v, cache_c_k,
              cache_c_v, cache_d_k, cache_d_v, cache_d_logf)
    states_p, states_s = [], []
    kv_p = kv_s = None
    for l in range(depth):
        pk = _pack_layer(l, params, w_in_t)
        mod = mod_all[l]
        mods_p = [m.reshape(b, 1, d) for m in jnp.split(mod[:b], 6, axis=-1)]
        mods_s = [jnp.repeat(m, t, axis=0).reshape(1, nb * t, d) for m in jnp.split(mod[b:n_c], 6, axis=-1)]
        bias_p = _relbias(pk["rel"], p_bases, tile, tile, -(CHUNK - 1), LOG2E)
        bias_s = jnp.concatenate(
            [_relbias(pk["rel"], [rows_c], t, rows_c, -REL_CLIP)[:, 0],
             _relbias(pk["rel"], [0], t, LANES, -REL_CLIP)[:, 0]], axis=-1)
        x_p, st_p, kv_p = _prompt_layer(x_p, mods_p, pk, tabs_p, bias_p, b, s, l, depth, kv_p)
        x_s, st_s, kv_s = _sample_layer(x_s, mods_s, pk, tabs_s, bias_s, caches, nb, t, l, depth, kv_s)
        states_p.append(st_p)
        states_s.append(st_s)
    lat_p, kpe_p, logf_p = [jnp.stack(z) for z in zip(*states_p)]
    lat_s, kpe_s, logf_s = [jnp.stack(z) for z in zip(*states_s)]

    def kv_out(states, mixer, which, nbatch):
        a = states[mixer][which]
        return a.reshape(depth, nbatch, a.shape[1] // nbatch, H, a.shape[-1])

    out = [x_p.reshape(b, s, d), x_s.reshape(nb, t, d), lat_p, lat_s, kpe_p, kpe_s]
    for mixer in ("b", "c", "d"):
        for which in (0, 1):
            out += [kv_out(kv_p, mixer, which, b), kv_out(kv_s, mixer, which, nb)]
    out += [logf_p, logf_s]
    return tuple(out)
```

```python
import functools
import math

import jax
import jax.numpy as jnp
from jax import lax
from jax.experimental import pallas as pl
from jax.experimental.pallas import tpu as pltpu

BF = jnp.bfloat16
F32 = jnp.float32

CHUNK = 64
EPS = 1e-6
NEG_INF = -1e30
H = 4
A_NOPE, A_ROPE, A_VDIM = 128, 64, 128
A_Q_RANK, A_KV_RANK = 512, 256
A_QK_PAD = 256
ROPE_THETA = 10000.0
B_DH = 64
C_DH = 128
BAND_CHUNKS = 8
BAND_ROWS = BAND_CHUNKS * CHUNK
REL_CLIP = 128
D_DH = 128
HW = 512
N_BRANCH = 4
LANES = 128
PROJ_TILE = 512

VMEM_LIMIT_BYTES = 56 * 1024 * 1024


def _cparams(*sem):
    return pltpu.CompilerParams(dimension_semantics=sem, vmem_limit_bytes=VMEM_LIMIT_BYTES)


def _nt_dot(a, b):
    return lax.dot_general(a, b, (((1,), (1,)), ((), ())), preferred_element_type=F32)


def _dot(a, b):
    return jnp.dot(a, b, preferred_element_type=F32)


def _rms(z, n):
    ms = jnp.sum(z * z, axis=-1, keepdims=True) * (1.0 / n)
    return z * lax.rsqrt(ms + EPS)


def _rms_groups(z, gs):
    w = z.shape[-1]
    if gs >= LANES:
        parts = [_rms(z[:, g * gs:(g + 1) * gs], gs) for g in range(w // gs)]
        return parts[0] if len(parts) == 1 else jnp.concatenate(parts, axis=-1)
    assert gs * 2 == LANES
    lo = lax.broadcasted_iota(jnp.int32, (1, LANES), 1) < gs
    parts = []
    for g in range(w // LANES):
        zz = z[:, g * LANES:(g + 1) * LANES]
        sq = zz * zz
        s_lo = jnp.sum(jnp.where(lo, sq, 0.0), axis=-1, keepdims=True)
        s_hi = jnp.sum(jnp.where(lo, 0.0, sq), axis=-1, keepdims=True)
        ms = jnp.where(lo, s_lo, s_hi) * (1.0 / gs)
        parts.append(zz * lax.rsqrt(ms + EPS))
    return jnp.concatenate(parts, axis=-1)


def _rope128(r, cos, sin):
    half = A_ROPE // 2
    lane = lax.broadcasted_iota(jnp.int32, (1, LANES), 1)
    swapped = jnp.where(lane < half, pltpu.roll(r, LANES - half, 1), pltpu.roll(r, half, 1))
    return r * cos + swapped * sin


CHUNK_SHIFT = CHUNK.bit_length() - 1
assert 1 << CHUNK_SHIFT == CHUNK


def _chunk(pos):
    return jnp.right_shift(pos, CHUNK_SHIFT)


def _log_sigmoid(x):
    return jnp.minimum(x, 0.0) - jnp.log1p(jnp.exp(-jnp.abs(x)))


def _sigmoid(x):
    return 1.0 / (1.0 + jnp.exp(-x))


def _ada_kernel(c_ref, w_ref, b_ref, o_ref):
    c = c_ref[...]
    a = (c * _sigmoid(c)).astype(BF)
    o_ref[...] = _dot(a, w_ref[...].astype(BF)) + b_ref[...]


def _ada(c_all, w_ada, b_ada):
    depth, d, n = w_ada.shape
    r = c_all.shape[0]
    tn = 1024
    return pl.pallas_call(
        _ada_kernel,
        out_shape=jax.ShapeDtypeStruct((depth, r, n), F32),
        grid=(depth, n // tn),
        in_specs=[
            pl.BlockSpec((r, d), lambda l, j: (0, 0)),
            pl.BlockSpec((None, d, tn), lambda l, j: (l, 0, j)),
            pl.BlockSpec((None, 1, tn), lambda l, j: (l, 0, j)),
        ],
        out_specs=pl.BlockSpec((None, r, tn), lambda l, j: (l, 0, j)),
        compiler_params=_cparams("arbitrary", "arbitrary"),
        name="ada_mod",
    )(c_all, w_ada, b_ada.reshape(depth, 1, n))


WIN_TILE = PROJ_TILE
WIN_PROJ_STARTS = (0, A_Q_RANK, 832, 1344, 1856, 2368, 2880, 3392, 3904, 4416, 4928, 5440)
WIN_GATE_START = 5444
PROJ_TILES = len(WIN_PROJ_STARTS)


def _winprep_kernel(off_ref, w_hbm, o_ref, buf, sem, *, layer, nsteps):
    s = pl.program_id(0)
    slot = s % 2

    def copy(step, slot_):
        return pltpu.make_async_copy(w_hbm.at[pl.ds(off_ref[step], WIN_TILE), layer, :],
                                     buf.at[slot_], sem.at[slot_])

    @pl.when(s == 0)
    def _():
        copy(0, 0).start()

    @pl.when(s + 1 < nsteps)
    def _():
        copy(s + 1, 1 - slot).start()

    copy(s, slot).wait()
    o_ref[...] = buf[slot].T.astype(BF)


def _win_prep(w_in_t, layer, d):
    gate_tiles = N_BRANCH * d // WIN_TILE
    starts = WIN_PROJ_STARTS + tuple(WIN_GATE_START + g * WIN_TILE for g in range(gate_tiles))
    assert starts[-1] + WIN_TILE == w_in_t.shape[0]
    nsteps = len(starts)
    rows = w_in_t.shape[2]
    return pl.pallas_call(
        functools.partial(_winprep_kernel, layer=layer, nsteps=nsteps),
        out_shape=jax.ShapeDtypeStruct((rows, nsteps * WIN_TILE), BF),
        grid_spec=pltpu.PrefetchScalarGridSpec(
            num_scalar_prefetch=1,
            grid=(nsteps,),
            in_specs=[pl.BlockSpec(memory_space=pl.ANY)],
            out_specs=pl.BlockSpec((rows, WIN_TILE), lambda s, off: (0, s)),
            scratch_shapes=[pltpu.VMEM((2, WIN_TILE, rows), F32), pltpu.SemaphoreType.DMA((2,))],
        ),
        compiler_params=_cparams("arbitrary"),
        name="w_in_repack",
    )(jnp.asarray(starts, jnp.int32), w_in_t)


PROJ_ROW_SPLIT = 2


def _row_parts(tm):
    part = tm // PROJ_ROW_SPLIT if tm % (PROJ_ROW_SPLIT * 16) == 0 else tm
    return [slice(r, r + part) for r in range(0, tm, part)]


def _mod_rows(ref, rows):
    return ref[...] if ref.shape[0] == 1 else ref[rows, :]


def _proj0_kernel(x_ref, sc_ref, sh_ref, ng_ref, w_ref, gq_ref, gkv_ref, gkpe_ref, fb_ref,
                  cos_ref, sin_ref, h_out, cq_out, ckv_out, kpe_out, kpep_out, logf_out):
    j = pl.program_id(1)
    parts = _row_parts(x_ref.shape[0])

    @pl.when(j == 0)
    def _():
        for rows in parts:
            x = x_ref[rows, :]
            y = _rms(x, x.shape[-1]) * ng_ref[...]
            h_out[rows, :] = (y * (1.0 + _mod_rows(sc_ref, rows)) + _mod_rows(sh_ref, rows)).astype(BF)

    def tile(epilogue):
        for rows in parts:
            epilogue(rows, _dot(h_out[rows, :], w_ref[...]))

    def cq_tile(rows, z):
        cq_out[rows, :] = (_rms(z, A_Q_RANK) * gq_ref[...]).astype(BF)

    def kv_tile(rows, z):
        ckv_out[rows, :] = _rms(z[:, :A_KV_RANK], A_KV_RANK) * gkv_ref[...]
        lane = lax.broadcasted_iota(jnp.int32, (1, LANES), 1)
        kp = jnp.where(lane < A_ROPE, z[:, A_KV_RANK:A_KV_RANK + LANES], 0.0)
        kp = _rope128(_rms(kp, A_ROPE) * gkpe_ref[...], cos_ref[rows, :], sin_ref[rows, :])
        kpe_out[rows, :] = kp[:, :A_ROPE]
        kpep_out[rows, :] = kp.astype(BF)

    def f_tile(rows, z):
        logf_out[rows, :] = _log_sigmoid(z[:, :LANES] + fb_ref[...])[:, :H]

    for jj, epilogue in enumerate((cq_tile, kv_tile, f_tile)):
        pl.when(j == jj)(functools.partial(tile, epilogue))


def _projg_kernel(*refs, gs, layer, aliased, keep, tiles_per_batch, n_tiles):
    if aliased:
        refs = refs[:3] + refs[5:]
    h_ref, w_ref, g_ref, q_out, k_out, v_out, ks_hbm, vs_hbm, kbuf, vbuf, sem = refs
    i = pl.program_id(0)
    j = pl.program_id(1)
    tm = h_ref.shape[0]
    parts = _row_parts(tm)
    dh = kbuf.shape[-1] // H

    def copies(tile_idx, buf, dst_hbm, s):
        if keep is None:
            src_rows, dst0 = slice(None), tile_idx * tm
        else:
            src_rows, dst0 = slice(tm - keep, tm), (tile_idx // tiles_per_batch) * keep
        n = tm if keep is None else keep
        layers = (layer,) if aliased else range(layer, dst_hbm.shape[0])
        return [pltpu.make_async_copy(buf.at[src_rows, pl.ds(hd * dh, dh)],
                                      dst_hbm.at[lyr, pl.ds(dst0, n), hd, :], s)
                for lyr in layers for hd in range(H)]

    def has_state(tile_idx):
        return True if keep is None else tile_idx % tiles_per_batch == tiles_per_batch - 1

    def tile(epilogue):
        for rows in parts:
            epilogue(rows, _dot(h_ref[rows, :], w_ref[...]))

    def q_tile(rows, z):
        q_out[rows, :] = (_rms_groups(z, gs) * g_ref[0:1, :]).astype(BF)

    def k_tile(rows, z):
        kn = _rms_groups(z, gs) * g_ref[1:2, :]
        kbuf[rows, :] = kn
        k_out[rows, :] = kn.astype(BF)

    def v_tile(rows, z):
        vbuf[rows, :] = z
        v_out[rows, :] = z.astype(BF)

    @pl.when(j == 0)
    def _():
        tile(q_tile)

        @pl.when((i > 0) & has_state(i - 1))
        def _():
            for c in copies(i - 1, vbuf, vs_hbm, sem.at[1]):
                c.wait()

    @pl.when(j == 1)
    def _():
        tile(k_tile)

        @pl.when(has_state(i))
        def _():
            for c in copies(i, kbuf, ks_hbm, sem.at[0]):
                c.start()

    @pl.when(j == 2)
    def _():
        tile(v_tile)

        @pl.when(has_state(i))
        def _():
            for c in copies(i, vbuf, vs_hbm, sem.at[1]):
                c.start()
            for c in copies(i, kbuf, ks_hbm, sem.at[0]):
                c.wait()

        @pl.when((i == n_tiles - 1) & has_state(i))
        def _():
            for c in copies(i, vbuf, vs_hbm, sem.at[1]):
                c.wait()


def _proj0(x2d, sc, sh, ng, w_all, gq, gkv, gkpe, fb, cos_t, sin_t, tm):
    m, d = x2d.shape
    groups, rows = sc.shape[0], sc.shape[1]
    tiles_per_group = (m // groups) // tm
    tab_blocks = cos_t.shape[0] // tm
    row = lambda i, j: (i, 0)
    const2 = lambda i, j: (0, 0)
    mod_spec = pl.BlockSpec((None, rows, d), lambda i, j: (i // tiles_per_group, 0, 0))
    tab_spec = pl.BlockSpec((tm, LANES), lambda i, j: (i % tab_blocks, 0))

    def out(width, dtype):
        return jax.ShapeDtypeStruct((m, width), dtype), pl.BlockSpec((tm, width), row)

    outs = [out(d, BF), out(A_Q_RANK, BF), out(A_KV_RANK, F32), out(A_ROPE, F32), out(LANES, BF),
            out(H, F32)]
    return pl.pallas_call(
        _proj0_kernel,
        out_shape=[o[0] for o in outs],
        grid=(m // tm, 3),
        in_specs=[
            pl.BlockSpec((tm, d), row),
            mod_spec, mod_spec,
            pl.BlockSpec((1, d), const2),
            pl.BlockSpec((d, PROJ_TILE), lambda i, j: (0, jnp.where(j == 2, PROJ_TILES - 1, j))),
            pl.BlockSpec((1, A_Q_RANK), const2),
            pl.BlockSpec((1, A_KV_RANK), const2),
            pl.BlockSpec((1, LANES), const2),
            pl.BlockSpec((1, LANES), const2),
            tab_spec, tab_spec,
        ],
        out_specs=[o[1] for o in outs],
        compiler_params=_cparams("arbitrary", "arbitrary"),
        name="in_proj_a",
    )(x2d, sc, sh, ng, w_all, gq, gkv, gkpe, fb, cos_t, sin_t)


def _projg(hb, w_all, gains, first_tile, gs, tm, name, layer, depth, prev_states, keep, rows_per_batch):
    m, d = hb.shape
    n_tiles = m // tm
    if keep is None:
        state_rows, tiles_per_batch = m, 1
    else:
        assert rows_per_batch % tm == 0 and keep <= tm
        state_rows, tiles_per_batch = (m // rows_per_batch) * keep, rows_per_batch // tm
    state = jax.ShapeDtypeStruct((depth, state_rows, H, HW // H), F32)
    aliased = prev_states is not None
    row = lambda i, j: (i, 0)
    hbm = pl.BlockSpec(memory_space=pl.ANY)
    return pl.pallas_call(
        functools.partial(_projg_kernel, gs=gs, layer=layer, aliased=aliased, keep=keep,
                          tiles_per_batch=tiles_per_batch, n_tiles=n_tiles),
        out_shape=[jax.ShapeDtypeStruct((m, HW), BF)] * 3 + [state, state],
        grid=(n_tiles, 3),
        in_specs=[pl.BlockSpec((tm, d), row),
                  pl.BlockSpec((d, PROJ_TILE), lambda i, j: (0, first_tile + j)),
                  pl.BlockSpec((2, HW), lambda i, j: (0, 0))] + ([hbm, hbm] if aliased else []),
        out_specs=[pl.BlockSpec((tm, HW), row)] * 3 + [hbm, hbm],
        scratch_shapes=[pltpu.VMEM((tm, HW), F32), pltpu.VMEM((tm, HW), F32), pltpu.SemaphoreType.DMA((2,))],
        input_output_aliases={3: 3, 4: 4} if aliased else {},
        compiler_params=_cparams("arbitrary", "arbitrary"),
        name=name,
    )(hb, w_all, gains, *(prev_states if aliased else ()))


def _aprep_kernel(cq_ref, ckv_ref, kpep_ref, wuq_ref, wukv_ref, gqn_ref, gqr_ref, gkn_ref,
                  cos_ref, sin_ref, q_out, k_out, v_out):
    zq = _dot(cq_ref[...], wuq_ref[...])
    cos, sin = cos_ref[...], sin_ref[...]
    parts = []
    for h in range(H):
        nope = zq[:, h * A_QK_PAD:h * A_QK_PAD + A_NOPE]
        rp = zq[:, h * A_QK_PAD + A_NOPE:(h + 1) * A_QK_PAD]
        parts.append(_rms(nope, A_NOPE) * gqn_ref[...])
        parts.append(_rope128(_rms(rp, A_ROPE) * gqr_ref[...], cos, sin))
    q_out[...] = jnp.concatenate(parts, axis=-1).astype(BF)

    zkv = _dot(ckv_ref[...].astype(BF), wukv_ref[...])
    kp = kpep_ref[...]
    parts = []
    for h in range(H):
        kn = _rms(zkv[:, h * A_NOPE:(h + 1) * A_NOPE], A_NOPE) * gkn_ref[...]
        parts.append(kn.astype(BF))
        parts.append(kp)
    k_out[...] = jnp.concatenate(parts, axis=-1)
    v_out[...] = zkv[:, H * A_NOPE:].astype(BF)


def _aprep(cqn, ckvn, kpep, wuq, wukv, gqn, gqr, gkn, cos_t, sin_t, tm):
    m = cqn.shape[0]
    tab_blocks = cos_t.shape[0] // tm
    row = lambda i: (i, 0)
    const = lambda i: (0, 0)
    tab_spec = pl.BlockSpec((tm, LANES), lambda i: (i % tab_blocks, 0))
    return pl.pallas_call(
        _aprep_kernel,
        out_shape=[jax.ShapeDtypeStruct((m, H * A_QK_PAD), BF),
                   jax.ShapeDtypeStruct((m, H * A_QK_PAD), BF),
                   jax.ShapeDtypeStruct((m, HW), BF)],
        grid=(m // tm,),
        in_specs=[
            pl.BlockSpec((tm, A_Q_RANK), row),
            pl.BlockSpec((tm, A_KV_RANK), row),
            pl.BlockSpec((tm, LANES), row),
            pl.BlockSpec(wuq.shape, const),
            pl.BlockSpec(wukv.shape, const),
            pl.BlockSpec((1, LANES), const),
            pl.BlockSpec((1, LANES), const),
            pl.BlockSpec((1, LANES), const),
            tab_spec, tab_spec,
        ],
        out_specs=[pl.BlockSpec((tm, H * A_QK_PAD), row),
                   pl.BlockSpec((tm, H * A_QK_PAD), row),
                   pl.BlockSpec((tm, HW), row)],
        compiler_params=_cparams("arbitrary"),
        name="a_prep",
    )(cqn, ckvn, kpep, wuq, wukv, gqn, gqr, gkn, cos_t, sin_t)


CUM_BLOCK = 256


def _tri_upper(n):
    r = lax.broadcasted_iota(jnp.int32, (n, n), 0)
    c = lax.broadcasted_iota(jnp.int32, (n, n), 1)
    return jnp.where(r <= c, 1.0, 0.0).astype(BF)


def _dot3(x, u):
    hi = x.astype(BF)
    r1 = x - hi.astype(F32)
    mid = r1.astype(BF)
    lo = (r1 - mid.astype(F32)).astype(BF)
    return _dot(hi, u) + _dot(mid, u) + _dot(lo, u)


def _cumsum_lanes(src_ref, dst_ref):
    u = _tri_upper(CUM_BLOCK)
    carry = jnp.zeros((8, 1), F32)
    for b in range(src_ref.shape[-1] // CUM_BLOCK):
        blk = slice(b * CUM_BLOCK, (b + 1) * CUM_BLOCK)
        c = _dot3(src_ref[:, blk], u) + carry
        dst_ref[:, blk] = c
        carry = c[:, CUM_BLOCK - 1:CUM_BLOCK]
    return carry


def _cumsum_kernel(x_ref, o_ref):
    _cumsum_lanes(x_ref, o_ref)


def _cumsum_rows(x):
    b, r, s = x.shape
    return pl.pallas_call(
        _cumsum_kernel,
        out_shape=jax.ShapeDtypeStruct(x.shape, F32),
        grid=(b,),
        in_specs=[pl.BlockSpec((None, r, s), lambda i: (i, 0, 0))],
        out_specs=pl.BlockSpec((None, r, s), lambda i: (i, 0, 0)),
        compiler_params=_cparams("arbitrary"),
        name="forget_cumsum",
    )(x)


def _relbias_kernel(tab_ref, o_ref, *, bases, rows, cols, dmin, mult):
    h = pl.program_id(0)
    r = lax.broadcasted_iota(jnp.int32, (rows, cols), 0)
    c = lax.broadcasted_iota(jnp.int32, (rows, cols), 1)
    for t, base in enumerate(bases):
        idx = jnp.clip(base + r - c, -REL_CLIP, REL_CLIP) + REL_CLIP
        lo = min(max(base - (cols - 1), max(dmin, -REL_CLIP)), REL_CLIP) + REL_CLIP
        hi = min(max(base + rows - 1, -REL_CLIP), REL_CLIP) + REL_CLIP

        def body(e, acc):
            return jnp.where(idx == e, tab_ref[h, e] * mult, acc)

        o_ref[t] = lax.fori_loop(lo, hi + 1, body, jnp.zeros((rows, cols), F32))


def _relbias(table, bases, rows, cols, dmin, mult=1.0):
    nh = table.shape[0]
    return pl.pallas_call(
        functools.partial(_relbias_kernel, bases=tuple(bases), rows=rows, cols=cols, dmin=dmin, mult=mult),
        out_shape=jax.ShapeDtypeStruct((nh, len(bases), rows, cols), F32),
        grid=(nh,),
        in_specs=[pl.BlockSpec(memory_space=pltpu.SMEM)],
        out_specs=pl.BlockSpec((None, len(bases), rows, cols), lambda h: (h, 0, 0, 0)),
        compiler_params=_cparams("arbitrary"),
        name="rel_bias_tiles",
    )(table)


def _diff_lambda(lam_ref, layer_idx):
    lam_init = 0.8 - 0.6 * math.exp(-0.3 * layer_idx)
    lp = lam_ref[...]
    a = jnp.sum(lp[0:1] * lp[1:2], keepdims=True)
    b = jnp.sum(lp[2:3] * lp[3:4], keepdims=True)
    return jnp.exp(a) - jnp.exp(b) + lam_init, lam_init


ALIBI_SLOPES = tuple(2.0 ** (-8.0 * (h + 1) / H) for h in range(H))


LOG2E = math.log2(math.e)
RG = 32


def _pattn_kernel(*refs, mode, tq, tk, nk, layer_idx, back):
    if mode != "c":
        qi_ref, kj_ref, *refs = refs
    if mode == "a":
        q_ref, k_ref, v_ref, o_ref, m_ref, acc_ref, s_scr, p_scr, pm_scr = refs
        dq, scale = A_QK_PAD, (A_NOPE + A_ROPE) ** -0.5
    elif mode == "b":
        q_ref, k_ref, v_ref, lam_ref, sg_ref, o_ref, m_ref, acc_ref, s_scr, p_scr, pm_scr = refs
        dq, scale = 2 * B_DH, B_DH ** -0.5
    elif mode == "c":
        q_ref, k_ref, v_ref, bias_ref, o_ref, m_ref, acc_ref, s_scr, p_scr, pm_scr = refs
        dq, scale = C_DH, C_DH ** -0.5
    else:
        q_ref, k_ref, v_ref, fq_ref, fk_ref, o_ref, m_ref, acc_ref, s_scr, p_scr, pm_scr, fq_scr = refs
        dq, scale = D_DH, D_DH ** -0.5
    assert tq == tk and CHUNK % RG == 0 and tk % LANES == 0
    dv = HW // H
    c1 = scale * LOG2E
    nchunk = tk // LANES
    if mode == "c":
        i = pl.program_id(1)
        j = pl.program_id(2)
    else:
        i = qi_ref[pl.program_id(1)]
        j = kj_ref[pl.program_id(1)]
    if mode == "c":
        kb = i - back + j
        valid = kb >= 0
        first = jnp.maximum(back - i, 0)
        last = nk - 1
    else:
        last = i
        kb = j
        valid = j <= last
        first = 0

    @pl.when(j == first)
    def _():
        m_ref[...] = jnp.full(m_ref.shape, -jnp.inf, F32)
        acc_ref[...] = jnp.zeros(acc_ref.shape, F32)
        if mode == "d":
            for h in range(H):
                fq_scr[h] = jnp.broadcast_to(fq_ref[:, h:h + 1] * LOG2E, (tq, LANES))

    lane = lax.broadcasted_iota(jnp.int32, (RG, LANES), 1)
    lane_minus_row = lane - lax.broadcasted_iota(jnp.int32, (RG, LANES), 0)

    def chunk_visibility(kind, r0, c):
        c0 = c * LANES
        if kind == "none":
            return "all"
        if kind == "lt":
            bound = (r0 // CHUNK + 1) * CHUNK
            if c0 >= bound:
                return "none"
            return "all" if c0 + LANES <= bound else lane < bound - c0
        if kind == "ge":
            bound = r0 // CHUNK * CHUNK
            if c0 + LANES <= bound:
                return "none"
            return "all" if c0 >= bound else lane >= bound - c0
        assert kind == "causal"
        if c0 > r0 + RG - 1:
            return "none"
        return "all" if c0 + LANES - 1 <= r0 else lane_minus_row <= r0 - c0

    def sweep1(idxs, h, g, kind, delta):
        r0 = g * RG
        rows = slice(r0, r0 + RG)
        pm = [None] * len(idxs)
        for c in range(nchunk):
            vis = chunk_visibility(kind, r0, c)
            if isinstance(vis, str) and vis == "none":
                continue
            cols = slice(c * LANES, (c + 1) * LANES)
            if mode == "b":
                dist = jnp.abs(lane_minus_row.astype(F32) - (delta + float(r0 - c * LANES)))
                bias = dist * (-ALIBI_SLOPES[h] * LOG2E)
            elif mode == "c":
                bias = bias_ref[h, rows, cols]
            elif mode == "d":
                bias = fq_scr[h, rows, :] - fk_ref[h:h + 1, cols] * LOG2E
            for n, idx in enumerate(idxs):
                t = s_scr[idx, rows, cols] * c1
                if mode != "a":
                    t = t + bias
                if not isinstance(vis, str):
                    t = jnp.where(vis, t, NEG_INF)
                s_scr[idx, rows, cols] = t
                pm[n] = t if pm[n] is None else jnp.maximum(pm[n], t)
        for n, idx in enumerate(idxs):
            pm_scr[idx, rows, :] = pm[n]

    def sweep2(idx, g, kind):
        r0 = g * RG
        rows = slice(r0, r0 + RG)
        m_rows = m_ref[idx, rows, :]
        for c in range(nchunk):
            cols = slice(c * LANES, (c + 1) * LANES)
            vis = chunk_visibility(kind, r0, c)
            if isinstance(vis, str) and vis == "none":
                p_scr[idx, rows, cols] = jnp.zeros((RG, LANES), BF)
            else:
                p_scr[idx, rows, cols] = jnp.exp2(s_scr[idx, rows, cols] - m_rows).astype(BF)

    def block(kind):
        delta = ((i - kb) * tq).astype(F32) if mode == "b" else None
        ones = jnp.ones((tk, LANES), BF)
        for h in range(H):
            q = q_ref[:, h * dq:(h + 1) * dq]
            k = k_ref[:, h * dq:(h + 1) * dq].astype(BF)
            v1 = jnp.concatenate([v_ref[:, h * dv:(h + 1) * dv].astype(BF), ones], axis=-1)
            if mode == "b":
                lo = lax.broadcasted_iota(jnp.int32, (1, dq), 1) < B_DH
                zero = jnp.zeros_like(q)
                s_scr[h] = _nt_dot(jnp.where(lo, q, zero), k)
                s_scr[H + h] = _nt_dot(jnp.where(lo, zero, q), k)
                idxs = (h, H + h)
            else:
                s_scr[h] = _nt_dot(q, k)
                idxs = (h,)
            for g in range(tq // RG):
                sweep1(idxs, h, g, kind, delta)
            for idx in idxs:
                m_prev = m_ref[idx]
                m_new = jnp.maximum(m_prev, jnp.max(pm_scr[idx], axis=-1, keepdims=True))
                m_ref[idx] = m_new
                alpha = jnp.exp2(m_prev - m_new)
                for g in range(tq // RG):
                    sweep2(idx, g, kind)
                acc_ref[idx] = jnp.concatenate([alpha, alpha], axis=-1) * acc_ref[idx] + _dot(p_scr[idx], v1)

    if mode == "c":
        assert back == 2 and BAND_ROWS == back * tk
        for jj, kind in enumerate(("ge", "none", "lt")):
            pl.when((j == jj) & valid)(functools.partial(block, kind))
    else:
        pl.when(j < last)(functools.partial(block, "none"))
        pl.when(j == last)(functools.partial(block, "causal" if mode == "d" else "lt"))

    @pl.when(j == last)
    def _():
        for h in range(H):
            o = acc_ref[h, :, :dv] / acc_ref[h, :, dv:]
            if mode == "b":
                lam, lam_init = _diff_lambda(lam_ref, layer_idx)
                o = o - lam * (acc_ref[H + h, :, :dv] / acc_ref[H + h, :, dv:])
                o = (_rms(o, dv) * sg_ref[...]) * (1.0 - lam_init)
            o_ref[:, h * dv:(h + 1) * dv] = o.astype(BF)


def _pattn(mode, q, k, v, extras, layer_idx=0):
    b, s, qw = q.shape
    kw = k.shape[-1]
    back = 0
    if mode == "c":
        tq = tk = 256
        back = BAND_ROWS // tk
        nk = back + 1
        kmap = lambda bi, i, j: (bi, jnp.maximum(i - back + j, 0), 0)
    else:
        tq = tk = 512
        nk = s // tk
    nstate = 2 * H if mode == "b" else H
    scratch = ([pltpu.VMEM((nstate, tq, LANES), F32),
                pltpu.VMEM((nstate, tq, 2 * (HW // H)), F32),
                pltpu.VMEM((nstate, tq, tk), F32),
                pltpu.VMEM((nstate, tq, tk), BF),
                pltpu.VMEM((nstate, tq, LANES), F32)]
               + ([pltpu.VMEM((H, tq, LANES), F32)] if mode == "d" else []))
    body = functools.partial(_pattn_kernel, mode=mode, tq=tq, tk=tk, nk=nk, layer_idx=layer_idx, back=back)
    out_shape = jax.ShapeDtypeStruct((b, s, HW), BF)
    if mode == "c":
        (bias,) = extras
        qmap = lambda bi, i, j: (bi, i, 0)
        return pl.pallas_call(
            body, out_shape=out_shape, grid=(b, s // tq, nk),
            in_specs=[pl.BlockSpec((None, tq, qw), qmap),
                      pl.BlockSpec((None, tk, kw), kmap),
                      pl.BlockSpec((None, tk, HW), kmap),
                      pl.BlockSpec((H, None, tq, tk), lambda bi, i, j: (0, j, 0, 0))],
            out_specs=pl.BlockSpec((None, tq, HW), qmap),
            scratch_shapes=scratch,
            compiler_params=_cparams("arbitrary", "arbitrary", "arbitrary"),
            name="prompt_attn_c",
        )(q, k, v, bias)
    pairs = [(i, j) for i in range(s // tq) for j in range(i + 1)]
    qi = jnp.asarray([p[0] for p in pairs], jnp.int32)
    kj = jnp.asarray([p[1] for p in pairs], jnp.int32)
    qmap = lambda bi, n, qi_ref, kj_ref: (bi, qi_ref[n], 0)
    kmap = lambda bi, n, qi_ref, kj_ref: (bi, kj_ref[n], 0)
    const = lambda bi, n, qi_ref, kj_ref: (0, 0)
    in_specs = [pl.BlockSpec((None, tq, qw), qmap),
                pl.BlockSpec((None, tk, kw), kmap),
                pl.BlockSpec((None, tk, HW), kmap)]
    if mode == "b":
        lam, sg = extras
        in_specs += [pl.BlockSpec(lam.shape, const), pl.BlockSpec(sg.shape, const)]
    elif mode == "d":
        in_specs += [pl.BlockSpec((None, tq, H), qmap),
                     pl.BlockSpec((None, 8, tk), lambda bi, n, qi_ref, kj_ref: (bi, 0, kj_ref[n]))]
    return pl.pallas_call(
        body, out_shape=out_shape,
        grid_spec=pltpu.PrefetchScalarGridSpec(
            num_scalar_prefetch=2, grid=(b, len(pairs)), in_specs=in_specs,
            out_specs=pl.BlockSpec((None, tq, HW), qmap), scratch_shapes=scratch),
        compiler_params=_cparams("arbitrary", "arbitrary"),
        name="prompt_attn_" + mode,
    )(qi, kj, q, k, v, *extras)


def _pad_rows(x, rows):
    return jnp.concatenate([x, jnp.zeros((rows - x.shape[0], x.shape[1]), x.dtype)], axis=0)


def _prefetch_head_caches(k_hbm, v_hbm, kbuf, vbuf, sem, layer, nb):
    b = pl.program_id(0)
    slot = b % 2
    dh = kbuf.shape[-1] // H

    def copies(bidx, slot_):
        out = []
        for h in range(H):
            cols = pl.ds(h * dh, dh)
            out.append(pltpu.make_async_copy(k_hbm.at[layer, bidx, :, h, :], kbuf.at[slot_, :, cols],
                                             sem.at[slot_, 0]))
            out.append(pltpu.make_async_copy(v_hbm.at[layer, bidx, :, h, :], vbuf.at[slot_, :, cols],
                                             sem.at[slot_, 1]))
        return out

    @pl.when(b == 0)
    def _():
        for c in copies(0, 0):
            c.start()

    @pl.when(b + 1 < nb)
    def _():
        for c in copies(b + 1, 1 - slot):
            c.start()

    for c in copies(b, slot):
        c.wait()
    return slot


def _sattn_kernel(*refs, mode, t, past, layer, nb):
    if mode == "a":
        (q_ref, lat_ref, kpet_ref, kn_ref, vn_ref, wukv_ref, gkn_ref, o_ref) = refs
        dq, scale = A_QK_PAD, (A_NOPE + A_ROPE) ** -0.5
        rows_c = lat_ref.shape[0]
    else:
        if mode == "b":
            (q_ref, k_hbm, v_hbm, kn_ref, vn_ref, lam_ref, sg_ref, o_ref, kbuf, vbuf, sem) = refs
            dq, scale = 2 * B_DH, B_DH ** -0.5
        elif mode == "c":
            (q_ref, k_hbm, v_hbm, kn_ref, vn_ref, bias_ref, o_ref, kbuf, vbuf, sem) = refs
            dq, scale = C_DH, C_DH ** -0.5
        else:
            (q_ref, k_hbm, v_hbm, kn_ref, vn_ref, lfc_ref, lfn_ref, o_ref, f_scr, kbuf, vbuf, sem) = refs
            dq, scale = D_DH, D_DH ** -0.5
        rows_c = kbuf.shape[1]
        slot = _prefetch_head_caches(k_hbm, v_hbm, kbuf, vbuf, sem, layer, nb)
        kc_ref, vc_ref = kbuf.at[slot], vbuf.at[slot]
    dv = HW // H
    npad = LANES
    tt = 2 * t if mode == "b" else t

    def query_index(shape):
        r = lax.broadcasted_iota(jnp.int32, shape, 0)
        return r if tt == t else jnp.where(r >= t, r - t, r)

    qpos_c = past + query_index((tt, rows_c))
    kpos_c = (past - rows_c) + lax.broadcasted_iota(jnp.int32, (tt, rows_c), 1)
    qpos_n = past + query_index((tt, npad))
    col_n = lax.broadcasted_iota(jnp.int32, (tt, npad), 1)
    kpos_n = past + col_n
    real_n = col_n < t
    if mode in ("a", "b"):
        mask_c = _chunk(kpos_c) <= _chunk(qpos_c)
        mask_n = real_n & (_chunk(kpos_n) <= _chunk(qpos_n))
    elif mode == "c":
        qc_c, kc_c = _chunk(qpos_c), _chunk(kpos_c)
        qc_n, kc_n = _chunk(qpos_n), _chunk(kpos_n)
        mask_c = (kpos_c >= 0) & (kc_c <= qc_c) & (kc_c >= qc_c - BAND_CHUNKS)
        mask_n = real_n & (kc_n <= qc_n) & (kc_n >= qc_n - BAND_CHUNKS)
    else:
        mask_c = kpos_c <= qpos_c
        mask_n = real_n & (kpos_n <= qpos_n)

    if mode == "a":
        lat = lat_ref[...].astype(BF)
        kpe_t = kpet_ref[...]
        kpe_t = jnp.concatenate([kpe_t, jnp.zeros_like(kpe_t)], axis=0).astype(BF)
    if mode == "b":
        dist_c = jnp.abs(qpos_c - kpos_c).astype(F32)
        dist_n = jnp.abs(qpos_n - kpos_n).astype(F32)
        lo = lax.broadcasted_iota(jnp.int32, (1, dq), 1) < B_DH
        lam, lam_init = _diff_lambda(lam_ref, layer)
    if mode == "d":
        carry = _cumsum_lanes(lfc_ref, f_scr)
        f_new = _dot3(lfn_ref[...], _tri_upper(LANES)) + carry
        eye = (lax.broadcasted_iota(jnp.int32, (t, npad), 0) == col_n)

    def softmax_pv(s_c, s_n, v_c, v_n):
        m = jnp.maximum(jnp.max(s_c, axis=-1, keepdims=True), jnp.max(s_n, axis=-1, keepdims=True))
        p_c = jnp.exp(s_c - m)
        p_n = jnp.exp(s_n - m)
        l = jnp.sum(p_c, axis=-1, keepdims=True) + jnp.sum(p_n, axis=-1, keepdims=True)
        return (_dot(p_c.astype(BF), v_c) + _dot(p_n.astype(BF), v_n)) / l

    for h in range(H):
        q = q_ref[:, h * dq:(h + 1) * dq]
        if mode == "a":
            kn = _pad_rows(kn_ref[:, h * dq:(h + 1) * dq], npad)
            vn = _pad_rows(vn_ref[:, h * dv:(h + 1) * dv], npad)
            w = wukv_ref[...]
            k_nope = _dot(lat, w[:, h * A_NOPE:(h + 1) * A_NOPE])
            k_nope = (_rms(k_nope, A_NOPE) * gkn_ref[...]).astype(BF)
            v_c = _dot(lat, w[:, H * A_NOPE + h * A_VDIM:H * A_NOPE + (h + 1) * A_VDIM]).astype(BF)
            s_c = (_nt_dot(q[:, :A_NOPE], k_nope) + _dot(q[:, A_NOPE:], kpe_t)) * scale
            s_n = _nt_dot(q, kn) * scale
        else:
            k_c = kc_ref[:, h * dq:(h + 1) * dq].astype(BF)
            v_c = vc_ref[:, h * dv:(h + 1) * dv].astype(BF)
            kn = _pad_rows(kn_ref[:, h * dq:(h + 1) * dq], npad).astype(BF)
            vn = _pad_rows(vn_ref[:, h * dv:(h + 1) * dv], npad).astype(BF)
        if mode == "b":
            zero = jnp.zeros_like(q)
            q12 = jnp.concatenate([jnp.where(lo, q, zero), jnp.where(lo, zero, q)], axis=0)
            s_c = _nt_dot(q12, k_c) * scale + (-ALIBI_SLOPES[h]) * dist_c
            s_n = _nt_dot(q12, kn) * scale + (-ALIBI_SLOPES[h]) * dist_n
            o12 = softmax_pv(jnp.where(mask_c, s_c, NEG_INF), jnp.where(mask_n, s_n, NEG_INF), v_c, vn)
            o = o12[:t] - lam * o12[t:]
            o = (_rms(o, dv) * sg_ref[...]) * (1.0 - lam_init)
        else:
            if mode != "a":
                s_c = _nt_dot(q, k_c) * scale
                s_n = _nt_dot(q, kn) * scale
            if mode == "c":
                s_c = s_c + bias_ref[h, :, :rows_c]
                s_n = s_n + bias_ref[h, :, rows_c:]
            elif mode == "d":
                fq = jnp.sum(jnp.where(eye, f_new[h:h + 1, :], 0.0), axis=-1, keepdims=True)
                s_c = s_c + (fq - f_scr[h:h + 1, :])
                s_n = s_n + (fq - f_new[h:h + 1, :])
            s_c = jnp.where(mask_c, s_c, NEG_INF)
            s_n = jnp.where(mask_n, s_n, NEG_INF)
            o = softmax_pv(s_c, s_n, v_c, vn)
        o_ref[:, h * dv:(h + 1) * dv] = o.astype(BF)


def _sattn(mode, layer, q, cache_k, cache_v, kn, vn, extras, t, past):
    nb = cache_k.shape[1]
    qw = q.shape[-1]
    row = lambda b: (b, 0)
    if mode == "a":
        cache_specs = [pl.BlockSpec((None, None) + cache_k.shape[2:], lambda b: (layer, b, 0, 0)),
                       pl.BlockSpec((None, None) + cache_v.shape[2:], lambda b: (layer, b, 0, 0))]
        dma_scratch = []
    else:
        rows_c = cache_k.shape[2]
        cache_specs = [pl.BlockSpec(memory_space=pl.ANY), pl.BlockSpec(memory_space=pl.ANY)]
        dma_scratch = [pltpu.VMEM((2, rows_c, HW), F32), pltpu.VMEM((2, rows_c, HW), F32),
                       pltpu.SemaphoreType.DMA((2, 2))]
    in_specs = [pl.BlockSpec((t, qw), row)] + cache_specs + [
                pl.BlockSpec((t, kn.shape[-1]), row),
                pl.BlockSpec((t, vn.shape[-1]), row)]
    scratch = []
    if mode == "d":
        lfc, lfn = extras
        in_specs += [pl.BlockSpec((None,) + lfc.shape[1:], lambda b: (b, 0, 0)),
                     pl.BlockSpec((None,) + lfn.shape[1:], lambda b: (b, 0, 0))]
        scratch = [pltpu.VMEM(lfc.shape[1:], F32)]
    elif mode == "c":
        (bias,) = extras
        in_specs += [pl.BlockSpec(bias.shape, lambda b: (0, 0, 0))]
    else:
        in_specs += [pl.BlockSpec(e.shape, lambda b: (0, 0)) for e in extras]
    return pl.pallas_call(
        functools.partial(_sattn_kernel, mode=mode, t=t, past=past, layer=layer, nb=nb),
        out_shape=jax.ShapeDtypeStruct((nb * t, HW), BF),
        grid=(nb,),
        in_specs=in_specs,
        out_specs=pl.BlockSpec((t, HW), row),
        scratch_shapes=scratch + dma_scratch,
        compiler_params=_cparams("arbitrary"),
        name="sample_attn_" + mode,
    )(q, cache_k, cache_v, kn, vn, *extras)


def _gmerge_kernel(h_ref, oa_ref, ob_ref, oc_ref, od_ref, wg0_ref, wg1_ref, wg2_ref, wg3_ref,
                   wb_ref, out_ref):
    hb = h_ref[...]
    acc = None
    for g, (o_ref, wg_ref) in enumerate(((oa_ref, wg0_ref), (ob_ref, wg1_ref),
                                         (oc_ref, wg2_ref), (od_ref, wg3_ref))):
        term = _sigmoid(_dot(hb, wg_ref[...])) * _dot(o_ref[...], wb_ref[g])
        acc = term if acc is None else acc + term
    out_ref[...] = acc.astype(BF)


def _gmerge(hb, outs, wg, wb, layer, tm):
    m, d = hb.shape
    tn = WIN_TILE
    nt = d // tn
    row = lambda i, n: (i, 0)
    wg_specs = [pl.BlockSpec((d, tn), functools.partial(lambda i, n, g: (0, PROJ_TILES + g * nt + n), g=g))
                for g in range(N_BRANCH)]
    return pl.pallas_call(
        _gmerge_kernel,
        out_shape=jax.ShapeDtypeStruct((m, d), BF),
        grid=(m // tm, nt),
        in_specs=[pl.BlockSpec((tm, d), row)] + [pl.BlockSpec((tm, HW), row)] * N_BRANCH + wg_specs
                 + [pl.BlockSpec((None, N_BRANCH, HW, tn), lambda i, n: (layer, 0, 0, n))],
        out_specs=pl.BlockSpec((tm, tn), lambda i, n: (i, n)),
        compiler_params=_cparams("arbitrary", "arbitrary"),
        name="gate_merge",
    )(hb, *outs, wg, wg, wg, wg, wb)


def _oproj_kernel(mix_ref, x_ref, g_ref, w_ref, o_ref):
    o_ref[...] = x_ref[...] + g_ref[...] * _dot(mix_ref[...], w_ref[...])


def _oproj(mix, x2d, gate, w, layer, tm):
    m, d = x2d.shape
    rows = gate.shape[1]
    tiles_per_group = (m // gate.shape[0]) // tm
    row = lambda i: (i, 0)
    return pl.pallas_call(
        _oproj_kernel,
        out_shape=jax.ShapeDtypeStruct((m, d), F32),
        grid=(m // tm,),
        in_specs=[pl.BlockSpec((tm, d), row), pl.BlockSpec((tm, d), row),
                  pl.BlockSpec((None, rows, d), lambda i: (i // tiles_per_group, 0, 0)),
                  pl.BlockSpec((None, d, d), lambda i: (layer, 0, 0))],
        out_specs=pl.BlockSpec((tm, d), row),
        compiler_params=_cparams("arbitrary"),
        name="out_proj",
    )(mix, x2d, gate, w)


def _ffn_kernel(x_ref, sc_ref, sh_ref, g_ref, ng_ref, wu_ref, wd_ref, o_ref, h_scr, *, nf):
    f = pl.program_id(1)

    @pl.when(f == 0)
    def _():
        x = x_ref[...]
        y = _rms(x, x.shape[-1]) * ng_ref[...]
        h_scr[...] = (y * (1.0 + sc_ref[...]) + sh_ref[...]).astype(BF)
        o_ref[...] = jnp.zeros(o_ref.shape, F32)

    u = jnp.maximum(_dot(h_scr[...], wu_ref[...]), 0.0)
    o_ref[...] += _dot((u * u).astype(BF), wd_ref[...])

    @pl.when(f == nf - 1)
    def _():
        o_ref[...] = x_ref[...] + g_ref[...] * o_ref[...]


def _ffn(x2d, sc, sh, gate, ng, wu, wd, layer, tm):
    m, d = x2d.shape
    dff = wu.shape[2]
    tf = 512
    rows = sc.shape[1]
    tiles_per_group = (m // sc.shape[0]) // tm
    row = lambda i, f: (i, 0)
    mod_spec = pl.BlockSpec((None, rows, d), lambda i, f: (i // tiles_per_group, 0, 0))
    return pl.pallas_call(
        functools.partial(_ffn_kernel, nf=dff // tf),
        out_shape=jax.ShapeDtypeStruct((m, d), F32),
        grid=(m // tm, dff // tf),
        in_specs=[pl.BlockSpec((tm, d), row), mod_spec, mod_spec, mod_spec,
                  pl.BlockSpec((1, d), lambda i, f: (0, 0)),
                  pl.BlockSpec((None, d, tf), lambda i, f: (layer, 0, f)),
                  pl.BlockSpec((None, tf, d), lambda i, f: (layer, f, 0))],
        out_specs=pl.BlockSpec((tm, d), row),
        scratch_shapes=[pltpu.VMEM((tm, d), BF)],
        compiler_params=_cparams("arbitrary", "arbitrary"),
        name="ffn",
    )(x2d, sc, sh, gate, ng, wu, wd)


def _pack_layer(l, p, w_in_t):
    w_all = _win_prep(w_in_t, l, w_in_t.shape[2])

    qk = A_NOPE + A_ROPE
    wuq = p["w_a_uq"][l].reshape(A_Q_RANK, H, qk)
    wuq = jnp.pad(wuq, ((0, 0), (0, 0), (0, A_QK_PAD - qk))).reshape(A_Q_RANK, H * A_QK_PAD).astype(BF)
    wukv = p["w_a_ukv"][l].reshape(A_KV_RANK, H, A_NOPE + A_VDIM)
    wukv = jnp.concatenate([wukv[:, :, :A_NOPE].reshape(A_KV_RANK, H * A_NOPE),
                            wukv[:, :, A_NOPE:].reshape(A_KV_RANK, H * A_VDIM)], axis=1).astype(BF)

    def row(v, width=None):
        v = v.reshape(1, -1).astype(F32)
        if width is not None and v.shape[1] < width:
            v = jnp.pad(v, ((0, 0), (0, width - v.shape[1])))
        return v

    def head_rows(gq, gk, reps):
        return jnp.stack([jnp.tile(gq, reps), jnp.tile(gk, reps)]).astype(F32)

    return dict(
        w_all=w_all, wuq=wuq, wukv=wukv,
        ng1=row(p["norm1_g"][l]), ng2=row(p["norm2_g"][l]),
        gq=row(p["a_q_norm_g"][l]), gkv=row(p["a_kv_norm_g"][l]),
        gkpe=row(p["a_k_gain"][l][A_NOPE:], LANES),
        gqn=row(p["a_q_gain"][l][:A_NOPE]), gqr=row(p["a_q_gain"][l][A_NOPE:], LANES),
        gkn=row(p["a_k_gain"][l][:A_NOPE]),
        gb=head_rows(p["b_q_gain"][l], p["b_k_gain"][l], HW // B_DH),
        gc=head_rows(p["c_q_gain"][l], p["c_k_gain"][l], H),
        gd=head_rows(p["d_q_gain"][l], p["d_k_gain"][l], H),
        fb=row(p["d_forget_b"][l], LANES),
        lam=p["b_lambda"][l].astype(F32), sg=row(p["b_subln_g"][l]),
        rel=p["c_rel_bias"][l].astype(F32),
        wb=p["wb16"], wo=p["wo16"], wu=p["wu16"], wd=p["wd16"], layer=l,
    )


def _rope_tables(pos):
    half = A_ROPE // 2
    inv = ROPE_THETA ** (-jnp.arange(half, dtype=F32) / half)
    ang = pos.astype(F32)[:, None] * inv[None, :]
    cos, sin = jnp.cos(ang), jnp.sin(ang)
    z = jnp.zeros((pos.shape[0], LANES - A_ROPE), F32)
    return jnp.concatenate([cos, cos, z], axis=1), jnp.concatenate([-sin, sin, z], axis=1)


def _heads_to_rows(x, lanes):
    b, t, h = x.shape
    return jnp.pad(jnp.swapaxes(x, 1, 2), ((0, 0), (0, 8 - h), (0, lanes - t)))


def _front(x2d, mods, pk, tabs, tm, l, depth, kv_states, keep_c, rows_per_batch):
    sh1, sc1 = mods[0], mods[1]
    cos_t, sin_t = tabs
    hb, cqn, ckvn, kpe, kpep, logf = _proj0(
        x2d, sc1, sh1, pk["ng1"], pk["w_all"], pk["gq"], pk["gkv"], pk["gkpe"], pk["fb"], cos_t, sin_t, tm)
    f = dict(h=hb, ckv=ckvn, kpe=kpe, logf=logf)
    new_states = {}
    for mixer, first_tile, gs, keep in (("b", 2, B_DH, None), ("c", 5, C_DH, keep_c), ("d", 8, D_DH, None)):
        prev = None if kv_states is None else kv_states[mixer]
        q, k, v, ks, vs = _projg(hb, pk["w_all"], pk["g" + mixer], first_tile, gs, tm, "in_proj_" + mixer,
                                 l, depth, prev, keep, rows_per_batch)
        f["q" + mixer], f["k" + mixer], f["v" + mixer] = q, k, v
        new_states[mixer] = (ks, vs)
    f["qa"], f["ka"], f["va"] = _aprep(cqn, ckvn, kpep, pk["wuq"], pk["wukv"], pk["gqn"], pk["gqr"],
                                       pk["gkn"], cos_t, sin_t, tm)
    return f, new_states


def _back(x2d, hb, outs, mods, pk, tm):
    g1, sh2, sc2, g2 = mods[2], mods[3], mods[4], mods[5]
    l = pk["layer"]
    mix = _gmerge(hb, outs, pk["w_all"], pk["wb"], l, tm)
    x2d = _oproj(mix, x2d, g1, pk["wo"], l, min(tm, 512))
    return _ffn(x2d, sc2, sh2, g2, pk["ng2"], pk["wu"], pk["wd"], l, tm)


def _prompt_layer(x2d, mods, pk, tabs, bias_tiles, b, s, l, depth, kv_states):
    tm = min(1024, s)
    f, kv_states = _front(x2d, mods, pk, tabs, tm, l, depth, kv_states, min(BAND_ROWS, s), s)
    r3 = lambda a: a.reshape(b, s, a.shape[-1])
    o_a = _pattn("a", r3(f["qa"]), r3(f["ka"]), r3(f["va"]), ())
    o_b = _pattn("b", r3(f["qb"]), r3(f["kb"]), r3(f["vb"]), (pk["lam"], pk["sg"]), l)
    o_c = _pattn("c", r3(f["qc"]), r3(f["kc"]), r3(f["vc"]), (bias_tiles,))
    frow = _cumsum_rows(_heads_to_rows(r3(f["logf"]), s))
    fcol = jnp.swapaxes(frow[:, :H, :], 1, 2)
    o_d = _pattn("d", r3(f["qd"]), r3(f["kd"]), r3(f["vd"]), (fcol, frow))
    outs = [o.reshape(b * s, HW) for o in (o_a, o_b, o_c, o_d)]
    x2d = _back(x2d, f["h"], outs, mods, pk, tm)
    return x2d, (r3(f["ckv"]), r3(f["kpe"]), r3(f["logf"])), kv_states


def _sample_layer(x2d, mods, pk, tabs, bias_tiles, caches, nb, t, l, depth, kv_states):
    lat_c, kpe_t, kb_c, vb_c, kc_c, vc_c, kd_c, vd_c, logf_c = caches
    past = lat_c.shape[2]
    tm = nb * t
    f, kv_states = _front(x2d, mods, pk, tabs, tm, l, depth, kv_states, None, t)
    o_a = _sattn("a", l, f["qa"], lat_c, kpe_t, f["ka"], f["va"], (pk["wukv"], pk["gkn"]), t, past)
    o_b = _sattn("b", l, f["qb"], kb_c, vb_c, f["kb"], f["vb"], (pk["lam"], pk["sg"]), t, past)
    o_c = _sattn("c", l, f["qc"], kc_c, vc_c, f["kc"], f["vc"], (bias_tiles,), t, past)
    lfc = _heads_to_rows(logf_c[l].astype(F32), past)
    lfn = _heads_to_rows(f["logf"].reshape(nb, t, H), LANES)
    o_d = _sattn("d", l, f["qd"], kd_c, vd_c, f["kd"], f["vd"], (lfc, lfn), t, past)
    x2d = _back(x2d, f["h"], [o_a, o_b, o_c, o_d], mods, pk, tm)
    r3 = lambda a: a.reshape(nb, t, a.shape[-1])
    return x2d, (r3(f["ckv"]), r3(f["kpe"]), r3(f["logf"])), kv_states


def kernel(x_prompt, x_sample, c_prompt, c_sample,
           cache_a_latent, cache_a_kpe, cache_b_k, cache_b_v, cache_c_k, cache_c_v,
           cache_d_k, cache_d_v, cache_d_logf,
           norm1_g, norm2_g, w_ada, b_ada, w_in,
           a_q_norm_g, a_kv_norm_g, w_a_uq, w_a_ukv, a_q_gain, a_k_gain,
           b_q_gain, b_k_gain, b_lambda, b_subln_g,
           c_q_gain, c_k_gain, c_rel_bias,
           d_q_gain, d_k_gain, d_forget_b,
           w_branch, w_out, w_up, w_down):
    params = dict(norm1_g=norm1_g, norm2_g=norm2_g, w_in=w_in, a_q_norm_g=a_q_norm_g,
                  a_kv_norm_g=a_kv_norm_g, w_a_uq=w_a_uq, w_a_ukv=w_a_ukv, a_q_gain=a_q_gain,
                  a_k_gain=a_k_gain, b_q_gain=b_q_gain, b_k_gain=b_k_gain, b_lambda=b_lambda,
                  b_subln_g=b_subln_g, c_q_gain=c_q_gain, c_k_gain=c_k_gain, c_rel_bias=c_rel_bias,
                  d_q_gain=d_q_gain, d_k_gain=d_k_gain, d_forget_b=d_forget_b,
                  wb16=w_branch.astype(BF), wo16=w_out.astype(BF), wu16=w_up.astype(BF),
                  wd16=w_down.astype(BF))
    depth = w_in.shape[0]
    b, s, d = x_prompt.shape
    nb, t, _ = x_sample.shape
    past = cache_a_latent.shape[2]
    rows_c = cache_c_k.shape[2]

    n_c = b + nb
    c_all = jnp.pad(jnp.concatenate([c_prompt, c_sample], axis=0), ((0, (-n_c) % 8), (0, 0)))
    mod_all = _ada(c_all, w_ada, b_ada)

    tabs_p = _rope_tables(jnp.arange(s, dtype=jnp.int32))
    tabs_s = _rope_tables(jnp.tile(past + jnp.arange(t, dtype=jnp.int32), nb))

    tile = 256
    p_bases = [(BAND_ROWS // tile - k) * tile for k in range(BAND_ROWS // tile + 1)]

    x_p = x_prompt.reshape(b * s, d)
    x_s = x_sample.reshape(nb * t, d)
    w_in_t = jnp.transpose(w_in, (2, 0, 1))
    caches = (cache_a_latent, jnp.swapaxes(cache_a_kpe, 2, 3), cache_b_k, cache_b_v, cache_c_k,
              cache_c_v, cache_d_k, cache_d_v, cache_d_logf)
    states_p, states_s = [], []
    kv_p = kv_s = None
    for l in range(depth):
        pk = _pack_layer(l, params, w_in_t)
        mod = mod_all[l]
        mods_p = [m.reshape(b, 1, d) for m in jnp.split(mod[:b], 6, axis=-1)]
        mods_s = [jnp.repeat(m, t, axis=0).reshape(1, nb * t, d) for m in jnp.split(mod[b:n_c], 6, axis=-1)]
        bias_p = _relbias(pk["rel"], p_bases, tile, tile, -(CHUNK - 1), LOG2E)
        bias_s = jnp.concatenate(
            [_relbias(pk["rel"], [rows_c], t, rows_c, -REL_CLIP)[:, 0],
             _relbias(pk["rel"], [0], t, LANES, -REL_CLIP)[:, 0]], axis=-1)
        x_p, st_p, kv_p = _prompt_layer(x_p, mods_p, pk, tabs_p, bias_p, b, s, l, depth, kv_p)
        x_s, st_s, kv_s = _sample_layer(x_s, mods_s, pk, tabs_s, bias_s, caches, nb, t, l, depth, kv_s)
        states_p.append(st_p)
        states_s.append(st_s)
    lat_p, kpe_p, logf_p = [jnp.stack(z) for z in zip(*states_p)]
    lat_s, kpe_s, logf_s = [jnp.stack(z) for z in zip(*states_s)]

    def kv_out(states, mixer, which, nbatch):
        a = states[mixer][which]
        return a.reshape(depth, nbatch, a.shape[1] // nbatch, H, a.shape[-1])

    out = [x_p.reshape(b, s, d), x_s.reshape(nb, t, d), lat_p, lat_s, kpe_p, kpe_s]
    for mixer in ("b", "c", "d"):
        for which in (0, 1):
            out += [kv_out(kv_p, mixer, which, b), kv_out(kv_s, mixer, which, nb)]
    out += [logf_p, logf_s]
    return tuple(out)
```
